```python
import math
import jax
import jax.numpy as jnp
from jax import lax
import numpy as np

D_MODEL = 2048
BATCH = 8
SEQ = 8192
DEPTH = 2

CHUNK = 64
Q_BLOCK = 128
EPS = 1e-6
PLE_DIM = 256
D_FF = 5504
SSD_WIDTH = D_MODEL
SSD_HEAD_DIM = 64
SSD_HEADS = SSD_WIDTH // SSD_HEAD_DIM
SSD_GROUPS = 4
SSD_HPG = SSD_HEADS // SSD_GROUPS
SSD_STATE = 128
SSD_CONV = 4
SSD_CONV_DIM = SSD_WIDTH + 2 * SSD_GROUPS * SSD_STATE
HGRN_WIDTH = D_MODEL
HGRN_KDIM = 128
HGRN_HEADS = HGRN_WIDTH // HGRN_KDIM
HGRN_VDIM = HGRN_WIDTH // HGRN_HEADS
FOX_WIDTH = D_MODEL
FOX_HEAD_DIM = 128
FOX_HEADS = FOX_WIDTH // FOX_HEAD_DIM

N_EVEN = (DEPTH + 1) // 2
N_ODD = DEPTH // 2
AB_SPLITS = (
    SSD_WIDTH,
    SSD_WIDTH + SSD_CONV_DIM,
    SSD_WIDTH + SSD_CONV_DIM + SSD_HEADS,
    SSD_WIDTH + SSD_CONV_DIM + SSD_HEADS + HGRN_WIDTH,
    SSD_WIDTH + SSD_CONV_DIM + SSD_HEADS + 2 * HGRN_WIDTH,
    SSD_WIDTH + SSD_CONV_DIM + SSD_HEADS + 3 * HGRN_WIDTH,
)
AB_IN = SSD_WIDTH + SSD_CONV_DIM + SSD_HEADS + 4 * HGRN_WIDTH
AB_OUT = SSD_WIDTH + HGRN_WIDTH
FOX_IN = 3 * FOX_WIDTH + FOX_HEADS

kernel_name = "hybrid_ssd_hgrn2_fox_macaron_trunk"


def rmsnorm(x, w):
    xf = x.astype(jnp.float32)
    y = xf * lax.rsqrt(jnp.mean(xf * xf, axis=-1, keepdims=True) + EPS)
    return (y * w.astype(jnp.float32)).astype(x.dtype)


def swiglu_half(h, norm_w, w_in, w_out):
    gate, up = jnp.split(rmsnorm(h, norm_w) @ w_in, 2, axis=-1)
    return h + 0.5 * ((jax.nn.silu(gate) * up) @ w_out)


def causal_dwconv(x, w, b):
    k = w.shape[0]
    y = lax.conv_general_dilated(
        x, w[:, None, :].astype(x.dtype), window_strides=(1,), padding=[(k - 1, 0)],
        dimension_numbers=("NWC", "WIO", "NWC"), feature_group_count=x.shape[-1])
    return y + b.astype(x.dtype)


def chunk_state_scan(states, decay):
    def step(carry, inp):
        s_c, a_c = inp
        return carry * a_c + s_c, carry
    init = jnp.zeros_like(states[:, 0])
    _, prev = lax.scan(step, init, (jnp.moveaxis(states, 1, 0), jnp.moveaxis(decay, 1, 0)))
    return jnp.moveaxis(prev, 0, 1)


def ssd_scan(xs, b_in, c_in, dt, a_log, d_skip):
    bsz, t = xs.shape[:2]
    nc = t // CHUNK
    dtype = xs.dtype
    x = xs.reshape(bsz, nc, CHUNK, SSD_GROUPS, SSD_HPG, SSD_HEAD_DIM)
    bm = b_in.reshape(bsz, nc, CHUNK, SSD_GROUPS, SSD_STATE)
    cm = c_in.reshape(bsz, nc, CHUNK, SSD_GROUPS, SSD_STATE)
    dtc = dt.reshape(bsz, nc, CHUNK, SSD_GROUPS, SSD_HPG)
    a = -jnp.exp(a_log.astype(jnp.float32)).reshape(SSD_GROUPS, SSD_HPG)
    cum = jnp.cumsum(dtc * a, axis=2)
    xdt = (x * dtc[..., None]).astype(dtype)
    cum_t = jnp.moveaxis(cum, 2, -1)
    causal = jnp.tril(jnp.ones((CHUNK, CHUNK), dtype=bool))
    decay = jnp.exp(jnp.where(causal, cum_t[..., :, None] - cum_t[..., None, :], -jnp.inf))
    cb = jnp.einsum("bclgn,bcsgn->bcgls", cm, bm).astype(jnp.float32)
    w = (cb[:, :, :, None] * decay).astype(dtype)
    y_diag = jnp.einsum("bcghls,bcsghp->bclghp", w, xdt)
    to_end = jnp.exp(cum[:, :, -1:] - cum)
    states = jnp.einsum("bclgn,bclghp->bcghpn", bm, (xdt * to_end[..., None]).astype(dtype))
    chunk_decay = jnp.exp(cum[:, :, -1]).astype(dtype)[..., None, None]
    prev = chunk_state_scan(states, chunk_decay)
    y_off = jnp.einsum("bclgn,bcghpn->bclghp", cm, prev) * jnp.exp(cum)[..., None].astype(dtype)
    y = y_diag + y_off + x * d_skip.reshape(SSD_GROUPS, SSD_HPG, 1).astype(dtype)
    return y.reshape(bsz, t, SSD_WIDTH)


def hgrn2_scan(q, f_raw, v, lb):
    bsz, t = q.shape[:2]
    nc = t // CHUNK
    dtype = q.dtype
    shp = (bsz, nc, CHUNK, HGRN_HEADS, HGRN_KDIM)
    lbh = lb.astype(jnp.float32).reshape(HGRN_HEADS, HGRN_KDIM)
    f = lbh + (1.0 - lbh) * jax.nn.sigmoid(f_raw.astype(jnp.float32))
    k = (1.0 - f).reshape(shp)
    cum = jnp.cumsum(jnp.log(f).reshape(shp), axis=2)
    qf = jax.nn.silu(q.astype(jnp.float32)).reshape(shp)
    vc = v.reshape(bsz, nc, CHUNK, HGRN_HEADS, HGRN_VDIM)
    mid = cum[:, :, CHUNK // 2 - 1:CHUNK // 2]
    q_rel = (qf * jnp.exp(cum - mid)).astype(dtype)
    k_rel = (k * jnp.exp(mid - cum)).astype(dtype)
    causal = jnp.tril(jnp.ones((CHUNK, CHUNK), dtype=bool))
    att = jnp.where(causal, jnp.einsum("bclhk,bcshk->bchls", q_rel, k_rel), 0)
    o_intra = jnp.einsum("bchls,bcshv->bclhv", att, vc)
    k_end = (k * jnp.exp(cum[:, :, -1:] - cum)).astype(dtype)
    states = jnp.einsum("bclhk,bclhv->bchkv", k_end, vc)
    chunk_decay = jnp.exp(cum[:, :, -1]).astype(dtype)[..., None]
    prev = chunk_state_scan(states, chunk_decay)
    o_inter = jnp.einsum("bclhk,bchkv->bclhv", (qf * jnp.exp(cum)).astype(dtype), prev)
    return (o_intra + o_inter).reshape(bsz, t, HGRN_HEADS, HGRN_VDIM)


def ssd_hgrn_mixer(hn, w_in, conv_w, conv_b, dt_bias, a_log, d_skip, ssd_norm_w, lb, hgrn_norm_w, w_out):
    bsz, t, _ = hn.shape
    z, xbc, dt_raw, q, f_raw, v, g = jnp.split(hn @ w_in, AB_SPLITS, axis=-1)
    xbc = jax.nn.silu(causal_dwconv(xbc, conv_w, conv_b))
    xs, b_in, c_in = jnp.split(xbc, [SSD_WIDTH, SSD_WIDTH + SSD_GROUPS * SSD_STATE], axis=-1)
    dt = jax.nn.softplus(dt_raw.astype(jnp.float32) + dt_bias.astype(jnp.float32))
    y_a = ssd_scan(xs, b_in, c_in, dt, a_log, d_skip) * jax.nn.silu(z)
    grp = SSD_WIDTH // SSD_GROUPS
    y_a = rmsnorm(y_a.reshape(bsz, t, SSD_GROUPS, grp), ssd_norm_w.reshape(SSD_GROUPS, grp)).reshape(bsz, t, SSD_WIDTH)
    y_b = hgrn2_scan(q.reshape(bsz, t, HGRN_HEADS, HGRN_KDIM), f_raw.reshape(bsz, t, HGRN_HEADS, HGRN_KDIM),
                     v.reshape(bsz, t, HGRN_HEADS, HGRN_VDIM), lb)
    y_b = rmsnorm(y_b, hgrn_norm_w.reshape(HGRN_HEADS, HGRN_VDIM)).reshape(bsz, t, HGRN_WIDTH) * jax.nn.silu(g)
    return jnp.concatenate([y_a, y_b], axis=-1) @ w_out


def fox_attention(hn, w_in, b_f, w_out):
    bsz, t, _ = hn.shape
    q, k, v, f_raw = jnp.split(hn @ w_in, [FOX_WIDTH, 2 * FOX_WIDTH, 3 * FOX_WIDTH], axis=-1)
    q = q.reshape(bsz, t, FOX_HEADS, FOX_HEAD_DIM)
    k = k.reshape(bsz, t, FOX_HEADS, FOX_HEAD_DIM)
    v = v.reshape(bsz, t, FOX_HEADS, FOX_HEAD_DIM)
    log_f = jax.nn.log_sigmoid(f_raw.astype(jnp.float32) + b_f.astype(jnp.float32))
    dcum = jnp.swapaxes(jnp.cumsum(log_f, axis=1), 1, 2)
    scale = FOX_HEAD_DIM ** -0.5
    q_idx = jnp.arange(Q_BLOCK)
    outs = []
    for blk in range(t // Q_BLOCK):
        s0 = blk * Q_BLOCK
        s1 = s0 + Q_BLOCK
        logits = jnp.einsum("bqhd,bkhd->bhqk", q[:, s0:s1], k[:, :s1]).astype(jnp.float32) * scale
        logits = logits + dcum[:, :, s0:s1, None] - dcum[:, :, None, :s1]
        mask = (s0 + q_idx)[:, None] >= jnp.arange(s1)[None, :]
        probs = jax.nn.softmax(jnp.where(mask, logits, -jnp.inf), axis=-1).astype(v.dtype)
        outs.append(jnp.einsum("bhqk,bkhd->bqhd", probs, v[:, :s1]))
    o = jnp.concatenate(outs, axis=1).reshape(bsz, t, FOX_WIDTH)
    return o @ w_out


def ple_add(h, p_i, gate_norm_w, w_gate, w_up, post_norm_w):
    emb = rmsnorm(p_i @ w_up, post_norm_w)
    gate = jax.nn.sigmoid(rmsnorm(h, gate_norm_w) @ w_gate)
    return h + emb * gate


def _fwd_setup_inputs(seed: int = 0) -> dict:
    key = jax.random.key(seed)
    ks = iter(jax.random.split(key, 40))
    f32 = jnp.float32
    D = D_MODEL

    def nrm(shape, scale):
        return jax.random.normal(next(ks), shape, f32) * scale

    def gain(shape):
        return 1.0 + 0.05 * jax.random.normal(next(ks), shape, f32)

    x = nrm((BATCH, SEQ, D), 1.0)
    p = nrm((DEPTH, BATCH, SEQ, PLE_DIM), 1.0)
    ffn1_norm = gain((DEPTH, D))
    ffn1_w_in = nrm((DEPTH, D, 2 * D_FF), D ** -0.5)
    ffn1_w_out = nrm((DEPTH, D_FF, D), D_FF ** -0.5)
    mix_norm = gain((DEPTH, D))
    ab_w_in = nrm((N_EVEN, D, AB_IN), D ** -0.5)
    ssd_conv_w = nrm((N_EVEN, SSD_CONV, SSD_CONV_DIM), SSD_CONV ** -0.5)
    ssd_conv_b = nrm((N_EVEN, SSD_CONV_DIM), 0.02)
    dt0 = jnp.exp(jax.random.uniform(next(ks), (N_EVEN, SSD_HEADS), f32, math.log(1e-3), math.log(1e-1)))
    ssd_dt_bias = dt0 + jnp.log(-jnp.expm1(-dt0))
    ssd_a_log = jnp.log(jax.random.uniform(next(ks), (N_EVEN, SSD_HEADS), f32, 1.0, 16.0))
    ssd_d = gain((N_EVEN, SSD_HEADS))
    ssd_norm = gain((N_EVEN, SSD_WIDTH))
    hgrn_lb_logits = nrm((DEPTH + 1, HGRN_WIDTH), 0.1)
    hgrn_norm = gain((N_EVEN, HGRN_WIDTH))
    ab_w_out = nrm((N_EVEN, AB_OUT, D), AB_OUT ** -0.5)
    fox_w_in = nrm((N_ODD, D, FOX_IN), D ** -0.5)
    fox_b_f = 2.0 + nrm((N_ODD, FOX_HEADS), 0.5)
    fox_w_out = nrm((N_ODD, FOX_WIDTH, D), FOX_WIDTH ** -0.5)
    ffn2_norm = gain((DEPTH, D))
    ffn2_w_in = nrm((DEPTH, D, 2 * D_FF), D ** -0.5)
    ffn2_w_out = nrm((DEPTH, D_FF, D), D_FF ** -0.5)
    ple_gate_norm = gain((DEPTH, D))
    ple_w_gate = nrm((DEPTH, D, D), D ** -0.5)
    ple_w_up = nrm((DEPTH, PLE_DIM, D), PLE_DIM ** -0.5)
    ple_norm = gain((DEPTH, D))
    final_norm = gain((D,))
    return {
        "x": x, "p": p,
        "ffn1_norm": ffn1_norm, "ffn1_w_in": ffn1_w_in, "ffn1_w_out": ffn1_w_out,
        "mix_norm": mix_norm, "ab_w_in": ab_w_in, "ssd_conv_w": ssd_conv_w, "ssd_conv_b": ssd_conv_b,
        "ssd_dt_bias": ssd_dt_bias, "ssd_a_log": ssd_a_log, "ssd_d": ssd_d, "ssd_norm": ssd_norm,
        "hgrn_lb_logits": hgrn_lb_logits, "hgrn_norm": hgrn_norm, "ab_w_out": ab_w_out,
        "fox_w_in": fox_w_in, "fox_b_f": fox_b_f, "fox_w_out": fox_w_out,
        "ffn2_norm": ffn2_norm, "ffn2_w_in": ffn2_w_in, "ffn2_w_out": ffn2_w_out,
        "ple_gate_norm": ple_gate_norm, "ple_w_gate": ple_w_gate, "ple_w_up": ple_w_up, "ple_norm": ple_norm,
        "final_norm": final_norm,
    }


def _fwd_reference(x, p, ffn1_norm, ffn1_w_in, ffn1_w_out, mix_norm, ab_w_in, ssd_conv_w, ssd_conv_b,
              ssd_dt_bias, ssd_a_log, ssd_d, ssd_norm, hgrn_lb_logits, hgrn_norm, ab_w_out,
              fox_w_in, fox_b_f, fox_w_out, ffn2_norm, ffn2_w_in, ffn2_w_out,
              ple_gate_norm, ple_w_gate, ple_w_up, ple_norm, final_norm):
    lb_all = jnp.cumsum(jax.nn.softmax(hgrn_lb_logits.astype(jnp.float32), axis=0), axis=0)
    h = x
    for i in range(DEPTH):
        h = swiglu_half(h, ffn1_norm[i], ffn1_w_in[i], ffn1_w_out[i])
        hn = rmsnorm(h, mix_norm[i])
        j = i // 2
        if i % 2 == 0:
            h = h + ssd_hgrn_mixer(hn, ab_w_in[j], ssd_conv_w[j], ssd_conv_b[j], ssd_dt_bias[j], ssd_a_log[j],
                                   ssd_d[j], ssd_norm[j], lb_all[i], hgrn_norm[j], ab_w_out[j])
        else:
            h = h + fox_attention(hn, fox_w_in[j], fox_b_f[j], fox_w_out[j])
        h = swiglu_half(h, ffn2_norm[i], ffn2_w_in[i], ffn2_w_out[i])
        h = ple_add(h, p[i], ple_gate_norm[i], ple_w_gate[i], ple_w_up[i], ple_norm[i])
    return rmsnorm(h, final_norm)


import jax as _jax
import jax.numpy as _jnp

TWIN_FORMAT = 'train_step'
FWD_PARAMS = ['x', 'p', 'ffn1_norm', 'ffn1_w_in', 'ffn1_w_out', 'mix_norm', 'ab_w_in', 'ssd_conv_w', 'ssd_conv_b', 'ssd_dt_bias', 'ssd_a_log', 'ssd_d', 'ssd_norm', 'hgrn_lb_logits', 'hgrn_norm', 'ab_w_out', 'fox_w_in', 'fox_b_f', 'fox_w_out', 'ffn2_norm', 'ffn2_w_in', 'ffn2_w_out', 'ple_gate_norm', 'ple_w_gate', 'ple_w_up', 'ple_norm', 'final_norm']
TWIN_WEIGHTS = ['ffn1_norm', 'ffn1_w_in', 'ffn1_w_out', 'mix_norm', 'ab_w_in', 'ssd_conv_w', 'ssd_conv_b', 'ssd_dt_bias', 'ssd_a_log', 'ssd_d', 'ssd_norm', 'hgrn_lb_logits', 'hgrn_norm', 'ab_w_out', 'fox_w_in', 'fox_b_f', 'fox_w_out', 'ffn2_norm', 'ffn2_w_in', 'ffn2_w_out', 'ple_gate_norm', 'ple_w_gate', 'ple_w_up', 'ple_norm', 'final_norm']
TWIN_DIFF_INPUT = 'x'
TWIN_INPUTS = ['x', 'p', 'ffn1_norm', 'ffn1_w_in', 'ffn1_w_out', 'mix_norm', 'ab_w_in', 'ssd_conv_w', 'ssd_conv_b', 'ssd_dt_bias', 'ssd_a_log', 'ssd_d', 'ssd_norm', 'hgrn_lb_logits', 'hgrn_norm', 'ab_w_out', 'fox_w_in', 'fox_b_f', 'fox_w_out', 'ffn2_norm', 'ffn2_w_in', 'ffn2_w_out', 'ple_gate_norm', 'ple_w_gate', 'ple_w_up', 'ple_norm', 'final_norm', 'loss_target', 'm_ffn1_norm', 'm_ffn1_w_in', 'm_ffn1_w_out', 'm_mix_norm', 'm_ab_w_in', 'm_ssd_conv_w', 'm_ssd_conv_b', 'm_ssd_dt_bias', 'm_ssd_a_log', 'm_ssd_d', 'm_ssd_norm', 'm_hgrn_lb_logits', 'm_hgrn_norm', 'm_ab_w_out', 'm_fox_w_in', 'm_fox_b_f', 'm_fox_w_out', 'm_ffn2_norm', 'm_ffn2_w_in', 'm_ffn2_w_out', 'm_ple_gate_norm', 'm_ple_w_gate', 'm_ple_w_up', 'm_ple_norm', 'm_final_norm', 'v_ffn1_norm', 'v_ffn1_w_in', 'v_ffn1_w_out', 'v_mix_norm', 'v_ab_w_in', 'v_ssd_conv_w', 'v_ssd_conv_b', 'v_ssd_dt_bias', 'v_ssd_a_log', 'v_ssd_d', 'v_ssd_norm', 'v_hgrn_lb_logits', 'v_hgrn_norm', 'v_ab_w_out', 'v_fox_w_in', 'v_fox_b_f', 'v_fox_w_out', 'v_ffn2_norm', 'v_ffn2_w_in', 'v_ffn2_w_out', 'v_ple_gate_norm', 'v_ple_w_gate', 'v_ple_w_up', 'v_ple_norm', 'v_final_norm']
TWIN_OUTPUTS = ['loss', 'grad_x', 'grad_ffn1_norm', 'grad_ffn1_w_in', 'grad_ffn1_w_out', 'grad_mix_norm', 'grad_ab_w_in', 'grad_ssd_conv_w', 'grad_ssd_conv_b', 'grad_ssd_dt_bias', 'grad_ssd_a_log', 'grad_ssd_d', 'grad_ssd_norm', 'grad_hgrn_lb_logits', 'grad_hgrn_norm', 'grad_ab_w_out', 'grad_fox_w_in', 'grad_fox_b_f', 'grad_fox_w_out', 'grad_ffn2_norm', 'grad_ffn2_w_in', 'grad_ffn2_w_out', 'grad_ple_gate_norm', 'grad_ple_w_gate', 'grad_ple_w_up', 'grad_ple_norm', 'grad_final_norm', 'delta_ffn1_norm', 'delta_ffn1_w_in', 'delta_ffn1_w_out', 'delta_mix_norm', 'delta_ab_w_in', 'delta_ssd_conv_w', 'delta_ssd_conv_b', 'delta_ssd_dt_bias', 'delta_ssd_a_log', 'delta_ssd_d', 'delta_ssd_norm', 'delta_hgrn_lb_logits', 'delta_hgrn_norm', 'delta_ab_w_out', 'delta_fox_w_in', 'delta_fox_b_f', 'delta_fox_w_out', 'delta_ffn2_norm', 'delta_ffn2_w_in', 'delta_ffn2_w_out', 'delta_ple_gate_norm', 'delta_ple_w_gate', 'delta_ple_w_up', 'delta_ple_norm', 'delta_final_norm', 'new_m_ffn1_norm', 'new_m_ffn1_w_in', 'new_m_ffn1_w_out', 'new_m_mix_norm', 'new_m_ab_w_in', 'new_m_ssd_conv_w', 'new_m_ssd_conv_b', 'new_m_ssd_dt_bias', 'new_m_ssd_a_log', 'new_m_ssd_d', 'new_m_ssd_norm', 'new_m_hgrn_lb_logits', 'new_m_hgrn_norm', 'new_m_ab_w_out', 'new_m_fox_w_in', 'new_m_fox_b_f', 'new_m_fox_w_out', 'new_m_ffn2_norm', 'new_m_ffn2_w_in', 'new_m_ffn2_w_out', 'new_m_ple_gate_norm', 'new_m_ple_w_gate', 'new_m_ple_w_up', 'new_m_ple_norm', 'new_m_final_norm', 'new_v_ffn1_norm', 'new_v_ffn1_w_in', 'new_v_ffn1_w_out', 'new_v_mix_norm', 'new_v_ab_w_in', 'new_v_ssd_conv_w', 'new_v_ssd_conv_b', 'new_v_ssd_dt_bias', 'new_v_ssd_a_log', 'new_v_ssd_d', 'new_v_ssd_norm', 'new_v_hgrn_lb_logits', 'new_v_hgrn_norm', 'new_v_ab_w_out', 'new_v_fox_w_in', 'new_v_fox_b_f', 'new_v_fox_w_out', 'new_v_ffn2_norm', 'new_v_ffn2_w_in', 'new_v_ffn2_w_out', 'new_v_ple_gate_norm', 'new_v_ple_w_gate', 'new_v_ple_w_up', 'new_v_ple_norm', 'new_v_final_norm']
TWIN_LEAF_KINDS = {'loss': 'loss', 'grad_x': 'grad_x', 'grad_ffn1_norm': 'grad_w', 'grad_ffn1_w_in': 'grad_w', 'grad_ffn1_w_out': 'grad_w', 'grad_mix_norm': 'grad_w', 'grad_ab_w_in': 'grad_w', 'grad_ssd_conv_w': 'grad_w', 'grad_ssd_conv_b': 'grad_w', 'grad_ssd_dt_bias': 'grad_w', 'grad_ssd_a_log': 'grad_w', 'grad_ssd_d': 'grad_w', 'grad_ssd_norm': 'grad_w', 'grad_hgrn_lb_logits': 'grad_w', 'grad_hgrn_norm': 'grad_w', 'grad_ab_w_out': 'grad_w', 'grad_fox_w_in': 'grad_w', 'grad_fox_b_f': 'grad_w', 'grad_fox_w_out': 'grad_w', 'grad_ffn2_norm': 'grad_w', 'grad_ffn2_w_in': 'grad_w', 'grad_ffn2_w_out': 'grad_w', 'grad_ple_gate_norm': 'grad_w', 'grad_ple_w_gate': 'grad_w', 'grad_ple_w_up': 'grad_w', 'grad_ple_norm': 'grad_w', 'grad_final_norm': 'grad_w', 'delta_ffn1_norm': 'delta_w', 'delta_ffn1_w_in': 'delta_w', 'delta_ffn1_w_out': 'delta_w', 'delta_mix_norm': 'delta_w', 'delta_ab_w_in': 'delta_w', 'delta_ssd_conv_w': 'delta_w', 'delta_ssd_conv_b': 'delta_w', 'delta_ssd_dt_bias': 'delta_w', 'delta_ssd_a_log': 'delta_w', 'delta_ssd_d': 'delta_w', 'delta_ssd_norm': 'delta_w', 'delta_hgrn_lb_logits': 'delta_w', 'delta_hgrn_norm': 'delta_w', 'delta_ab_w_out': 'delta_w', 'delta_fox_w_in': 'delta_w', 'delta_fox_b_f': 'delta_w', 'delta_fox_w_out': 'delta_w', 'delta_ffn2_norm': 'delta_w', 'delta_ffn2_w_in': 'delta_w', 'delta_ffn2_w_out': 'delta_w', 'delta_ple_gate_norm': 'delta_w', 'delta_ple_w_gate': 'delta_w', 'delta_ple_w_up': 'delta_w', 'delta_ple_norm': 'delta_w', 'delta_final_norm': 'delta_w', 'new_m_ffn1_norm': 'new_m', 'new_m_ffn1_w_in': 'new_m', 'new_m_ffn1_w_out': 'new_m', 'new_m_mix_norm': 'new_m', 'new_m_ab_w_in': 'new_m', 'new_m_ssd_conv_w': 'new_m', 'new_m_ssd_conv_b': 'new_m', 'new_m_ssd_dt_bias': 'new_m', 'new_m_ssd_a_log': 'new_m', 'new_m_ssd_d': 'new_m', 'new_m_ssd_norm': 'new_m', 'new_m_hgrn_lb_logits': 'new_m', 'new_m_hgrn_norm': 'new_m', 'new_m_ab_w_out': 'new_m', 'new_m_fox_w_in': 'new_m', 'new_m_fox_b_f': 'new_m', 'new_m_fox_w_out': 'new_m', 'new_m_ffn2_norm': 'new_m', 'new_m_ffn2_w_in': 'new_m', 'new_m_ffn2_w_out': 'new_m', 'new_m_ple_gate_norm': 'new_m', 'new_m_ple_w_gate': 'new_m', 'new_m_ple_w_up': 'new_m', 'new_m_ple_norm': 'new_m', 'new_m_final_norm': 'new_m', 'new_v_ffn1_norm': 'new_v', 'new_v_ffn1_w_in': 'new_v', 'new_v_ffn1_w_out': 'new_v', 'new_v_mix_norm': 'new_v', 'new_v_ab_w_in': 'new_v', 'new_v_ssd_conv_w': 'new_v', 'new_v_ssd_conv_b': 'new_v', 'new_v_ssd_dt_bias': 'new_v', 'new_v_ssd_a_log': 'new_v', 'new_v_ssd_d': 'new_v', 'new_v_ssd_norm': 'new_v', 'new_v_hgrn_lb_logits': 'new_v', 'new_v_hgrn_norm': 'new_v', 'new_v_ab_w_out': 'new_v', 'new_v_fox_w_in': 'new_v', 'new_v_fox_b_f': 'new_v', 'new_v_fox_w_out': 'new_v', 'new_v_ffn2_norm': 'new_v', 'new_v_ffn2_w_in': 'new_v', 'new_v_ffn2_w_out': 'new_v', 'new_v_ple_gate_norm': 'new_v', 'new_v_ple_w_gate': 'new_v', 'new_v_ple_w_up': 'new_v', 'new_v_ple_norm': 'new_v', 'new_v_final_norm': 'new_v'}


def _forward(args):
    return _fwd_reference(*[args[k] for k in FWD_PARAMS])


def _output_shape():
    def fwd():
        inp = _fwd_setup_inputs(0)
        return _fwd_reference(*[inp[k] for k in FWD_PARAMS])
    out = _jax.eval_shape(fwd)
    return out.shape, out.dtype

N_MICROBATCH = 1
ADAM_LR = 0.001
ADAM_B1 = 0.9
ADAM_B2 = 0.999
ADAM_EPS = 1e-08
ADAM_WD = 0.01
ADAM_STEP = 10
PER_EXAMPLE_BATCH_AXIS = {'x': 0, 'p': 1, 'loss_target': 0}
SHARED_INPUTS = []
_WEIGHT_DTYPES = {'ffn1_norm': _jnp.float32, 'ffn1_w_in': _jnp.float32, 'ffn1_w_out': _jnp.float32, 'mix_norm': _jnp.float32, 'ab_w_in': _jnp.float32, 'ssd_conv_w': _jnp.float32, 'ssd_conv_b': _jnp.float32, 'ssd_dt_bias': _jnp.float32, 'ssd_a_log': _jnp.float32, 'ssd_d': _jnp.float32, 'ssd_norm': _jnp.float32, 'hgrn_lb_logits': _jnp.float32, 'hgrn_norm': _jnp.float32, 'ab_w_out': _jnp.float32, 'fox_w_in': _jnp.float32, 'fox_b_f': _jnp.float32, 'fox_w_out': _jnp.float32, 'ffn2_norm': _jnp.float32, 'ffn2_w_in': _jnp.float32, 'ffn2_w_out': _jnp.float32, 'ple_gate_norm': _jnp.float32, 'ple_w_gate': _jnp.float32, 'ple_w_up': _jnp.float32, 'ple_norm': _jnp.float32, 'final_norm': _jnp.float32}
MOMENT_SCALE = {'ffn1_norm': 5.204778e-02, 'ffn1_w_in': 2.213367e-02, 'ffn1_w_out': 3.569759e-02, 'mix_norm': 8.748050e-02, 'ab_w_in': 4.544378e-02, 'ssd_conv_w': 5.958615e-02, 'ssd_conv_b': 9.104344e-02, 'ssd_dt_bias': 1.347610e-01, 'ssd_a_log': 2.280468e-01, 'ssd_d': 4.633792e-01, 'ssd_norm': 7.186475e-02, 'hgrn_lb_logits': 1.987556e-03, 'hgrn_norm': 3.959414e-02, 'ab_w_out': 8.064881e-02, 'fox_w_in': 2.794435e-02, 'fox_b_f': 1.827352e-01, 'fox_w_out': 3.357942e-02, 'ffn2_norm': 3.554163e-02, 'ffn2_w_in': 1.547310e-02, 'ffn2_w_out': 2.502327e-02, 'ple_gate_norm': 1.763045e-02, 'ple_w_gate': 1.854074e-02, 'ple_w_up': 4.595960e-02, 'ple_norm': 2.784285e-01, 'final_norm': 3.204634e+01}


def _to_microbatches(a, axis):
    t = _jnp.moveaxis(a, axis, 0)
    t = t.reshape((N_MICROBATCH, t.shape[0] // N_MICROBATCH) + t.shape[1:])
    return _jnp.moveaxis(t, 1, axis + 1)


def setup_inputs(seed: int = 0) -> dict:
    inp = _fwd_setup_inputs(seed)
    key = _jax.random.fold_in(_jax.random.key(seed), 7919)
    shape, _ = _output_shape()
    out = dict(inp)
    out["loss_target"] = _jax.random.normal(_jax.random.fold_in(key, 0), shape, _jnp.float32)
    for i, name in enumerate(TWIN_WEIGHTS):
        w = inp[name].astype(_jnp.float32)
        if MOMENT_SCALE is None:
            s = _jnp.sqrt(_jnp.mean(_jnp.square(w)) + 1e-30)
        else:
            s = MOMENT_SCALE[name]
        km, kv = _jax.random.split(_jax.random.fold_in(key, i + 1))
        out[name] = w
        out["m_" + name] = s * _jax.random.normal(km, w.shape, _jnp.float32)
        out["v_" + name] = (s * s) * _jax.random.uniform(kv, w.shape, _jnp.float32, 0.5, 1.5)
    if N_MICROBATCH > 1:
        for name, axis in PER_EXAMPLE_BATCH_AXIS.items():
            out[name] = _to_microbatches(out[name], axis)
    return {'x': out['x'], 'p': out['p'], 'ffn1_norm': out['ffn1_norm'], 'ffn1_w_in': out['ffn1_w_in'], 'ffn1_w_out': out['ffn1_w_out'], 'mix_norm': out['mix_norm'], 'ab_w_in': out['ab_w_in'], 'ssd_conv_w': out['ssd_conv_w'], 'ssd_conv_b': out['ssd_conv_b'], 'ssd_dt_bias': out['ssd_dt_bias'], 'ssd_a_log': out['ssd_a_log'], 'ssd_d': out['ssd_d'], 'ssd_norm': out['ssd_norm'], 'hgrn_lb_logits': out['hgrn_lb_logits'], 'hgrn_norm': out['hgrn_norm'], 'ab_w_out': out['ab_w_out'], 'fox_w_in': out['fox_w_in'], 'fox_b_f': out['fox_b_f'], 'fox_w_out': out['fox_w_out'], 'ffn2_norm': out['ffn2_norm'], 'ffn2_w_in': out['ffn2_w_in'], 'ffn2_w_out': out['ffn2_w_out'], 'ple_gate_norm': out['ple_gate_norm'], 'ple_w_gate': out['ple_w_gate'], 'ple_w_up': out['ple_w_up'], 'ple_norm': out['ple_norm'], 'final_norm': out['final_norm'], 'loss_target': out['loss_target'], 'm_ffn1_norm': out['m_ffn1_norm'], 'm_ffn1_w_in': out['m_ffn1_w_in'], 'm_ffn1_w_out': out['m_ffn1_w_out'], 'm_mix_norm': out['m_mix_norm'], 'm_ab_w_in': out['m_ab_w_in'], 'm_ssd_conv_w': out['m_ssd_conv_w'], 'm_ssd_conv_b': out['m_ssd_conv_b'], 'm_ssd_dt_bias': out['m_ssd_dt_bias'], 'm_ssd_a_log': out['m_ssd_a_log'], 'm_ssd_d': out['m_ssd_d'], 'm_ssd_norm': out['m_ssd_norm'], 'm_hgrn_lb_logits': out['m_hgrn_lb_logits'], 'm_hgrn_norm': out['m_hgrn_norm'], 'm_ab_w_out': out['m_ab_w_out'], 'm_fox_w_in': out['m_fox_w_in'], 'm_fox_b_f': out['m_fox_b_f'], 'm_fox_w_out': out['m_fox_w_out'], 'm_ffn2_norm': out['m_ffn2_norm'], 'm_ffn2_w_in': out['m_ffn2_w_in'], 'm_ffn2_w_out': out['m_ffn2_w_out'], 'm_ple_gate_norm': out['m_ple_gate_norm'], 'm_ple_w_gate': out['m_ple_w_gate'], 'm_ple_w_up': out['m_ple_w_up'], 'm_ple_norm': out['m_ple_norm'], 'm_final_norm': out['m_final_norm'], 'v_ffn1_norm': out['v_ffn1_norm'], 'v_ffn1_w_in': out['v_ffn1_w_in'], 'v_ffn1_w_out': out['v_ffn1_w_out'], 'v_mix_norm': out['v_mix_norm'], 'v_ab_w_in': out['v_ab_w_in'], 'v_ssd_conv_w': out['v_ssd_conv_w'], 'v_ssd_conv_b': out['v_ssd_conv_b'], 'v_ssd_dt_bias': out['v_ssd_dt_bias'], 'v_ssd_a_log': out['v_ssd_a_log'], 'v_ssd_d': out['v_ssd_d'], 'v_ssd_norm': out['v_ssd_norm'], 'v_hgrn_lb_logits': out['v_hgrn_lb_logits'], 'v_hgrn_norm': out['v_hgrn_norm'], 'v_ab_w_out': out['v_ab_w_out'], 'v_fox_w_in': out['v_fox_w_in'], 'v_fox_b_f': out['v_fox_b_f'], 'v_fox_w_out': out['v_fox_w_out'], 'v_ffn2_norm': out['v_ffn2_norm'], 'v_ffn2_w_in': out['v_ffn2_w_in'], 'v_ffn2_w_out': out['v_ffn2_w_out'], 'v_ple_gate_norm': out['v_ple_gate_norm'], 'v_ple_w_gate': out['v_ple_w_gate'], 'v_ple_w_up': out['v_ple_w_up'], 'v_ple_norm': out['v_ple_norm'], 'v_final_norm': out['v_final_norm']}


def _loss(weights, diff, rest, loss_target):
    with _jax.named_scope("forward"):
        args = {**rest, TWIN_DIFF_INPUT: diff, **{k: w.astype(_WEIGHT_DTYPES[k]) for k, w in weights.items()}}
        y = _forward(args)
    with _jax.named_scope("loss_head"):
        err = _jnp.square(y.astype(_jnp.float32) - loss_target)
        return 0.5 * _jnp.sum(_jnp.mean(err, axis=-1)) if err.ndim else 0.5 * err


def _adamw(w, g, m, v):
    m = ADAM_B1 * m + (1.0 - ADAM_B1) * g
    v = ADAM_B2 * v + (1.0 - ADAM_B2) * _jnp.square(g)
    m_hat = m / (1.0 - ADAM_B1 ** ADAM_STEP)
    v_hat = v / (1.0 - ADAM_B2 ** ADAM_STEP)
    delta = -ADAM_LR * (m_hat / (_jnp.sqrt(v_hat) + ADAM_EPS) + ADAM_WD * w)
    return delta, m, v


def reference(x, p, ffn1_norm, ffn1_w_in, ffn1_w_out, mix_norm, ab_w_in, ssd_conv_w, ssd_conv_b, ssd_dt_bias, ssd_a_log, ssd_d, ssd_norm, hgrn_lb_logits, hgrn_norm, ab_w_out, fox_w_in, fox_b_f, fox_w_out, ffn2_norm, ffn2_w_in, ffn2_w_out, ple_gate_norm, ple_w_gate, ple_w_up, ple_norm, final_norm, loss_target, m_ffn1_norm, m_ffn1_w_in, m_ffn1_w_out, m_mix_norm, m_ab_w_in, m_ssd_conv_w, m_ssd_conv_b, m_ssd_dt_bias, m_ssd_a_log, m_ssd_d, m_ssd_norm, m_hgrn_lb_logits, m_hgrn_norm, m_ab_w_out, m_fox_w_in, m_fox_b_f, m_fox_w_out, m_ffn2_norm, m_ffn2_w_in, m_ffn2_w_out, m_ple_gate_norm, m_ple_w_gate, m_ple_w_up, m_ple_norm, m_final_norm, v_ffn1_norm, v_ffn1_w_in, v_ffn1_w_out, v_mix_norm, v_ab_w_in, v_ssd_conv_w, v_ssd_conv_b, v_ssd_dt_bias, v_ssd_a_log, v_ssd_d, v_ssd_norm, v_hgrn_lb_logits, v_hgrn_norm, v_ab_w_out, v_fox_w_in, v_fox_b_f, v_fox_w_out, v_ffn2_norm, v_ffn2_w_in, v_ffn2_w_out, v_ple_gate_norm, v_ple_w_gate, v_ple_w_up, v_ple_norm, v_final_norm):
    given = dict(x=x, p=p, ffn1_norm=ffn1_norm, ffn1_w_in=ffn1_w_in, ffn1_w_out=ffn1_w_out, mix_norm=mix_norm, ab_w_in=ab_w_in, ssd_conv_w=ssd_conv_w, ssd_conv_b=ssd_conv_b, ssd_dt_bias=ssd_dt_bias, ssd_a_log=ssd_a_log, ssd_d=ssd_d, ssd_norm=ssd_norm, hgrn_lb_logits=hgrn_lb_logits, hgrn_norm=hgrn_norm, ab_w_out=ab_w_out, fox_w_in=fox_w_in, fox_b_f=fox_b_f, fox_w_out=fox_w_out, ffn2_norm=ffn2_norm, ffn2_w_in=ffn2_w_in, ffn2_w_out=ffn2_w_out, ple_gate_norm=ple_gate_norm, ple_w_gate=ple_w_gate, ple_w_up=ple_w_up, ple_norm=ple_norm, final_norm=final_norm, loss_target=loss_target, m_ffn1_norm=m_ffn1_norm, m_ffn1_w_in=m_ffn1_w_in, m_ffn1_w_out=m_ffn1_w_out, m_mix_norm=m_mix_norm, m_ab_w_in=m_ab_w_in, m_ssd_conv_w=m_ssd_conv_w, m_ssd_conv_b=m_ssd_conv_b, m_ssd_dt_bias=m_ssd_dt_bias, m_ssd_a_log=m_ssd_a_log, m_ssd_d=m_ssd_d, m_ssd_norm=m_ssd_norm, m_hgrn_lb_logits=m_hgrn_lb_logits, m_hgrn_norm=m_hgrn_norm, m_ab_w_out=m_ab_w_out, m_fox_w_in=m_fox_w_in, m_fox_b_f=m_fox_b_f, m_fox_w_out=m_fox_w_out, m_ffn2_norm=m_ffn2_norm, m_ffn2_w_in=m_ffn2_w_in, m_ffn2_w_out=m_ffn2_w_out, m_ple_gate_norm=m_ple_gate_norm, m_ple_w_gate=m_ple_w_gate, m_ple_w_up=m_ple_w_up, m_ple_norm=m_ple_norm, m_final_norm=m_final_norm, v_ffn1_norm=v_ffn1_norm, v_ffn1_w_in=v_ffn1_w_in, v_ffn1_w_out=v_ffn1_w_out, v_mix_norm=v_mix_norm, v_ab_w_in=v_ab_w_in, v_ssd_conv_w=v_ssd_conv_w, v_ssd_conv_b=v_ssd_conv_b, v_ssd_dt_bias=v_ssd_dt_bias, v_ssd_a_log=v_ssd_a_log, v_ssd_d=v_ssd_d, v_ssd_norm=v_ssd_norm, v_hgrn_lb_logits=v_hgrn_lb_logits, v_hgrn_norm=v_hgrn_norm, v_ab_w_out=v_ab_w_out, v_fox_w_in=v_fox_w_in, v_fox_b_f=v_fox_b_f, v_fox_w_out=v_fox_w_out, v_ffn2_norm=v_ffn2_norm, v_ffn2_w_in=v_ffn2_w_in, v_ffn2_w_out=v_ffn2_w_out, v_ple_gate_norm=v_ple_gate_norm, v_ple_w_gate=v_ple_w_gate, v_ple_w_up=v_ple_w_up, v_ple_norm=v_ple_norm, v_final_norm=v_final_norm)
    weights = {n: given[n] for n in TWIN_WEIGHTS}
    shared = {n: given[n] for n in SHARED_INPUTS}
    per_example = {n: given[n] for n in ['x', 'p']}
    grad_fn = _jax.value_and_grad(_loss, argnums=(0, 1))

    def one_microbatch(ex, loss_target):
        ex = dict(ex)
        diff = ex.pop(TWIN_DIFF_INPUT)
        return grad_fn(weights, diff, {**shared, **ex}, loss_target)

    if N_MICROBATCH == 1:
        loss, (grad_w, grad_x) = one_microbatch(per_example, given["loss_target"])
    else:
        def body(carry, xs):
            loss_sum, grad_sum = carry
            l_k, (gw_k, gx_k) = one_microbatch(xs[0], xs[1])
            with _jax.named_scope("update"):
                return (loss_sum + l_k, _jax.tree.map(_jnp.add, grad_sum, gw_k)), gx_k

        init = (_jnp.zeros((), _jnp.float32), _jax.tree.map(_jnp.zeros_like, weights))
        (loss, grad_w), grad_x = _jax.lax.scan(body, init, (per_example, given["loss_target"]))
    with _jax.named_scope("update"):
        delta_w, new_m, new_v = {}, {}, {}
        for n in TWIN_WEIGHTS:
            delta_w[n], new_m[n], new_v[n] = _adamw(weights[n], grad_w[n], given["m_" + n], given["v_" + n])
    return (loss, grad_x, *[grad_w[n] for n in TWIN_WEIGHTS], *[delta_w[n] for n in TWIN_WEIGHTS],
            *[new_m[n] for n in TWIN_WEIGHTS], *[new_v[n] for n in TWIN_WEIGHTS])
```

```python
import functools

import jax
import jax.numpy as jnp
from jax import lax
from jax.experimental import pallas as pl
from jax.experimental.pallas import tpu as pltpu

F32 = jnp.float32
BF16 = jnp.bfloat16
HI = lax.Precision.HIGHEST

EPS = 1e-6
CHUNK = 64
SSD_HEAD_DIM = 64
SSD_GROUPS = 4
SSD_STATE = 128
SSD_CONV = 4
HGRN_KDIM = 128
FOX_HEAD_DIM = 128
LANE = 128
FF_ALIGN = 512
FLAT_W = 2048
ROW_ALIGN = 32

ADAM_LR = 0.001
ADAM_B1 = 0.9
ADAM_B2 = 0.999
ADAM_EPS = 1e-08
ADAM_WD = 0.01
ADAM_STEP = 10

VMEM_LIMIT = 56 * 1024 * 1024
MESH_ID = pl.DeviceIdType.MESH


def _cparams(sem):
    return pltpu.CompilerParams(dimension_semantics=sem, vmem_limit_bytes=VMEM_LIMIT)


def _pick(n, prefs):
    for t in prefs:
        if n % t == 0:
            return t
    return n


def _dot(a, b, dims, precision=None):
    return lax.dot_general(a, b, (dims, ((), ())), preferred_element_type=F32, precision=precision)


NN = ((1,), (0,))
NT = ((1,), (1,))
TN = ((0,), (0,))


def _sigmoid(x):
    return 1.0 / (1.0 + jnp.exp(-x))


def _silu(x):
    return x * _sigmoid(x)


def _dsilu(x):
    s = _sigmoid(x)
    return s * (1.0 + x * (1.0 - s))


def matmul(a, b, mode, out_dtype, name, scale=None, residual=None):
    if mode == "nn":
        (m, k), (k2, n) = a.shape, b.shape
    elif mode == "nt":
        (m, k), (n, k2) = a.shape, b.shape
    else:
        (k, m), (k2, n) = a.shape, b.shape
    assert k == k2, (a.shape, b.shape, mode)
    tm = _pick(m, (1024, 512, 256, 128))
    tn = _pick(n, (1024, 1408, 512, 256, 128))
    tk = _pick(k, (512, 256, 128))
    nk = k // tk
    dims = {"nn": NN, "nt": NT, "tn": TN}[mode]

    def body(*refs):
        if residual is None:
            a_ref, b_ref, o_ref, acc_ref = refs
            r_ref = None
        else:
            a_ref, b_ref, r_ref, o_ref, acc_ref = refs
        kk = pl.program_id(2)

        @pl.when(kk == 0)
        def _():
            acc_ref[...] = jnp.zeros_like(acc_ref)

        acc_ref[...] += _dot(a_ref[...].astype(BF16), b_ref[...].astype(BF16), dims)

        @pl.when(kk == nk - 1)
        def _():
            r = acc_ref[...]
            if scale is not None:
                r = r * scale
            if r_ref is not None:
                r = r + r_ref[...].astype(F32)
            o_ref[...] = r.astype(o_ref.dtype)

    if mode == "nn":
        a_spec = pl.BlockSpec((tm, tk), lambda i, j, kk: (i, kk))
        b_spec = pl.BlockSpec((tk, tn), lambda i, j, kk: (kk, j))
    elif mode == "nt":
        a_spec = pl.BlockSpec((tm, tk), lambda i, j, kk: (i, kk))
        b_spec = pl.BlockSpec((tn, tk), lambda i, j, kk: (j, kk))
    else:
        a_spec = pl.BlockSpec((tk, tm), lambda i, j, kk: (kk, i))
        b_spec = pl.BlockSpec((tk, tn), lambda i, j, kk: (kk, j))
    o_spec = pl.BlockSpec((tm, tn), lambda i, j, kk: (i, j))
    in_specs = [a_spec, b_spec]
    args = [a, b]
    if residual is not None:
        in_specs.append(o_spec)
        args.append(residual)
    return pl.pallas_call(
        body, name=name, grid=(m // tm, n // tn, nk),
        in_specs=in_specs, out_specs=o_spec,
        out_shape=jax.ShapeDtypeStruct((m, n), out_dtype),
        scratch_shapes=[pltpu.VMEM((tm, tn), F32)],
        compiler_params=_cparams(("parallel", "parallel", "arbitrary")),
    )(*args)


def rowwise(fn, name, rows, consts, outs, accs=(), tm=256):
    rows = [r if isinstance(r, tuple) else (r, r.shape[1], 0) for r in rows]
    t = rows[0][0].shape[0]
    tm = min(tm, t)
    assert t % tm == 0
    n_in = len(rows) + len(consts)
    n_out = len(outs)

    def body(*refs):
        res = fn(*[r[...] for r in refs[:n_in]])
        if not isinstance(res, tuple):
            res = (res,)
        for r, v in zip(refs[n_in:n_in + n_out], res[:n_out]):
            r[...] = v.astype(r.dtype)
        if accs:
            a_refs = refs[n_in + n_out:]
            first = pl.program_id(0) == 0

            @pl.when(first)
            def _():
                for r, v in zip(a_refs, res[n_out:]):
                    r[...] = v

            @pl.when(jnp.logical_not(first))
            def _():
                for r, v in zip(a_refs, res[n_out:]):
                    r[...] += v

    in_specs = [pl.BlockSpec((tm, w), functools.partial(lambda i, cb: (i, cb), cb=cb)) for _, w, cb in rows]
    in_specs += [pl.BlockSpec(c.shape, lambda i: (0, 0)) for c in consts]
    out_specs = [pl.BlockSpec((tm, w), lambda i: (i, 0)) for w, _ in outs]
    out_specs += [pl.BlockSpec(s, lambda i: (0, 0)) for s in accs]
    out_shape = [jax.ShapeDtypeStruct((t, w), d) for w, d in outs]
    out_shape += [jax.ShapeDtypeStruct(s, F32) for s in accs]
    res = pl.pallas_call(
        body, name=name, grid=(t // tm,), in_specs=in_specs, out_specs=out_specs, out_shape=out_shape,
        compiler_params=_cparams(("arbitrary",) if accs else ("parallel",)),
    )(*[r[0] for r in rows], *consts)
    return res


def _rms(x):
    return lax.rsqrt(jnp.mean(x * x, axis=-1, keepdims=True) + EPS)


def _norm_bwd(x, w, dy):
    r = _rms(x)
    xh = x * r
    g = dy * w
    dx = r * (g - xh * jnp.mean(g * xh, axis=-1, keepdims=True))
    return dx, jnp.sum(dy * xh, axis=0, keepdims=True)


def norm_fwd(h, w, name):
    return rowwise(lambda x, w_: x * _rms(x) * w_, name, [h], [w], [(h.shape[1], BF16)])[0]


def norm_bwd_res(h, w, dhn, dres, name):
    def fn(x, dy, dr, w_):
        dx, dw = _norm_bwd(x, w_, dy.astype(F32))
        return dr + dx, dw
    d = h.shape[1]
    return rowwise(fn, name, [h, dhn, dres], [w], [(d, F32)], [(1, d)])


def ffn_fwd(h, norm_w, w_in, w_out, tag):
    fp = w_out.shape[0]
    hn = norm_fwd(h, norm_w, f"{tag}_norm")
    u = matmul(hn, w_in, "nn", BF16, f"{tag}_in")
    a = rowwise(lambda g, up: _silu(g.astype(F32)) * up.astype(F32), f"{tag}_act",
                [(u, fp, 0), (u, fp, 1)], [], [(fp, BF16)], tm=128)[0]
    out = matmul(a, w_out, "nn", F32, f"{tag}_out", scale=0.5, residual=h)
    return out, (h, hn, u, a)


def ffn_bwd(dout, saved, norm_w, w_in, w_out, tag):
    h, hn, u, a = saved
    fp = w_out.shape[0]
    da = matmul(dout, w_out, "nt", BF16, f"{tag}_da", scale=0.5)

    def act_bwd(da_, g, up):
        da_, g, up = da_.astype(F32), g.astype(F32), up.astype(F32)
        return jnp.concatenate([da_ * up * _dsilu(g), da_ * _silu(g)], axis=1)

    du = rowwise(act_bwd, f"{tag}_dact", [da, (u, fp, 0), (u, fp, 1)], [], [(2 * fp, BF16)], tm=128)[0]
    dw_out = matmul(a, dout, "tn", F32, f"{tag}_dwout", scale=0.5)
    dhn = matmul(du, w_in, "nt", F32, f"{tag}_dhn")
    dw_in = matmul(hn, du, "tn", F32, f"{tag}_dwin")
    dh, dnw = norm_bwd_res(h, norm_w, dhn, dout, f"{tag}_dnorm")
    return dh, dnw, dw_in, dw_out


def ple_fwd(h, p_i, gate_norm_w, w_gate, w_up, post_norm_w, tag):
    d = h.shape[1]
    e0 = matmul(p_i, w_up, "nn", F32, f"{tag}_up")
    hn = norm_fwd(h, gate_norm_w, f"{tag}_norm")
    s = matmul(hn, w_gate, "nn", F32, f"{tag}_gate")
    out = rowwise(lambda x, e, s_, pw: x + e * _rms(e) * pw * _sigmoid(s_), f"{tag}_add",
                  [h, e0, s], [post_norm_w], [(d, F32)])[0]
    return out, (h, hn, e0, s)


def ple_bwd(dout, saved, p_i, gate_norm_w, w_gate, post_norm_w, tag):
    h, hn, e0, s = saved
    d = h.shape[1]

    def fn(dy, e, s_, pw):
        gate = _sigmoid(s_)
        emb = e * _rms(e) * pw
        de0, dpw = _norm_bwd(e, pw, dy * gate)
        return de0, dy * emb * gate * (1.0 - gate), dpw

    de0, ds, dpw = rowwise(fn, f"{tag}_dadd", [dout, e0, s], [post_norm_w], [(d, BF16), (d, BF16)], [(1, d)])
    dw_up = matmul(p_i, de0, "tn", F32, f"{tag}_dwup")
    dhn = matmul(ds, w_gate, "nt", F32, f"{tag}_dhn")
    dw_gate = matmul(hn, ds, "tn", F32, f"{tag}_dwgate")
    dh, dgn = norm_bwd_res(h, gate_norm_w, dhn, dout, f"{tag}_dnorm")
    return dh, dgn, dw_gate, dw_up, dpw


def loss_head(h, target, w):
    d = h.shape[1]

    def fn(x, tgt, w_):
        r = _rms(x)
        err = x * r * w_ - tgt
        dx, dw = _norm_bwd(x, w_, err * (1.0 / d))
        part = 0.5 * jnp.sum(jnp.sum(err * err, axis=-1, keepdims=True), axis=0, keepdims=True) * (1.0 / d)
        return dx, dw, jnp.broadcast_to(part, (1, LANE))

    dh, dw, loss = rowwise(fn, "loss_head", [h, target], [w], [(d, F32)], [(1, d), (1, LANE)])
    return loss[0, 0], dh, dw


def cumsum_rows(x, name, reverse=False):
    t, w = x.shape
    tb = min(256, t)
    nb = t // tb

    def body(x_ref, o_ref, carry):
        @pl.when(pl.program_id(0) == 0)
        def _():
            carry[...] = jnp.zeros_like(carry)

        r = lax.broadcasted_iota(jnp.int32, (tb, tb), 0)
        c = lax.broadcasted_iota(jnp.int32, (tb, tb), 1)
        tri = ((c >= r) if reverse else (c <= r)).astype(F32)
        y = _dot(tri, x_ref[...], NN, HI) + carry[...]
        o_ref[...] = y
        carry[...] = y[0:1, :] if reverse else y[tb - 1:tb, :]

    idx = (lambda i: (nb - 1 - i, 0)) if reverse else (lambda i: (i, 0))
    return pl.pallas_call(
        body, name=name, grid=(nb,), in_specs=[pl.BlockSpec((tb, w), idx)], out_specs=pl.BlockSpec((tb, w), idx),
        out_shape=jax.ShapeDtypeStruct((t, w), F32), scratch_shapes=[pltpu.VMEM((1, w), F32)],
        compiler_params=_cparams(("arbitrary",)),
    )(x)


def _fox_tiles(t):
    tq = _pick(t, (512, 256, 128))
    return tq, t // tq


def _causal(s, row0, col0, transposed=False):
    r = row0 + lax.broadcasted_iota(jnp.int32, s.shape, 0)
    c = col0 + lax.broadcasted_iota(jnp.int32, s.shape, 1)
    return jnp.where((c >= r) if transposed else (r >= c), s, -jnp.inf)


def fox_flash_fwd(qkv, dcol, drow, name):
    t = qkv.shape[0]
    nh = dcol.shape[0]
    tq, nq = _fox_tiles(t)
    scale = FOX_HEAD_DIM ** -0.5

    def body(q_ref, k_ref, v_ref, dc_ref, dr_ref, o_ref, lse_ref, m_s, l_s, acc_s):
        qi, ki = pl.program_id(1), pl.program_id(2)

        @pl.when(ki == 0)
        def _():
            m_s[...] = jnp.full_like(m_s, -jnp.inf)
            l_s[...] = jnp.zeros_like(l_s)
            acc_s[...] = jnp.zeros_like(acc_s)

        @pl.when(ki <= qi)
        def _():
            s = _dot(q_ref[...], k_ref[...], NT) * scale + dc_ref[0] - dr_ref[0]
            s = _causal(s, qi * tq, ki * tq)
            m_new = jnp.maximum(m_s[...], jnp.max(s, axis=1, keepdims=True))
            alpha = jnp.exp(m_s[...] - m_new)
            p = jnp.exp(s - m_new)
            l_s[...] = alpha * l_s[...] + jnp.sum(p, axis=1, keepdims=True)
            acc_s[...] = alpha * acc_s[...] + _dot(p.astype(BF16), v_ref[...], NN)
            m_s[...] = m_new

        @pl.when(ki == nq - 1)
        def _():
            o_ref[...] = (acc_s[...] / l_s[...]).astype(o_ref.dtype)
            lse_ref[0] = m_s[...] + jnp.log(l_s[...])

    hd = FOX_HEAD_DIM
    return pl.pallas_call(
        body, name=name, grid=(nh, nq, nq),
        in_specs=[
            pl.BlockSpec((tq, hd), lambda h, qi, ki: (qi, h)),
            pl.BlockSpec((tq, hd), lambda h, qi, ki: (jnp.minimum(ki, qi), nh + h)),
            pl.BlockSpec((tq, hd), lambda h, qi, ki: (jnp.minimum(ki, qi), 2 * nh + h)),
            pl.BlockSpec((1, tq, 1), lambda h, qi, ki: (h, qi, 0)),
            pl.BlockSpec((1, 1, tq), lambda h, qi, ki: (h, 0, jnp.minimum(ki, qi))),
        ],
        out_specs=[
            pl.BlockSpec((tq, hd), lambda h, qi, ki: (qi, h)),
            pl.BlockSpec((1, tq, 1), lambda h, qi, ki: (h, qi, 0)),
        ],
        out_shape=[jax.ShapeDtypeStruct((t, nh * hd), BF16), jax.ShapeDtypeStruct((nh, t, 1), F32)],
        scratch_shapes=[pltpu.VMEM((tq, 1), F32), pltpu.VMEM((tq, 1), F32), pltpu.VMEM((tq, hd), F32)],
        compiler_params=_cparams(("parallel", "parallel", "arbitrary")),
    )(qkv, qkv, qkv, dcol, drow)


def fox_flash_dq(qkv, do, dcol, drow, lse_col, delta_col, name):
    t = qkv.shape[0]
    nh = dcol.shape[0]
    tq, nq = _fox_tiles(t)
    scale = FOX_HEAD_DIM ** -0.5

    def body(q_ref, k_ref, v_ref, do_ref, dc_ref, dr_ref, lse_ref, dl_ref, dq_ref, db_ref, acc_s, db_s):
        qi, ki = pl.program_id(1), pl.program_id(2)

        @pl.when(ki == 0)
        def _():
            acc_s[...] = jnp.zeros_like(acc_s)
            db_s[...] = jnp.zeros_like(db_s)

        @pl.when(ki <= qi)
        def _():
            s = _dot(q_ref[...], k_ref[...], NT) * scale + dc_ref[0] - dr_ref[0]
            p = jnp.exp(_causal(s, qi * tq, ki * tq) - lse_ref[0])
            dp = _dot(do_ref[...], v_ref[...], NT)
            ds = p * (dp - dl_ref[0])
            acc_s[...] += _dot(ds.astype(BF16), k_ref[...], NN)
            db_s[...] += jnp.sum(ds, axis=1, keepdims=True)

        @pl.when(ki == nq - 1)
        def _():
            dq_ref[...] = (acc_s[...] * scale).astype(dq_ref.dtype)
            db_ref[0] = db_s[...]

    hd = FOX_HEAD_DIM
    col = pl.BlockSpec((1, tq, 1), lambda h, qi, ki: (h, qi, 0))
    return pl.pallas_call(
        body, name=name, grid=(nh, nq, nq),
        in_specs=[
            pl.BlockSpec((tq, hd), lambda h, qi, ki: (qi, h)),
            pl.BlockSpec((tq, hd), lambda h, qi, ki: (jnp.minimum(ki, qi), nh + h)),
            pl.BlockSpec((tq, hd), lambda h, qi, ki: (jnp.minimum(ki, qi), 2 * nh + h)),
            pl.BlockSpec((tq, hd), lambda h, qi, ki: (qi, h)),
            col,
            pl.BlockSpec((1, 1, tq), lambda h, qi, ki: (h, 0, jnp.minimum(ki, qi))),
            col, col,
        ],
        out_specs=[pl.BlockSpec((tq, hd), lambda h, qi, ki: (qi, h)), col],
        out_shape=[jax.ShapeDtypeStruct((t, nh * hd), BF16), jax.ShapeDtypeStruct((nh, t, 1), F32)],
        scratch_shapes=[pltpu.VMEM((tq, hd), F32), pltpu.VMEM((tq, 1), F32)],
        compiler_params=_cparams(("parallel", "parallel", "arbitrary")),
    )(qkv, qkv, qkv, do, dcol, drow, lse_col, delta_col)


def fox_flash_dkv(qkv, do, dcol, drow, lse_row, delta_row, name):
    t = qkv.shape[0]
    nh = dcol.shape[0]
    tq, nq = _fox_tiles(t)
    scale = FOX_HEAD_DIM ** -0.5

    def body(q_ref, k_ref, v_ref, do_ref, dc_ref, dr_ref, lse_ref, dl_ref, dk_ref, dv_ref, db_ref, dk_s, dv_s, db_s):
        ki, qi = pl.program_id(1), pl.program_id(2)

        @pl.when(qi == 0)
        def _():
            dk_s[...] = jnp.zeros_like(dk_s)
            dv_s[...] = jnp.zeros_like(dv_s)
            db_s[...] = jnp.zeros_like(db_s)

        @pl.when(qi >= ki)
        def _():
            st = _dot(k_ref[...], q_ref[...], NT) * scale + dr_ref[0] - dc_ref[0]
            pt = jnp.exp(_causal(st, ki * tq, qi * tq, transposed=True) - lse_ref[0])
            dv_s[...] += _dot(pt.astype(BF16), do_ref[...], NN)
            dpt = _dot(v_ref[...], do_ref[...], NT)
            dst = pt * (dpt - dl_ref[0])
            dk_s[...] += _dot(dst.astype(BF16), q_ref[...], NN)
            db_s[...] -= jnp.sum(dst, axis=1, keepdims=True)

        @pl.when(qi == nq - 1)
        def _():
            dk_ref[...] = (dk_s[...] * scale).astype(dk_ref.dtype)
            dv_ref[...] = dv_s[...].astype(dv_ref.dtype)
            db_ref[0] = db_s[...]

    hd = FOX_HEAD_DIM
    row = pl.BlockSpec((1, 1, tq), lambda h, ki, qi: (h, 0, jnp.maximum(qi, ki)))
    kv_out = pl.BlockSpec((tq, hd), lambda h, ki, qi: (ki, h))
    return pl.pallas_call(
        body, name=name, grid=(nh, nq, nq),
        in_specs=[
            pl.BlockSpec((tq, hd), lambda h, ki, qi: (jnp.maximum(qi, ki), h)),
            pl.BlockSpec((tq, hd), lambda h, ki, qi: (ki, nh + h)),
            pl.BlockSpec((tq, hd), lambda h, ki, qi: (ki, 2 * nh + h)),
            pl.BlockSpec((tq, hd), lambda h, ki, qi: (jnp.maximum(qi, ki), h)),
            pl.BlockSpec((1, tq, 1), lambda h, ki, qi: (h, ki, 0)),
            row, row, row,
        ],
        out_specs=[kv_out, kv_out, pl.BlockSpec((1, tq, 1), lambda h, ki, qi: (h, ki, 0))],
        out_shape=[jax.ShapeDtypeStruct((t, nh * hd), BF16), jax.ShapeDtypeStruct((t, nh * hd), BF16),
                   jax.ShapeDtypeStruct((nh, t, 1), F32)],
        scratch_shapes=[pltpu.VMEM((tq, hd), F32), pltpu.VMEM((tq, hd), F32), pltpu.VMEM((tq, 1), F32)],
        compiler_params=_cparams(("parallel", "parallel", "arbitrary")),
    )(qkv, qkv, qkv, do, dcol, drow, lse_row, delta_row)


def _head_selector(d, hd):
    return (jnp.arange(d)[:, None] // hd == jnp.arange(LANE)[None, :]).astype(F32)


def _log_sigmoid(x):
    return jnp.minimum(x, 0.0) - jnp.log1p(jnp.exp(-jnp.abs(x)))


def fox_fwd(h, norm_w, w_qkv, w_f, b_f, w_out, tag):
    t, d = h.shape
    nh = d // FOX_HEAD_DIM
    hn = norm_fwd(h, norm_w, f"{tag}_norm")
    qkv = matmul(hn, w_qkv, "nn", BF16, f"{tag}_qkv")
    fr = matmul(hn, w_f, "nn", F32, f"{tag}_f")
    logf = rowwise(lambda x, b: _log_sigmoid(x + b), f"{tag}_logf", [fr], [b_f], [(LANE, F32)])[0]
    dcum = cumsum_rows(logf, f"{tag}_cum").T[:nh]
    dcol, drow = dcum[:, :, None], dcum[:, None, :]
    o, lse = fox_flash_fwd(qkv, dcol, drow, f"{tag}_attn")
    out = matmul(o, w_out, "nn", F32, f"{tag}_out", residual=h)
    return out, (h, hn, qkv, fr, dcol, drow, o, lse)


def fox_bwd(dout, saved, norm_w, w_qkv, w_f, b_f, w_out, tag):
    h, hn, qkv, fr, dcol, drow, o, lse = saved
    t, d = h.shape
    nh = d // FOX_HEAD_DIM
    do = matmul(dout, w_out, "nt", BF16, f"{tag}_do")
    dw_out = matmul(o, dout, "tn", F32, f"{tag}_dwout")
    sel = _head_selector(d, FOX_HEAD_DIM)
    delta = rowwise(lambda a, b, s: _dot(a.astype(F32) * b.astype(F32), s, NN, HI), f"{tag}_delta",
                    [do, o], [sel], [(LANE, F32)])[0].T[:nh]
    dq, dbias_q = fox_flash_dq(qkv, do, dcol, drow, lse, delta[:, :, None], f"{tag}_dq")
    dk, dv, dbias_k = fox_flash_dkv(qkv, do, dcol, drow, jnp.swapaxes(lse, 1, 2), delta[:, None, :], f"{tag}_dkv")
    dlogf_q = cumsum_rows(jnp.pad(dbias_q[:, :, 0].T, ((0, 0), (0, LANE - nh))), f"{tag}_dcum_q", reverse=True)
    dlogf_k = cumsum_rows(jnp.pad(dbias_k[:, :, 0].T, ((0, 0), (0, LANE - nh))), f"{tag}_dcum_k", reverse=True)

    def dlogf_fn(gq, gk, x, b):
        r = (gq + gk) * _sigmoid(-(x + b))
        return r, jnp.sum(r, axis=0, keepdims=True)

    dfr, db_f = rowwise(dlogf_fn, f"{tag}_dlogf", [dlogf_q, dlogf_k, fr], [b_f], [(LANE, BF16)], [(1, LANE)])
    dqkv = jnp.concatenate([dq, dk, dv], axis=1)
    dhn = matmul(dfr, w_f, "nt", F32, f"{tag}_dhn_f")
    dhn = matmul(dqkv, w_qkv, "nt", F32, f"{tag}_dhn", residual=dhn)
    dw_qkv = matmul(hn, dqkv, "tn", F32, f"{tag}_dwqkv")
    dw_f = matmul(hn, dfr, "tn", F32, f"{tag}_dwf")
    dh, dnw = norm_bwd_res(h, norm_w, dhn, dout, f"{tag}_dnorm")
    return dh, dnw, dw_qkv, dw_f, db_f, dw_out


CONV_ROWS = 256


def _shift_rows(cur, halo, shift, up=False):
    if shift == 0:
        return cur
    n = cur.shape[0]
    row = lax.broadcasted_iota(jnp.int32, cur.shape, 0)
    if up:
        return jnp.where(row >= n - shift, pltpu.roll(halo, n - shift, 0), pltpu.roll(cur, n - shift, 0))
    return jnp.where(row < shift, pltpu.roll(halo, shift, 0), pltpu.roll(cur, shift, 0))


def _conv_pre(x, halo, w, b):
    acc = b + w[SSD_CONV - 1:SSD_CONV] * x
    for k in range(SSD_CONV - 1):
        acc = acc + w[k:k + 1] * _shift_rows(x, halo, SSD_CONV - 1 - k)
    return acc


def conv_fwd(x, w, b, name):
    t, cw = x.shape
    tb = min(CONV_ROWS, t)

    def body(x_ref, w_ref, b_ref, o_ref, halo):
        @pl.when(pl.program_id(0) == 0)
        def _():
            halo[...] = jnp.zeros_like(halo)

        xv = x_ref[...]
        o_ref[...] = _silu(_conv_pre(xv, halo[...], w_ref[...], b_ref[...]))
        halo[...] = xv

    blk = pl.BlockSpec((tb, cw), lambda i: (i, 0))
    return pl.pallas_call(
        body, name=name, grid=(t // tb,),
        in_specs=[blk, pl.BlockSpec(w.shape, lambda i: (0, 0)), pl.BlockSpec(b.shape, lambda i: (0, 0))],
        out_specs=blk, out_shape=jax.ShapeDtypeStruct((t, cw), F32), scratch_shapes=[pltpu.VMEM((tb, cw), F32)],
        compiler_params=_cparams(("arbitrary",)),
    )(x, w, b)


def conv_bwd(x, w, b, dact, name):
    t, cw = x.shape
    tb = min(CONV_ROWS, t)
    nb = t // tb

    def body_pre(x_ref, w_ref, b_ref, da_ref, dpre_ref, dw_ref, db_ref, halo):
        first = pl.program_id(0) == 0

        @pl.when(first)
        def _():
            halo[...] = jnp.zeros_like(halo)

        xv, hv = x_ref[...], halo[...]
        dpre = da_ref[...] * _dsilu(_conv_pre(xv, hv, w_ref[...], b_ref[...]))
        dpre_ref[...] = dpre
        dw = jnp.concatenate([jnp.sum(dpre * _shift_rows(xv, hv, SSD_CONV - 1 - k), axis=0, keepdims=True)
                              for k in range(SSD_CONV)], axis=0)
        db = jnp.sum(dpre, axis=0, keepdims=True)

        @pl.when(first)
        def _():
            dw_ref[...] = dw
            db_ref[...] = db

        @pl.when(jnp.logical_not(first))
        def _():
            dw_ref[...] += dw
            db_ref[...] += db

        halo[...] = xv

    blk = pl.BlockSpec((tb, cw), lambda i: (i, 0))
    wspec = pl.BlockSpec(w.shape, lambda i: (0, 0))
    bspec = pl.BlockSpec(b.shape, lambda i: (0, 0))
    dpre, dw, db = pl.pallas_call(
        body_pre, name=f"{name}_pre", grid=(nb,), in_specs=[blk, wspec, bspec, blk], out_specs=[blk, wspec, bspec],
        out_shape=[jax.ShapeDtypeStruct((t, cw), F32), jax.ShapeDtypeStruct(w.shape, F32), jax.ShapeDtypeStruct(b.shape, F32)],
        scratch_shapes=[pltpu.VMEM((tb, cw), F32)], compiler_params=_cparams(("arbitrary",)),
    )(x, w, b, dact)

    def body_dx(dp_ref, w_ref, dx_ref, halo):
        @pl.when(pl.program_id(0) == 0)
        def _():
            halo[...] = jnp.zeros_like(halo)

        dp, wv = dp_ref[...], w_ref[...]
        acc = wv[SSD_CONV - 1:SSD_CONV] * dp
        for k in range(SSD_CONV - 1):
            acc = acc + wv[k:k + 1] * _shift_rows(dp, halo[...], SSD_CONV - 1 - k, up=True)
        dx_ref[...] = acc.astype(dx_ref.dtype)
        halo[...] = dp

    rblk = pl.BlockSpec((tb, cw), lambda i: (nb - 1 - i, 0))
    dx = pl.pallas_call(
        body_dx, name=f"{name}_dx", grid=(nb,), in_specs=[rblk, wspec], out_specs=rblk,
        out_shape=jax.ShapeDtypeStruct((t, cw), BF16), scratch_shapes=[pltpu.VMEM((tb, cw), F32)],
        compiler_params=_cparams(("arbitrary",)),
    )(dpre, w)
    return dx, dw, db


def _tri(n, upper=False):
    r = lax.broadcasted_iota(jnp.int32, (n, n), 0)
    c = lax.broadcasted_iota(jnp.int32, (n, n), 1)
    return (c >= r) if upper else (r >= c)


def _ssd_decay(dtc, dtr, a):
    low = _tri(CHUNK)[None]
    cumc = jnp.sum(jnp.where(low, dtr * a, 0.0), axis=2, keepdims=True)
    cumr = jnp.sum(jnp.where(_tri(CHUNK, upper=True)[None], dtc * a, 0.0), axis=1, keepdims=True)
    return cumc, cumr


def _bdot(a, b, nt=False):
    dims = (((2,), (2,)), ((0,), (0,))) if nt else (((2,), (1,)), ((0,), (0,)))
    return lax.dot_general(a.astype(BF16), b.astype(BF16), dims, preferred_element_type=F32)


def _ssd_specs(d, hpg):
    l, n, p = CHUNK, SSD_STATE, SSD_HEAD_DIM
    ng = d // LANE
    x3 = pl.BlockSpec((hpg, l, p), lambda g, c: (g, c, 0))
    bsp = pl.BlockSpec((l, n), lambda g, c: (c, ng + g))
    csp = pl.BlockSpec((l, n), lambda g, c: (c, ng + SSD_GROUPS + g))
    dtc = pl.BlockSpec((hpg, l, 1), lambda g, c: (g, c, 0))
    dtr = pl.BlockSpec((hpg, 1, 1, l), lambda g, c: (g, c, 0, 0))
    per_head = pl.BlockSpec((hpg, 1, 1), lambda g, c: (g, 0, 0))
    return x3, bsp, csp, dtc, dtr, per_head


def ssd_intra_fwd(x3, xbc, dtc, dtr, a_log, d_skip, name):
    nh, t, p = x3.shape
    hpg = nh // SSD_GROUPS
    d = nh * p

    def body(x_ref, b_ref, c_ref, dtc_ref, dtr_ref, al_ref, ds_ref, y_ref):
        a = -jnp.exp(al_ref[...])
        cumc, cumr = _ssd_decay(dtc_ref[...], dtr_ref[:, 0], a)
        mdec = jnp.exp(jnp.where(_tri(CHUNK)[None], cumc - cumr, -jnp.inf))
        g = _dot(c_ref[...].astype(BF16), b_ref[...].astype(BF16), NT)
        xv = x_ref[...]
        y_ref[...] = _bdot(g[None] * mdec, xv * dtc_ref[...]) + xv * ds_ref[...]

    x3s, bsp, csp, dtcs, dtrs, ph = _ssd_specs(d, hpg)
    return pl.pallas_call(
        body, name=name, grid=(SSD_GROUPS, t // CHUNK), in_specs=[x3s, bsp, csp, dtcs, dtrs, ph, ph], out_specs=x3s,
        out_shape=jax.ShapeDtypeStruct((nh, t, p), F32), compiler_params=_cparams(("parallel", "parallel")),
    )(x3, xbc, xbc, dtc, dtr, a_log, d_skip)


def ssd_intra_bwd(x3, xbc, dtc, dtr, a_log, d_skip, dy3, name):
    nh, t, p = x3.shape
    hpg = nh // SSD_GROUPS
    d = nh * p
    l, n = CHUNK, SSD_STATE

    def body(x_ref, b_ref, c_ref, dtc_ref, dtr_ref, al_ref, ds_ref, dy_ref,
             dx_ref, ddt_ref, db_ref, dc_ref, dal_ref, dds_ref):
        first = pl.program_id(1) == 0
        a = -jnp.exp(al_ref[...])
        dtc_v = dtc_ref[...]
        cumc, cumr = _ssd_decay(dtc_v, dtr_ref[:, 0], a)
        low = _tri(l)[None]
        mdec = jnp.exp(jnp.where(low, cumc - cumr, -jnp.inf))
        up = _tri(l, upper=True)[None]
        mdec_t = jnp.exp(jnp.where(up, cumr - cumc, -jnp.inf))
        bv, cv = b_ref[...].astype(BF16), c_ref[...].astype(BF16)
        g = _dot(cv, bv, NT)
        g_t = _dot(bv, cv, NT)
        xv, dy = x_ref[...], dy_ref[...]
        xd = xv * dtc_v
        dw = _bdot(dy, xd, nt=True)
        dw_t = _bdot(xd, dy, nt=True)
        dxd = _bdot(g_t[None] * mdec_t, dy)
        dx_ref[...] = dy * ds_ref[...] + dxd * dtc_v
        dg = jnp.sum(dw * mdec, axis=0)
        dg_t = jnp.sum(dw_t * mdec_t, axis=0)
        dc_ref[0] = _dot(dg.astype(BF16), bv, NN)
        db_ref[0] = _dot(dg_t.astype(BF16), cv, NN)
        e = dw * mdec * g[None]
        e_t = dw_t * mdec_t * g_t[None]
        dcum_r = jnp.sum(e_t, axis=1, keepdims=True) - jnp.sum(e, axis=1, keepdims=True)
        dda = jnp.sum(jnp.where(up, dcum_r, 0.0), axis=2, keepdims=True)
        ddt_ref[...] = jnp.sum(dxd * xv, axis=2, keepdims=True) + dda * a
        dal = jnp.sum(dda * dtc_v, axis=1, keepdims=True) * a
        dds = jnp.sum(jnp.sum(dy * xv, axis=2, keepdims=True), axis=1, keepdims=True)

        @pl.when(first)
        def _():
            dal_ref[...] = dal
            dds_ref[...] = dds

        @pl.when(jnp.logical_not(first))
        def _():
            dal_ref[...] += dal
            dds_ref[...] += dds

    x3s, bsp, csp, dtcs, dtrs, ph = _ssd_specs(d, hpg)
    grp = pl.BlockSpec((1, l, n), lambda g, c: (g, c, 0))
    return pl.pallas_call(
        body, name=name, grid=(SSD_GROUPS, t // l),
        in_specs=[x3s, bsp, csp, dtcs, dtrs, ph, ph, x3s], out_specs=[x3s, dtcs, grp, grp, ph, ph],
        out_shape=[jax.ShapeDtypeStruct((nh, t, p), F32), jax.ShapeDtypeStruct((nh, t, 1), F32),
                   jax.ShapeDtypeStruct((SSD_GROUPS, t, n), F32), jax.ShapeDtypeStruct((SSD_GROUPS, t, n), F32),
                   jax.ShapeDtypeStruct((nh, 1, 1), F32), jax.ShapeDtypeStruct((nh, 1, 1), F32)],
        compiler_params=_cparams(("parallel", "arbitrary")),
    )(x3, xbc, xbc, dtc, dtr, a_log, d_skip, dy3)


def _ssd_state_common(x, dt, a_lane):
    l = CHUNK
    da = dt * a_lane
    cum = _dot(_tri(l).astype(F32), da, NN, HI)
    cend = cum[l - 1:l]
    ec = jnp.exp(cum)
    te = jnp.exp(cend - cum)
    cd_col = jnp.exp(_dot(da, jnp.ones((l, SSD_STATE), F32), TN, HI))
    return cum, cend, ec, te, cd_col


def ssd_state_fwd(xbc, dt_lane, a_lane, name):
    t = xbc.shape[0]
    d = dt_lane.shape[1]
    gw = d // SSD_GROUPS
    l, n = CHUNK, SSD_STATE
    nc = t // l
    ng = d // LANE

    def body(x_ref, b_ref, c_ref, dt_ref, a_ref, y_ref, sp_ref, s_s):
        @pl.when(pl.program_id(1) == 0)
        def _():
            s_s[...] = jnp.zeros_like(s_s)

        xv, dt = x_ref[...], dt_ref[...]
        cum, cend, ec, te, cd_col = _ssd_state_common(xv, dt, a_ref[...])
        s_prev = s_s[...]
        sp_ref[0] = s_prev.astype(BF16)
        y_ref[...] = _dot(c_ref[...].astype(BF16), s_prev.astype(BF16), NT) * ec
        xt = (xv * dt * te).astype(BF16)
        s_s[...] = s_prev * cd_col + _dot(xt, b_ref[...].astype(BF16), TN)

    xs = pl.BlockSpec((l, gw), lambda g, c: (c, g))
    return pl.pallas_call(
        body, name=name, grid=(SSD_GROUPS, nc),
        in_specs=[xs, pl.BlockSpec((l, n), lambda g, c: (c, ng + g)), pl.BlockSpec((l, n), lambda g, c: (c, ng + SSD_GROUPS + g)),
                  xs, pl.BlockSpec((1, gw), lambda g, c: (0, g))],
        out_specs=[xs, pl.BlockSpec((1, gw, n), lambda g, c: (c, g, 0))],
        out_shape=[jax.ShapeDtypeStruct((t, d), F32), jax.ShapeDtypeStruct((nc, d, n), BF16)],
        scratch_shapes=[pltpu.VMEM((gw, n), F32)],
        compiler_params=_cparams(("parallel", "arbitrary")),
    )(xbc, xbc, xbc, dt_lane, a_lane)


def ssd_state_bwd(xbc, dt_lane, a_lane, s_prev_all, dy, name):
    t = xbc.shape[0]
    d = dt_lane.shape[1]
    gw = d // SSD_GROUPS
    l, n = CHUNK, SSD_STATE
    nc = t // l
    ng = d // LANE

    def body(x_ref, b_ref, c_ref, dt_ref, a_ref, sp_ref, dy_ref, dx_ref, ddt_ref, db_ref, dc_ref, da_ref, ds_s):
        first = pl.program_id(1) == 0

        @pl.when(first)
        def _():
            ds_s[...] = jnp.zeros_like(ds_s)

        xv, dt, a_lane_v = x_ref[...], dt_ref[...], a_ref[...]
        cum, cend, ec, te, cd_col = _ssd_state_common(xv, dt, a_lane_v)
        bv, cv = b_ref[...].astype(BF16), c_ref[...].astype(BF16)
        s_prev = sp_ref[0]
        ds_next = ds_s[...]
        dyv = dy_ref[...]
        z = _dot(cv, s_prev, NT)
        dz = (dyv * ec).astype(BF16)
        dc_ref[...] = _dot(dz, s_prev, NN)
        xd = xv * dt
        dxt = _dot(bv, ds_next.astype(BF16), NT)
        db_ref[...] = _dot((xd * te).astype(BF16), ds_next.astype(BF16), NN)
        dcd = _dot(jnp.ones((8, n), F32), ds_next * s_prev.astype(F32), NT, HI)[0:1]
        dte_te = dxt * xd * te
        last = lax.broadcasted_iota(jnp.int32, (l, gw), 0) == l - 1
        dcum = dyv * z * ec - dte_te + jnp.where(last, jnp.sum(dte_te, axis=0, keepdims=True) + dcd * jnp.exp(cend), 0.0)
        dda = _dot(_tri(l, upper=True).astype(F32), dcum, NN, HI)
        dxd = dxt * te
        dx_ref[...] = dxd * dt
        ddt_ref[...] = dxd * xv + dda * a_lane_v
        da = jnp.sum(dda * dt, axis=0, keepdims=True)
        ds_s[...] = ds_next * cd_col + _dot(dz, cv, TN)

        @pl.when(first)
        def _():
            da_ref[...] = da

        @pl.when(jnp.logical_not(first))
        def _():
            da_ref[...] += da

    rc = lambda c: nc - 1 - c
    xs = pl.BlockSpec((l, gw), lambda g, c: (rc(c), g))
    gs = pl.BlockSpec((l, n), lambda g, c: (rc(c), g))
    return pl.pallas_call(
        body, name=name, grid=(SSD_GROUPS, nc),
        in_specs=[xs, pl.BlockSpec((l, n), lambda g, c: (rc(c), ng + g)),
                  pl.BlockSpec((l, n), lambda g, c: (rc(c), ng + SSD_GROUPS + g)),
                  xs, pl.BlockSpec((1, gw), lambda g, c: (0, g)),
                  pl.BlockSpec((1, gw, n), lambda g, c: (rc(c), g, 0)), xs],
        out_specs=[xs, xs, gs, gs, pl.BlockSpec((1, gw), lambda g, c: (0, g))],
        out_shape=[jax.ShapeDtypeStruct((t, d), F32), jax.ShapeDtypeStruct((t, d), F32),
                   jax.ShapeDtypeStruct((t, SSD_GROUPS * n), F32), jax.ShapeDtypeStruct((t, SSD_GROUPS * n), F32),
                   jax.ShapeDtypeStruct((1, d), F32)],
        scratch_shapes=[pltpu.VMEM((gw, n), F32)],
        compiler_params=_cparams(("parallel", "arbitrary")),
    )(xbc, xbc, xbc, dt_lane, a_lane, s_prev_all, dy)


def _hgrn_common(q, fr, lb):
    l = CHUNK
    sig = _sigmoid(fr)
    f = lb + (1.0 - lb) * sig
    kk = 1.0 - f
    lf = jnp.log(f)
    cum = _dot(_tri(l).astype(F32), lf, NN, HI)
    mid = cum[l // 2 - 1:l // 2]
    cend = cum[l - 1:l]
    qf = _silu(q)
    eq, ek, ee, ec = jnp.exp(cum - mid), jnp.exp(mid - cum), jnp.exp(cend - cum), jnp.exp(cum)
    cd_col = jnp.exp(_dot(lf, jnp.ones((l, HGRN_KDIM), F32), TN, HI))
    return sig, f, kk, qf, eq, ek, ee, ec, cend, cd_col


def hgrn_fwd(qfvg, lb, name):
    t = qfvg.shape[0]
    d = lb.shape[1]
    l, kd = CHUNK, HGRN_KDIM
    nh = d // kd
    nc = t // l

    def body(q_ref, f_ref, v_ref, lb_ref, o_ref, sp_ref, s_s):
        @pl.when(pl.program_id(1) == 0)
        def _():
            s_s[...] = jnp.zeros_like(s_s)

        sig, f, kk, qf, eq, ek, ee, ec, cend, cd_col = _hgrn_common(q_ref[...], f_ref[...], lb_ref[...])
        v = v_ref[...].astype(BF16)
        s_prev = s_s[...]
        sp_ref[0] = s_prev.astype(BF16)
        att = jnp.where(_tri(l), _dot((qf * eq).astype(BF16), (kk * ek).astype(BF16), NT), 0.0)
        o_ref[...] = _dot(att.astype(BF16), v, NN) + _dot((qf * ec).astype(BF16), s_prev.astype(BF16), NN)
        s_s[...] = s_prev * cd_col + _dot((kk * ee).astype(BF16), v, TN)

    def col(j):
        return pl.BlockSpec((l, kd), lambda h, c: (c, j * nh + h))

    return pl.pallas_call(
        body, name=name, grid=(nh, nc),
        in_specs=[col(0), col(1), col(2), pl.BlockSpec((1, kd), lambda h, c: (0, h))],
        out_specs=[col(0), pl.BlockSpec((1, kd, kd), lambda h, c: (c, h, 0))],
        out_shape=[jax.ShapeDtypeStruct((t, d), F32), jax.ShapeDtypeStruct((nc, d, kd), BF16)],
        scratch_shapes=[pltpu.VMEM((kd, kd), F32)],
        compiler_params=_cparams(("parallel", "arbitrary")),
    )(qfvg, qfvg, qfvg, lb)


def hgrn_bwd(qfvg, lb, s_prev_all, do, name):
    t = qfvg.shape[0]
    d = lb.shape[1]
    l, kd = CHUNK, HGRN_KDIM
    nh = d // kd
    nc = t // l

    def body(q_ref, f_ref, v_ref, lb_ref, sp_ref, do_ref, dq_ref, df_ref, dv_ref, dlb_ref, ds_s):
        first = pl.program_id(1) == 0

        @pl.when(first)
        def _():
            ds_s[...] = jnp.zeros_like(ds_s)

        q, lbv = q_ref[...], lb_ref[...]
        sig, f, kk, qf, eq, ek, ee, ec, cend, cd_col = _hgrn_common(q, f_ref[...], lbv)
        v = v_ref[...].astype(BF16)
        dov = do_ref[...].astype(BF16)
        s_prev = sp_ref[0]
        ds_next = ds_s[...]
        ds_b = ds_next.astype(BF16)
        qr, kr, ke, qe = qf * eq, kk * ek, kk * ee, qf * ec
        low = _tri(l)
        att = jnp.where(low, _dot(qr.astype(BF16), kr.astype(BF16), NT), 0.0).astype(BF16)
        datt = jnp.where(low, _dot(dov, v, NT), 0.0).astype(BF16)
        dqe = _dot(dov, s_prev, NT)
        dke = _dot(v, ds_b, NT)
        dqr = _dot(datt, kr.astype(BF16), NN)
        dkr = _dot(datt, qr.astype(BF16), TN)
        dv_ref[...] = (_dot(ke.astype(BF16), ds_b, NN) + _dot(att, dov, TN)).astype(dv_ref.dtype)
        dcd = _dot(jnp.ones((8, kd), F32), ds_next * s_prev.astype(F32), NT, HI)[0:1]
        row = lax.broadcasted_iota(jnp.int32, (l, kd), 0)
        a_q, a_k, a_e, a_c = dqr * qr, dkr * kr, dke * ke, dqe * qe
        dmid = jnp.sum(a_k - a_q, axis=0, keepdims=True)
        dcend = jnp.sum(a_e, axis=0, keepdims=True) + dcd * jnp.exp(cend)
        dcum = a_q - a_k - a_e + a_c + jnp.where(row == l // 2 - 1, dmid, 0.0) + jnp.where(row == l - 1, dcend, 0.0)
        dlf = _dot(_tri(l, upper=True).astype(F32), dcum, NN, HI)
        df = dlf / f - (dkr * ek + dke * ee)
        df_ref[...] = (df * (1.0 - lbv) * sig * (1.0 - sig)).astype(df_ref.dtype)
        dq_ref[...] = ((dqr * eq + dqe * ec) * _dsilu(q)).astype(dq_ref.dtype)
        dlb = jnp.sum(df * (1.0 - sig), axis=0, keepdims=True)
        ds_s[...] = ds_next * cd_col + _dot(qe.astype(BF16), dov, TN)

        @pl.when(first)
        def _():
            dlb_ref[...] = dlb

        @pl.when(jnp.logical_not(first))
        def _():
            dlb_ref[...] += dlb

    def col(j):
        return pl.BlockSpec((l, kd), lambda h, c: (nc - 1 - c, j * nh + h))

    head = pl.BlockSpec((1, kd), lambda h, c: (0, h))
    return pl.pallas_call(
        body, name=name, grid=(nh, nc),
        in_specs=[col(0), col(1), col(2), head, pl.BlockSpec((1, kd, kd), lambda h, c: (nc - 1 - c, h, 0)), col(0)],
        out_specs=[col(0), col(0), col(0), head],
        out_shape=[jax.ShapeDtypeStruct((t, d), BF16)] * 3 + [jax.ShapeDtypeStruct((1, d), F32)],
        scratch_shapes=[pltpu.VMEM((kd, kd), F32)],
        compiler_params=_cparams(("parallel", "arbitrary")),
    )(qfvg, qfvg, qfvg, lb, s_prev_all, do)


def _softplus(x):
    return jnp.maximum(x, 0.0) + jnp.log1p(jnp.exp(-jnp.abs(x)))


def _grouped(fn, width, *arrs):
    n = arrs[0].shape[1] // width
    outs = [fn(*[a[:, i * width:(i + 1) * width] for a in arrs]) for i in range(n)]
    if isinstance(outs[0], tuple):
        return tuple(jnp.concatenate([o[j] for o in outs], axis=1) for j in range(len(outs[0])))
    return jnp.concatenate(outs, axis=1)


def mixer_fwd(h, norm_w, wts, conv_w, conv_b, dt_bias, a_log, d_skip, ssd_norm_w, lb, hgrn_norm_w, w_out, tag):
    t, d = h.shape
    nh = d // SSD_HEAD_DIM
    p = SSD_HEAD_DIM
    w_z, w_xbc, w_qfvg, w_dt = wts
    hn = norm_fwd(h, norm_w, f"{tag}_norm")
    z = matmul(hn, w_z, "nn", F32, f"{tag}_z")
    xbc_raw = matmul(hn, w_xbc, "nn", F32, f"{tag}_xbc")
    qfvg = matmul(hn, w_qfvg, "nn", F32, f"{tag}_qfvg")
    dt_raw = matmul(hn, w_dt, "nn", F32, f"{tag}_dt")
    xbc = conv_fwd(xbc_raw, conv_w, conv_b, f"{tag}_conv")
    dt = rowwise(lambda x, b: _softplus(x + b), f"{tag}_softplus", [dt_raw], [dt_bias], [(LANE, F32)])[0]
    dt_h = dt[:, :nh]
    dtr = dt_h.T.reshape(nh, t // CHUNK, 1, CHUNK)
    dtc = dt_h.T[:, :, None]
    dt_lane = jnp.repeat(dt_h, p, axis=1)
    a_lane = jnp.repeat(-jnp.exp(a_log[:, :nh]), p, axis=1)
    al3 = a_log[0, :nh].reshape(nh, 1, 1)
    ds3 = d_skip.reshape(nh, 1, 1)
    x3 = xbc[:, :d].reshape(t, nh, p).transpose(1, 0, 2)
    y3 = ssd_intra_fwd(x3, xbc, dtc, dtr, al3, ds3, f"{tag}_ssd_intra")
    y_off, s_ssd = ssd_state_fwd(xbc, dt_lane, a_lane, f"{tag}_ssd_state")
    y_diag = y3.transpose(1, 0, 2).reshape(t, d)
    o_b, s_hgrn = hgrn_fwd(qfvg, lb, f"{tag}_hgrn")
    gw = d // SSD_GROUPS

    def gate(yd, yo, z_, o, g_, nw_a, nw_b):
        ya = (yd + yo) * _silu(z_)
        ya = _grouped(lambda a, w: a * _rms(a) * w, gw, ya, nw_a)
        yb = _grouped(lambda a, w: a * _rms(a) * w, HGRN_KDIM, o, nw_b) * _silu(g_)
        return jnp.concatenate([ya, yb], axis=1)

    cat = rowwise(gate, f"{tag}_gate", [y_diag, y_off, z, o_b, (qfvg, d, 3)], [ssd_norm_w, hgrn_norm_w], [(2 * d, BF16)], tm=128)[0]
    out = matmul(cat, w_out, "nn", F32, f"{tag}_out", residual=h)
    saved = (h, hn, z, xbc_raw, qfvg, dt_raw, xbc, dtc, dtr, dt_lane, a_lane, al3, ds3, x3, y_diag, y_off, s_ssd, o_b, s_hgrn, cat)
    return out, saved


def mixer_bwd(dout, saved, norm_w, wts, conv_w, conv_b, dt_bias, a_log, ssd_norm_w, lb, hgrn_norm_w, w_out, tag):
    (h, hn, z, xbc_raw, qfvg, dt_raw, xbc, dtc, dtr, dt_lane, a_lane, al3, ds3, x3, y_diag, y_off, s_ssd, o_b, s_hgrn, cat) = saved
    t, d = h.shape
    nh = d // SSD_HEAD_DIM
    p = SSD_HEAD_DIM
    gw = d // SSD_GROUPS
    w_z, w_xbc, w_qfvg, w_dt = wts
    dcat = matmul(dout, w_out, "nt", F32, f"{tag}_dcat")
    dw_out = matmul(cat, dout, "tn", F32, f"{tag}_dwout")

    def gate_bwd(dya_n, dyb_g, yd, yo, z_, o, g_, nw_a, nw_b):
        y = yd + yo
        sz = _silu(z_)
        dya, dnw_a = _grouped(lambda a, w, dy: _norm_bwd(a, w, dy), gw, y * sz, nw_a, dya_n)
        sg = _silu(g_)
        tb = _grouped(lambda a, w: a * _rms(a) * w, HGRN_KDIM, o, nw_b)
        do_, dnw_b = _grouped(lambda a, w, dy: _norm_bwd(a, w, dy), HGRN_KDIM, o, nw_b, dyb_g * sg)
        return dya * sz, dya * y * _dsilu(z_), do_, dyb_g * tb * _dsilu(g_), dnw_a, dnw_b

    dy, dz, do_b, dg, dnw_a, dnw_b = rowwise(
        gate_bwd, f"{tag}_dgate", [(dcat, d, 0), (dcat, d, 1), y_diag, y_off, z, o_b, (qfvg, d, 3)],
        [ssd_norm_w, hgrn_norm_w], [(d, F32), (d, BF16), (d, F32), (d, BF16)], [(1, d), (1, d)], tm=128)
    dq, dfr, dv, dlb = hgrn_bwd(qfvg, lb, s_hgrn, do_b, f"{tag}_dhgrn")
    dqfvg = jnp.concatenate([dq, dfr, dv, dg], axis=1)
    dy3 = dy.reshape(t, nh, p).transpose(1, 0, 2)
    dx3, ddt3, db_a, dc_a, dal_a, dds = ssd_intra_bwd(x3, xbc, dtc, dtr, al3, ds3, dy3, f"{tag}_dssd_intra")
    dx_s, ddt_lane, db_s, dc_s, da_lane = ssd_state_bwd(xbc, dt_lane, a_lane, s_ssd, dy, f"{tag}_dssd_state")
    n = SSD_STATE
    dxbc_act_parts = (dx3.transpose(1, 0, 2).reshape(t, d), dx_s,
                      db_a.transpose(1, 0, 2).reshape(t, SSD_GROUPS * n), db_s,
                      dc_a.transpose(1, 0, 2).reshape(t, SSD_GROUPS * n), dc_s)
    sel = _head_selector(d, p)
    ddt_a = jnp.pad(ddt3[:, :, 0].T, ((0, 0), (0, LANE - nh)))

    def dt_bwd(ddl, dda, x, b, s):
        r = (_dot(ddl, s, NN, HI) + dda) * _sigmoid(x + b)
        return r, jnp.sum(r, axis=0, keepdims=True)

    ddt_raw, ddt_bias = rowwise(dt_bwd, f"{tag}_ddt", [ddt_lane, ddt_a, dt_raw], [dt_bias, sel], [(LANE, BF16)], [(1, LANE)])
    a_pad = -jnp.exp(a_log)
    dal_a_row = jnp.pad(dal_a.reshape(1, nh), ((0, 0), (0, LANE - nh)))
    dalog = rowwise(lambda dal, da, a, s: dal + _dot(da, s, NN, HI) * a, f"{tag}_dalog",
                    [dal_a_row, da_lane, a_pad], [sel], [(LANE, F32)])[0]
    dxbc_act = rowwise(lambda x1, x2, b1, b2, c1, c2: jnp.concatenate([x1 + x2, b1 + b2, c1 + c2], axis=1),
                       f"{tag}_dxbc_sum", list(dxbc_act_parts), [], [(d + 2 * SSD_GROUPS * n, F32)], tm=128)[0]
    dxbc_raw, dconv_w, dconv_b = conv_bwd(xbc_raw, conv_w, conv_b, dxbc_act, f"{tag}_dconv")
    dhn = matmul(dz, w_z, "nt", F32, f"{tag}_dhn_z")
    dhn = matmul(dxbc_raw, w_xbc, "nt", F32, f"{tag}_dhn_xbc", residual=dhn)
    dhn = matmul(dqfvg, w_qfvg, "nt", F32, f"{tag}_dhn_qfvg", residual=dhn)
    dhn = matmul(ddt_raw, w_dt, "nt", F32, f"{tag}_dhn_dt", residual=dhn)
    dw_z = matmul(hn, dz, "tn", F32, f"{tag}_dwz")
    dw_xbc = matmul(hn, dxbc_raw, "tn", F32, f"{tag}_dwxbc")
    dw_qfvg = matmul(hn, dqfvg, "tn", F32, f"{tag}_dwqfvg")
    dw_dt = matmul(hn, ddt_raw, "tn", F32, f"{tag}_dwdt")
    dh, dnw = norm_bwd_res(h, norm_w, dhn, dout, f"{tag}_dnorm")
    grads = dict(mix_norm=dnw, w_z=dw_z, w_xbc=dw_xbc, w_qfvg=dw_qfvg, w_dt=dw_dt, conv_w=dconv_w, conv_b=dconv_b,
                 dt_bias=ddt_bias, a_log=dalog, d_skip=dds.reshape(1, nh), ssd_norm=dnw_a, lb=dlb, hgrn_norm=dnw_b,
                 w_out=dw_out)
    return dh, grads


ANY = pl.BlockSpec(memory_space=pl.ANY)


def _place():
    x, y, c = lax.axis_index("x"), lax.axis_index("y"), lax.axis_index("c")
    chips = [(1 - x, y), (x, 1 - y), (1 - x, 1 - y)]
    return x, y, c, chips


def _rcopy(src, dst, send_sems, recv_sems, k, dev):
    return pltpu.make_async_remote_copy(src_ref=src, dst_ref=dst, send_sem=send_sems.at[k], recv_sem=recv_sems.at[k],
                                        device_id=dev, device_id_type=MESH_ID)


def gather_weights(wsh):
    def body(w_ref, o_ref, send_sems, recv_sems, local_sem):
        x, y, c, chips = _place()
        me = 2 * x + y
        sib = (x, y, 1 - c)
        mine = pltpu.make_async_copy(w_ref, o_ref.at[me], local_sem)
        mine.start()
        first = [_rcopy(w_ref.at[c], o_ref.at[me, c], send_sems, recv_sems, j, (cx, cy, c)) for j, (cx, cy) in enumerate(chips)]
        for cp in first:
            cp.start()
        passed = []
        for j, (cx, cy) in enumerate(chips):
            blk = o_ref.at[2 * cx + cy, c]
            _rcopy(blk, blk, send_sems, recv_sems, j, sib).wait_recv()
            cp = _rcopy(blk, blk, send_sems, recv_sems, 3 + j, sib)
            cp.start()
            passed.append(cp)
        for j, (cx, cy) in enumerate(chips):
            blk = o_ref.at[2 * cx + cy, 1 - c]
            _rcopy(blk, blk, send_sems, recv_sems, 3 + j, sib).wait_recv()
        for cp in first + passed:
            cp.wait_send()
        mine.wait()

    return pl.pallas_call(
        body, name="gather_weights", in_specs=[ANY], out_specs=ANY,
        out_shape=jax.ShapeDtypeStruct((4,) + wsh.shape, wsh.dtype),
        scratch_shapes=[pltpu.SemaphoreType.DMA((6,)), pltpu.SemaphoreType.DMA((6,)), pltpu.SemaphoreType.DMA],
    )(wsh)


def exchange_halves(gh):
    def body(g_ref, o_ref, send_sems, recv_sems):
        x, y, c, _ = _place()
        cp = _rcopy(g_ref.at[1 - c], o_ref, send_sems, recv_sems, 0, (x, y, 1 - c))
        cp.start()
        cp.wait()

    return pl.pallas_call(
        body, name="rs_exchange_halves", in_specs=[ANY], out_specs=ANY,
        out_shape=jax.ShapeDtypeStruct(gh.shape[1:], gh.dtype),
        scratch_shapes=[pltpu.SemaphoreType.DMA((1,)), pltpu.SemaphoreType.DMA((1,))],
    )(gh)


RS_ROWS = 256


def add_own_half(gh, other, c_idx):
    _, nchip, hr, w = gh.shape
    tr = _pick(hr, (RS_ROWS, 128, 64, 32, 16, 8))

    def body(c_ref, a_ref, b_ref, o_ref):
        o_ref[...] = a_ref[0] + b_ref[...]

    grid_spec = pltpu.PrefetchScalarGridSpec(
        num_scalar_prefetch=1, grid=(nchip, hr // tr),
        in_specs=[pl.BlockSpec((1, 1, tr, w), lambda k, i, c_ref: (c_ref[0], k, i, 0)),
                  pl.BlockSpec((1, tr, w), lambda k, i, c_ref: (k, i, 0))],
        out_specs=pl.BlockSpec((1, tr, w), lambda k, i, c_ref: (k, i, 0)))
    return pl.pallas_call(
        body, name="rs_add_own_half", grid_spec=grid_spec, out_shape=jax.ShapeDtypeStruct((nchip, hr, w), F32),
        compiler_params=_cparams(("parallel", "parallel")),
    )(c_idx, gh, other)


def scatter_to_chips(part):
    def body(p_ref, o_ref, send_sems, recv_sems, local_sem):
        x, y, c, chips = _place()
        me = 2 * x + y
        mine = pltpu.make_async_copy(p_ref.at[me], o_ref.at[me], local_sem)
        mine.start()
        cps = [_rcopy(p_ref.at[2 * cx + cy], o_ref.at[me], send_sems, recv_sems, j, (cx, cy, c)) for j, (cx, cy) in enumerate(chips)]
        for cp in cps:
            cp.start()
        for j, (cx, cy) in enumerate(chips):
            blk = o_ref.at[2 * cx + cy]
            _rcopy(blk, blk, send_sems, recv_sems, j, (cx, cy, c)).wait_recv()
        for cp in cps:
            cp.wait_send()
        mine.wait()

    return pl.pallas_call(
        body, name="rs_scatter_to_chips", in_specs=[ANY], out_specs=ANY, out_shape=jax.ShapeDtypeStruct(part.shape, part.dtype),
        scratch_shapes=[pltpu.SemaphoreType.DMA((3,)), pltpu.SemaphoreType.DMA((3,)), pltpu.SemaphoreType.DMA],
    )(part)


def sum_chips(slots):
    nchip, hr, w = slots.shape
    tr = _pick(hr, (RS_ROWS, 128, 64, 32, 16, 8))

    def body(s_ref, o_ref):
        acc = s_ref[0]
        for k in range(1, nchip):
            acc = acc + s_ref[k]
        o_ref[...] = acc

    return pl.pallas_call(
        body, name="rs_sum_chips", grid=(hr // tr,), in_specs=[pl.BlockSpec((nchip, tr, w), lambda i: (0, i, 0))],
        out_specs=pl.BlockSpec((tr, w), lambda i: (i, 0)), out_shape=jax.ShapeDtypeStruct((hr, w), F32),
        compiler_params=_cparams(("parallel",)),
    )(slots)


def share_with_sibling(half):
    def body(h_ref, o_ref, send_sems, recv_sems, local_sem):
        x, y, c, _ = _place()
        mine = pltpu.make_async_copy(h_ref, o_ref.at[c], local_sem)
        mine.start()
        cp = _rcopy(h_ref, o_ref.at[c], send_sems, recv_sems, 0, (x, y, 1 - c))
        cp.start()
        blk = o_ref.at[1 - c]
        _rcopy(blk, blk, send_sems, recv_sems, 0, (x, y, 1 - c)).wait_recv()
        cp.wait_send()
        mine.wait()

    return pl.pallas_call(
        body, name="rs_share_with_sibling", in_specs=[ANY], out_specs=ANY,
        out_shape=jax.ShapeDtypeStruct((2,) + half.shape, half.dtype),
        scratch_shapes=[pltpu.SemaphoreType.DMA((1,)), pltpu.SemaphoreType.DMA((1,)), pltpu.SemaphoreType.DMA],
    )(half)


def reduce_scatter_grads(gh, c_idx):
    other = exchange_halves(gh)
    part = add_own_half(gh, other, c_idx)
    slots = scatter_to_chips(part)
    half = sum_chips(slots)
    both = share_with_sibling(half)
    return both.reshape(2 * half.shape[0], half.shape[1])


def allreduce_small(v, name):
    rows, w = v.shape

    def body(v_ref, o_ref, slots, send_sems, recv_sems):
        x, y, c, _ = _place()
        me = 4 * x + 2 * y + c
        slots[me] = v_ref[...]
        cps = []
        for r in range(1, 8):
            dev = (x ^ (r >> 2), y ^ ((r >> 1) & 1), c ^ (r & 1))
            cp = _rcopy(v_ref, slots.at[me], send_sems, recv_sems, r - 1, dev)
            cp.start()
            cps.append(cp)
        for r in range(1, 8):
            blk = slots.at[me ^ r]
            _rcopy(blk, blk, send_sems, recv_sems, r - 1, (x, y, c)).wait_recv()
        for cp in cps:
            cp.wait_send()
        acc = slots[0]
        for k in range(1, 8):
            acc = acc + slots[k]
        o_ref[...] = acc

    vm = pl.BlockSpec(memory_space=pltpu.VMEM)
    return pl.pallas_call(
        body, name=name, in_specs=[vm], out_specs=vm, out_shape=jax.ShapeDtypeStruct((rows, w), F32),
        scratch_shapes=[pltpu.VMEM((8, rows, w), F32), pltpu.SemaphoreType.DMA((7,)), pltpu.SemaphoreType.DMA((7,))],
    )(v)


def adamw(w, g, m, v, name):
    def fn(w_, g_, m_, v_):
        m2 = ADAM_B1 * m_ + (1.0 - ADAM_B1) * g_
        v2 = ADAM_B2 * v_ + (1.0 - ADAM_B2) * (g_ * g_)
        m_hat = m2 / (1.0 - ADAM_B1 ** ADAM_STEP)
        v_hat = v2 / (1.0 - ADAM_B2 ** ADAM_STEP)
        return -ADAM_LR * (m_hat / (jnp.sqrt(v_hat) + ADAM_EPS) + ADAM_WD * w_), m2, v2

    cols = w.shape[1]
    tm = _pick(w.shape[0], (64, 32, 16, 8) if cols > 1024 else (256, 128, 64, 32, 16, 8))
    return rowwise(fn, name, [w, g, m, v], [], [(cols, F32)] * 3, tm=tm)


def _pack_rows(arrs, row_align):
    parts, offs, r = [], [], 0
    for a in arrs:
        n = a.size
        nr = -(-n // (FLAT_W * row_align)) * row_align
        parts.append(jnp.pad(a.reshape(-1), (0, nr * FLAT_W - n)).reshape(nr, FLAT_W))
        offs.append(r)
        r += nr
    return jnp.concatenate(parts, axis=0), offs


def _unpack_rows(packed, offs, shapes):
    out = []
    for o, s in zip(offs, shapes):
        n = 1
        for k in s:
            n *= k
        nr = -(-n // FLAT_W)
        out.append(packed[..., o:o + nr, :].reshape(packed.shape[:-2] + (nr * FLAT_W,))[..., :n].reshape(packed.shape[:-2] + tuple(s)))
    return out


BIG = ("ffn1_w_in", "ffn1_w_out", "ab_w_in", "ab_w_out", "fox_w_in", "fox_w_out", "ffn2_w_in", "ffn2_w_out",
       "ple_w_gate", "ple_w_up")
COL_SHARDED = ("ffn1_w_in", "ab_w_in", "fox_w_in", "ffn2_w_in", "ple_w_up")
SMALL = ("ffn1_norm", "mix_norm", "ssd_conv_w", "ssd_conv_b", "ssd_dt_bias", "ssd_a_log", "ssd_d", "ssd_norm",
         "hgrn_lb_logits", "hgrn_norm", "fox_b_f", "ffn2_norm", "ple_gate_norm", "ple_norm", "final_norm")
WEIGHTS = ("ffn1_norm", "ffn1_w_in", "ffn1_w_out", "mix_norm", "ab_w_in", "ssd_conv_w", "ssd_conv_b", "ssd_dt_bias",
           "ssd_a_log", "ssd_d", "ssd_norm", "hgrn_lb_logits", "hgrn_norm", "ab_w_out", "fox_w_in", "fox_b_f", "fox_w_out",
           "ffn2_norm", "ffn2_w_in", "ffn2_w_out", "ple_gate_norm", "ple_w_gate", "ple_w_up", "ple_norm", "final_norm")


def _full_from_shards(name, g4):
    if name in COL_SHARDED:
        return jnp.moveaxis(g4, 0, 2).reshape(g4.shape[1], g4.shape[2], 4 * g4.shape[3])
    return jnp.moveaxis(g4, 0, 1).reshape(g4.shape[1], 4 * g4.shape[2], g4.shape[3])


def _shards_from_full(name, full):
    ly, r, c = full.shape
    if name in COL_SHARDED:
        return jnp.moveaxis(full.reshape(ly, r, 4, c // 4), 2, 0)
    return jnp.moveaxis(full.reshape(ly, 4, r // 4, c), 1, 0)


def _pad_to(a, axis, n):
    pad = [(0, 0)] * a.ndim
    pad[axis] = (0, n - a.shape[axis])
    return jnp.pad(a, pad)


def _lane_row(v):
    return _pad_to(v.reshape(1, -1), 1, LANE)


def kernel(x, p, ffn1_norm, ffn1_w_in, ffn1_w_out, mix_norm, ab_w_in, ssd_conv_w, ssd_conv_b, ssd_dt_bias, ssd_a_log, ssd_d, ssd_norm, hgrn_lb_logits, hgrn_norm, ab_w_out, fox_w_in, fox_b_f, fox_w_out, ffn2_norm, ffn2_w_in, ffn2_w_out, ple_gate_norm, ple_w_gate, ple_w_up, ple_norm, final_norm, loss_target, m_ffn1_norm, m_ffn1_w_in, m_ffn1_w_out, m_mix_norm, m_ab_w_in, m_ssd_conv_w, m_ssd_conv_b, m_ssd_dt_bias, m_ssd_a_log, m_ssd_d, m_ssd_norm, m_hgrn_lb_logits, m_hgrn_norm, m_ab_w_out, m_fox_w_in, m_fox_b_f, m_fox_w_out, m_ffn2_norm, m_ffn2_w_in, m_ffn2_w_out, m_ple_gate_norm, m_ple_w_gate, m_ple_w_up, m_ple_norm, m_final_norm, v_ffn1_norm, v_ffn1_w_in, v_ffn1_w_out, v_mix_norm, v_ab_w_in, v_ssd_conv_w, v_ssd_conv_b, v_ssd_dt_bias, v_ssd_a_log, v_ssd_d, v_ssd_norm, v_hgrn_lb_logits, v_hgrn_norm, v_ab_w_out, v_fox_w_in, v_fox_b_f, v_fox_w_out, v_ffn2_norm, v_ffn2_w_in, v_ffn2_w_out, v_ple_gate_norm, v_ple_w_gate, v_ple_w_up, v_ple_norm, v_final_norm):
    given = dict(locals())
    w = {n: given[n] for n in WEIGHTS}
    mom = {n: given["m_" + n] for n in WEIGHTS}
    var = {n: given["v_" + n] for n in WEIGHTS}
    h = x[0]
    t, d = h.shape
    depth = p.shape[0]
    nh_ssd = d // SSD_HEAD_DIM
    nh_fox = d // FOX_HEAD_DIM
    conv_dim = d + 2 * SSD_GROUPS * SSD_STATE
    d_ff = ffn1_w_out.shape[1] * 4
    fp = -(-d_ff // FF_ALIGN) * FF_ALIGN
    xi, yi, ci = lax.axis_index("x"), lax.axis_index("y"), lax.axis_index("c")
    chip = 2 * xi + yi

    big_local = [w[n].astype(BF16) for n in BIG]
    packed, offs = _pack_rows(big_local, ROW_ALIGN)
    rows = packed.shape[0]
    gathered = gather_weights(packed.reshape(2, rows // 2, FLAT_W)).reshape(4, rows, FLAT_W)
    full = {n: _full_from_shards(n, g4) for n, g4 in zip(BIG, _unpack_rows(gathered, offs, [a.shape for a in big_local]))}
    cw_local = ssd_conv_w[0]
    cshard = cw_local.shape[1]
    cw_rows = -(-SSD_CONV * conv_dim // FLAT_W)
    cw_placed = lax.dynamic_update_slice(jnp.zeros((SSD_CONV, conv_dim), F32), cw_local, (0, chip * cshard))
    cw_placed = jnp.where(ci == 0, cw_placed, 0.0)
    cw_packed, _ = _pack_rows([cw_placed], 8)
    conv_w = allreduce_small(cw_packed, "gather_conv_w")[:cw_rows].reshape(-1)[:SSD_CONV * conv_dim].reshape(SSD_CONV, conv_dim)

    def ffn_weights(w_in, w_out):
        gate, up = w_in[:, :d_ff], w_in[:, d_ff:]
        return jnp.concatenate([_pad_to(gate, 1, fp), _pad_to(up, 1, fp)], axis=1), _pad_to(w_out, 0, fp)

    ffn_w = {(k, i): ffn_weights(full[f"ffn{k}_w_in"][i], full[f"ffn{k}_w_out"][i]) for k in (1, 2) for i in range(depth)}
    ab = full["ab_w_in"][0]
    s = [0, d, d + conv_dim, d + conv_dim + nh_ssd]
    ab_wts = (ab[:, s[0]:s[1]], ab[:, s[1]:s[2]], ab[:, s[3]:], _pad_to(ab[:, s[2]:s[3]], 1, LANE))
    fox = full["fox_w_in"][0]
    fox_qkv, fox_f = fox[:, :3 * d], _pad_to(fox[:, 3 * d:], 1, LANE)
    fox_bias = _lane_row(fox_b_f[0])
    dt_bias, a_log = _lane_row(ssd_dt_bias[0]), _lane_row(ssd_a_log[0])
    lb_soft = rowwise(lambda z: (lambda e: e / jnp.sum(e, axis=0, keepdims=True))(jnp.exp(z - jnp.max(z, axis=0, keepdims=True))),
                      "lb_softmax", [hgrn_lb_logits], [], [(d, F32)], tm=hgrn_lb_logits.shape[0])[0]
    lb = lb_soft[0:1]
    conv_b = ssd_conv_b

    saved = []
    for i in range(depth):
        h, s1 = ffn_fwd(h, ffn1_norm[i:i + 1], *ffn_w[(1, i)], f"l{i}_ffn1")
        if i % 2 == 0:
            h, s2 = mixer_fwd(h, mix_norm[i:i + 1], ab_wts, conv_w, conv_b, dt_bias, a_log, ssd_d[0], ssd_norm, lb, hgrn_norm,
                              full["ab_w_out"][0], f"l{i}_mix")
        else:
            h, s2 = fox_fwd(h, mix_norm[i:i + 1], fox_qkv, fox_f, fox_bias, full["fox_w_out"][0], f"l{i}_fox")
        h, s3 = ffn_fwd(h, ffn2_norm[i:i + 1], *ffn_w[(2, i)], f"l{i}_ffn2")
        h, s4 = ple_fwd(h, p[i, 0], ple_gate_norm[i:i + 1], full["ple_w_gate"][i], full["ple_w_up"][i], ple_norm[i:i + 1], f"l{i}_ple")
        saved.append((s1, s2, s3, s4))
    loss, dh, g_final = loss_head(h, loss_target[0], final_norm.reshape(1, d))

    gb = {n: [None] * w[n].shape[0] for n in BIG}
    gs = {n: [None] * w[n].shape[0] for n in SMALL}
    gs["final_norm"] = g_final[0]

    def ffn_grads(k, i, dw_in, dw_out):
        gb[f"ffn{k}_w_in"][i] = jnp.concatenate([dw_in[:, :d_ff], dw_in[:, fp:fp + d_ff]], axis=1)
        gb[f"ffn{k}_w_out"][i] = dw_out[:d_ff]

    for i in reversed(range(depth)):
        s1, s2, s3, s4 = saved[i]
        dh, dgn, dwg, dwu, dpn = ple_bwd(dh, s4, p[i, 0], ple_gate_norm[i:i + 1], full["ple_w_gate"][i], ple_norm[i:i + 1], f"l{i}_ple")
        gs["ple_gate_norm"][i], gs["ple_norm"][i] = dgn[0], dpn[0]
        gb["ple_w_gate"][i], gb["ple_w_up"][i] = dwg, dwu
        dh, dnw, dw_in, dw_out = ffn_bwd(dh, s3, ffn2_norm[i:i + 1], *ffn_w[(2, i)], f"l{i}_ffn2")
        gs["ffn2_norm"][i] = dnw[0]
        ffn_grads(2, i, dw_in, dw_out)
        if i % 2 == 0:
            dh, gm = mixer_bwd(dh, s2, mix_norm[i:i + 1], ab_wts, conv_w, conv_b, dt_bias, a_log, ssd_norm, lb, hgrn_norm,
                               full["ab_w_out"][0], f"l{i}_mix")
            gs["mix_norm"][i] = gm["mix_norm"][0]
            q4 = gm["w_qfvg"]
            gb["ab_w_in"][0] = jnp.concatenate([gm["w_z"], gm["w_xbc"], gm["w_dt"][:, :nh_ssd], q4], axis=1)
            gb["ab_w_out"][0] = gm["w_out"]
            gs["ssd_conv_w"][0], gs["ssd_conv_b"][0] = gm["conv_w"], gm["conv_b"][0]
            gs["ssd_dt_bias"][0], gs["ssd_a_log"][0] = gm["dt_bias"][0, :nh_ssd], gm["a_log"][0, :nh_ssd]
            gs["ssd_d"][0], gs["ssd_norm"][0], gs["hgrn_norm"][0] = gm["d_skip"][0], gm["ssd_norm"][0], gm["hgrn_norm"][0]
            dlb = gm["lb"]
        else:
            dh, dnw, dwqkv, dwf, dbf, dwo = fox_bwd(dh, s2, mix_norm[i:i + 1], fox_qkv, fox_f, fox_bias, full["fox_w_out"][0], f"l{i}_fox")
            gs["mix_norm"][i] = dnw[0]
            gb["fox_w_in"][0] = jnp.concatenate([dwqkv, dwf[:, :nh_fox]], axis=1)
            gb["fox_w_out"][0] = dwo
            gs["fox_b_f"][0] = dbf[0, :nh_fox]
        dh, dnw, dw_in, dw_out = ffn_bwd(dh, s1, ffn1_norm[i:i + 1], *ffn_w[(1, i)], f"l{i}_ffn1")
        gs["ffn1_norm"][i] = dnw[0]
        ffn_grads(1, i, dw_in, dw_out)
    grad_x = dh[None]
    first_row = (jnp.arange(hgrn_lb_logits.shape[0]) == 0).astype(F32)[:, None]
    gs["hgrn_lb_logits"] = rowwise(lambda sm, g, e: sm * (e - sm[0:1]) * g, "lb_softmax_bwd",
                                   [lb_soft, jnp.broadcast_to(dlb, lb_soft.shape), jnp.broadcast_to(first_row, lb_soft.shape)],
                                   [], [(d, F32)], tm=lb_soft.shape[0])[0]

    g4 = [_shards_from_full(n, jnp.stack(gb[n])) for n in BIG]
    g_packed = jnp.stack([_pack_rows([a[k] for a in g4], ROW_ALIGN)[0] for k in range(4)])
    gh = jnp.moveaxis(g_packed.reshape(4, 2, rows // 2, FLAT_W), 1, 0)
    g_red = reduce_scatter_grads(gh, ci.reshape(1).astype(jnp.int32))
    g_big = dict(zip(BIG, _unpack_rows(g_red, offs, [a.shape for a in big_local])))

    small_local = [jnp.stack(gs[n]) if isinstance(gs[n], list) else gs[n] for n in SMALL]
    small_local = [a.reshape(w[n].shape if n != "ssd_conv_w" else (1, SSD_CONV, conv_dim)) for n, a in zip(SMALL, small_local)]
    sp, soffs = _pack_rows(small_local + [loss.reshape(1)], 8)
    sr = allreduce_small(sp, "allreduce_small_grads")
    small_red = _unpack_rows(sr, soffs, [a.shape for a in small_local] + [(1,)])
    loss_total = small_red[-1][0]
    g_small = dict(zip(SMALL, small_red[:-1]))
    g_small["ssd_conv_w"] = lax.dynamic_slice(g_small["ssd_conv_w"], (0, 0, chip * cshard), (1, SSD_CONV, cshard))

    grads, delta, new_m, new_v = {}, {}, {}, {}
    for n in BIG:
        shp = w[n].shape
        two_d = (shp[0] * shp[1], shp[2])
        grads[n] = g_big[n]
        dl, m2, v2 = adamw(w[n].reshape(two_d), g_big[n].reshape(two_d), mom[n].reshape(two_d), var[n].reshape(two_d), f"adamw_{n}")
        delta[n], new_m[n], new_v[n] = dl.reshape(shp), m2.reshape(shp), v2.reshape(shp)
    packs = [_pack_rows([src[n] for n in SMALL], 8) for src in (w, g_small, mom, var)]
    dl, m2, v2 = adamw(packs[0][0], packs[1][0], packs[2][0], packs[3][0], "adamw_small")
    shapes = [w[n].shape for n in SMALL]
    for n, a, b, c_ in zip(SMALL, _unpack_rows(dl, packs[0][1], shapes), _unpack_rows(m2, packs[0][1], shapes), _unpack_rows(v2, packs[0][1], shapes)):
        grads[n], delta[n], new_m[n], new_v[n] = g_small[n], a, b, c_
    return (loss_total, grad_x, *[grads[n] for n in WEIGHTS], *[delta[n] for n in WEIGHTS],
            *[new_m[n] for n in WEIGHTS], *[new_v[n] for n in WEIGHTS])
```

```python
import functools

import jax
import jax.numpy as jnp
from jax import lax
from jax.experimental import pallas as pl
from jax.experimental.pallas import tpu as pltpu

F32 = jnp.float32
BF16 = jnp.bfloat16
HI = lax.Precision.HIGHEST

EPS = 1e-6
CHUNK = 64
SSD_HEAD_DIM = 64
SSD_GROUPS = 4
SSD_STATE = 128
SSD_CONV = 4
HGRN_KDIM = 128
FOX_HEAD_DIM = 128
LANE = 128
FF_ALIGN = 512
FLAT_W = 2048
ROW_ALIGN = 32
PACK_ALIGN = 1024

ADAM_LR = 0.001
ADAM_B1 = 0.9
ADAM_B2 = 0.999
ADAM_EPS = 1e-08
ADAM_WD = 0.01
ADAM_STEP = 10

VMEM_LIMIT = 56 * 1024 * 1024
MESH_ID = pl.DeviceIdType.MESH


def _cparams(sem):
    return pltpu.CompilerParams(dimension_semantics=sem, vmem_limit_bytes=VMEM_LIMIT)


def _pick(n, prefs):
    for t in prefs:
        if n % t == 0:
            return t
    return n


def _dot(a, b, dims, precision=None):
    return lax.dot_general(a, b, (dims, ((), ())), preferred_element_type=F32, precision=precision)


NN = ((1,), (0,))
NT = ((1,), (1,))
TN = ((0,), (0,))


def _sigmoid(x):
    return 1.0 / (1.0 + jnp.exp(-x))


def _silu(x):
    return x * _sigmoid(x)


def _dsilu(x):
    s = _sigmoid(x)
    return s * (1.0 + x * (1.0 - s))


def matmul(a, b, mode, out_dtype, name, scale=None, residual=None):
    if mode == "nn":
        (m, k), (k2, n) = a.shape, b.shape
    elif mode == "nt":
        (m, k), (n, k2) = a.shape, b.shape
    else:
        (k, m), (k2, n) = a.shape, b.shape
    assert k == k2, (a.shape, b.shape, mode)
    tm = _pick(m, (1024, 512, 256, 128))
    tn = _pick(n, (1024, 1408, 512, 256, 128))
    tk = _pick(k, (2048, 1408, 1024, 512, 256, 128))
    nk = k // tk
    dims = {"nn": NN, "nt": NT, "tn": TN}[mode]

    def body(*refs):
        if residual is None:
            a_ref, b_ref, o_ref, acc_ref = refs
            r_ref = None
        else:
            a_ref, b_ref, r_ref, o_ref, acc_ref = refs
        kk = pl.program_id(2)

        @pl.when(kk == 0)
        def _():
            acc_ref[...] = jnp.zeros_like(acc_ref)

        acc_ref[...] += _dot(a_ref[...].astype(BF16), b_ref[...].astype(BF16), dims)

        @pl.when(kk == nk - 1)
        def _():
            r = acc_ref[...]
            if scale is not None:
                r = r * scale
            if r_ref is not None:
                r = r + r_ref[...].astype(F32)
            o_ref[...] = r.astype(o_ref.dtype)

    if mode == "nn":
        a_spec = pl.BlockSpec((tm, tk), lambda i, j, kk: (i, kk))
        b_spec = pl.BlockSpec((tk, tn), lambda i, j, kk: (kk, j))
    elif mode == "nt":
        a_spec = pl.BlockSpec((tm, tk), lambda i, j, kk: (i, kk))
        b_spec = pl.BlockSpec((tn, tk), lambda i, j, kk: (j, kk))
    else:
        a_spec = pl.BlockSpec((tk, tm), lambda i, j, kk: (kk, i))
        b_spec = pl.BlockSpec((tk, tn), lambda i, j, kk: (kk, j))
    o_spec = pl.BlockSpec((tm, tn), lambda i, j, kk: (i, j))
    in_specs = [a_spec, b_spec]
    args = [a, b]
    if residual is not None:
        in_specs.append(o_spec)
        args.append(residual)
    return pl.pallas_call(
        body, name=name, grid=(m // tm, n // tn, nk),
        in_specs=in_specs, out_specs=o_spec,
        out_shape=jax.ShapeDtypeStruct((m, n), out_dtype),
        scratch_shapes=[pltpu.VMEM((tm, tn), F32)],
        compiler_params=_cparams(("parallel", "parallel", "arbitrary")),
    )(*args)


def rowwise(fn, name, rows, consts, outs, accs=(), tm=256):
    rows = [r if isinstance(r, tuple) else (r, r.shape[1], 0) for r in rows]
    t = rows[0][0].shape[0]
    tm = min(tm, t)
    assert t % tm == 0
    n_in = len(rows) + len(consts)
    n_out = len(outs)

    def body(*refs):
        res = fn(*[r[...] for r in refs[:n_in]])
        if not isinstance(res, tuple):
            res = (res,)
        for r, v in zip(refs[n_in:n_in + n_out], res[:n_out]):
            r[...] = v.astype(r.dtype)
        if accs:
            a_refs = refs[n_in + n_out:]
            first = pl.program_id(0) == 0

            @pl.when(first)
            def _():
                for r, v in zip(a_refs, res[n_out:]):
                    r[...] = v

            @pl.when(jnp.logical_not(first))
            def _():
                for r, v in zip(a_refs, res[n_out:]):
                    r[...] += v

    in_specs = [pl.BlockSpec((tm, w), functools.partial(lambda i, cb: (i, cb), cb=cb)) for _, w, cb in rows]
    in_specs += [pl.BlockSpec(c.shape, lambda i: (0, 0)) for c in consts]
    out_specs = [pl.BlockSpec((tm, w), lambda i: (i, 0)) for w, _ in outs]
    out_specs += [pl.BlockSpec(s, lambda i: (0, 0)) for s in accs]
    out_shape = [jax.ShapeDtypeStruct((t, w), d) for w, d in outs]
    out_shape += [jax.ShapeDtypeStruct(s, F32) for s in accs]
    res = pl.pallas_call(
        body, name=name, grid=(t // tm,), in_specs=in_specs, out_specs=out_specs, out_shape=out_shape,
        compiler_params=_cparams(("arbitrary",) if accs else ("parallel",)),
    )(*[r[0] for r in rows], *consts)
    return res


def _rms(x):
    return lax.rsqrt(jnp.mean(x * x, axis=-1, keepdims=True) + EPS)


def _norm_bwd(x, w, dy):
    r = _rms(x)
    xh = x * r
    g = dy * w
    dx = r * (g - xh * jnp.mean(g * xh, axis=-1, keepdims=True))
    return dx, jnp.sum(dy * xh, axis=0, keepdims=True)


def norm_fwd(h, w, name):
    return rowwise(lambda x, w_: x * _rms(x) * w_, name, [h], [w], [(h.shape[1], BF16)])[0]


def norm_bwd_res(h, w, dhn, dres, name):
    def fn(x, dy, dr, w_):
        dx, dw = _norm_bwd(x, w_, dy.astype(F32))
        return dr + dx, dw
    d = h.shape[1]
    return rowwise(fn, name, [h, dhn, dres], [w], [(d, F32)], [(1, d)])


def ffn_fwd(h, norm_w, w_in, w_out, tag):
    fp = w_out.shape[0]
    hn = norm_fwd(h, norm_w, f"{tag}_norm")
    u = matmul(hn, w_in, "nn", BF16, f"{tag}_in")
    a = rowwise(lambda g, up: _silu(g.astype(F32)) * up.astype(F32), f"{tag}_act",
                [(u, fp, 0), (u, fp, 1)], [], [(fp, BF16)], tm=128)[0]
    out = matmul(a, w_out, "nn", F32, f"{tag}_out", scale=0.5, residual=h)
    return out, (h, hn, u, a)


def ffn_bwd(dout, saved, norm_w, w_in, w_out, tag):
    h, hn, u, a = saved
    fp = w_out.shape[0]
    da = matmul(dout, w_out, "nt", BF16, f"{tag}_da", scale=0.5)

    def act_bwd(da_, g, up):
        da_, g, up = da_.astype(F32), g.astype(F32), up.astype(F32)
        return jnp.concatenate([da_ * up * _dsilu(g), da_ * _silu(g)], axis=1)

    du = rowwise(act_bwd, f"{tag}_dact", [da, (u, fp, 0), (u, fp, 1)], [], [(2 * fp, BF16)], tm=128)[0]
    dw_out = matmul(a, dout, "tn", F32, f"{tag}_dwout", scale=0.5)
    dhn = matmul(du, w_in, "nt", F32, f"{tag}_dhn")
    dw_in = matmul(hn, du, "tn", F32, f"{tag}_dwin")
    dh, dnw = norm_bwd_res(h, norm_w, dhn, dout, f"{tag}_dnorm")
    return dh, dnw, dw_in, dw_out


def ple_fwd(h, p_i, gate_norm_w, w_gate, w_up, post_norm_w, tag):
    d = h.shape[1]
    e0 = matmul(p_i, w_up, "nn", F32, f"{tag}_up")
    hn = norm_fwd(h, gate_norm_w, f"{tag}_norm")
    s = matmul(hn, w_gate, "nn", F32, f"{tag}_gate")
    out = rowwise(lambda x, e, s_, pw: x + e * _rms(e) * pw * _sigmoid(s_), f"{tag}_add",
                  [h, e0, s], [post_norm_w], [(d, F32)])[0]
    return out, (h, hn, e0, s)


def ple_bwd(dout, saved, p_i, gate_norm_w, w_gate, post_norm_w, tag):
    h, hn, e0, s = saved
    d = h.shape[1]

    def fn(dy, e, s_, pw):
        gate = _sigmoid(s_)
        emb = e * _rms(e) * pw
        de0, dpw = _norm_bwd(e, pw, dy * gate)
        return de0, dy * emb * gate * (1.0 - gate), dpw

    de0, ds, dpw = rowwise(fn, f"{tag}_dadd", [dout, e0, s], [post_norm_w], [(d, BF16), (d, BF16)], [(1, d)])
    dw_up = matmul(p_i, de0, "tn", F32, f"{tag}_dwup")
    dhn = matmul(ds, w_gate, "nt", F32, f"{tag}_dhn")
    dw_gate = matmul(hn, ds, "tn", F32, f"{tag}_dwgate")
    dh, dgn = norm_bwd_res(h, gate_norm_w, dhn, dout, f"{tag}_dnorm")
    return dh, dgn, dw_gate, dw_up, dpw


def loss_head(h, target, w):
    d = h.shape[1]

    def fn(x, tgt, w_):
        r = _rms(x)
        err = x * r * w_ - tgt
        dx, dw = _norm_bwd(x, w_, err * (1.0 / d))
        part = 0.5 * jnp.sum(jnp.sum(err * err, axis=-1, keepdims=True), axis=0, keepdims=True) * (1.0 / d)
        return dx, dw, jnp.broadcast_to(part, (1, LANE))

    dh, dw, loss = rowwise(fn, "loss_head", [h, target], [w], [(d, F32)], [(1, d), (1, LANE)])
    return loss[0, 0], dh, dw


def cumsum_rows(x, name, reverse=False):
    t, w = x.shape
    tb = min(256, t)
    nb = t // tb

    def body(x_ref, o_ref, carry):
        @pl.when(pl.program_id(0) == 0)
        def _():
            carry[...] = jnp.zeros_like(carry)

        r = lax.broadcasted_iota(jnp.int32, (tb, tb), 0)
        c = lax.broadcasted_iota(jnp.int32, (tb, tb), 1)
        tri = ((c >= r) if reverse else (c <= r)).astype(F32)
        y = _dot(tri, x_ref[...], NN, HI) + carry[...]
        o_ref[...] = y
        carry[...] = y[0:1, :] if reverse else y[tb - 1:tb, :]

    idx = (lambda i: (nb - 1 - i, 0)) if reverse else (lambda i: (i, 0))
    return pl.pallas_call(
        body, name=name, grid=(nb,), in_specs=[pl.BlockSpec((tb, w), idx)], out_specs=pl.BlockSpec((tb, w), idx),
        out_shape=jax.ShapeDtypeStruct((t, w), F32), scratch_shapes=[pltpu.VMEM((1, w), F32)],
        compiler_params=_cparams(("arbitrary",)),
    )(x)


def _fox_tiles(t):
    return _pick(t, (1024, 512, 256, 128)), _pick(t, (512, 256, 128))


def _fox_pairs(t, resident_is_query):
    tr, ts = _fox_tiles(t)
    rows = []
    for ri in range(t // tr):
        if resident_is_query:
            sis = list(range((ri * tr + tr - 1) // ts + 1))
        else:
            sis = list(range((ri * tr) // ts, t // ts))
        for si in sis:
            q0, k0 = (ri * tr, si * ts) if resident_is_query else (si * ts, ri * tr)
            qn, kn = (tr, ts) if resident_is_query else (ts, tr)
            rows.append((ri, si, si == sis[0], si == sis[-1], q0 < k0 + kn - 1))
    return tuple(jnp.asarray([r[j] for r in rows], jnp.int32) for j in range(5))


def _causal(s, row0, col0, transposed=False):
    r = row0 + lax.broadcasted_iota(jnp.int32, s.shape, 0)
    c = col0 + lax.broadcasted_iota(jnp.int32, s.shape, 1)
    return jnp.where((c >= r) if transposed else (r >= c), s, -jnp.inf)


def _on_diagonal(flag, step):
    @pl.when(flag == 1)
    def _():
        step(True)

    @pl.when(flag == 0)
    def _():
        step(False)


def fox_attn_fwd(qkv, dcol, drow, name):
    t = qkv.shape[0]
    nh = dcol.shape[0]
    tq, tk = _fox_tiles(t)
    pairs = _fox_pairs(t, True)
    scale = FOX_HEAD_DIM ** -0.5

    def body(qs, ks, fs, ls, dg, q_ref, k_ref, v_ref, dc_ref, dr_ref, o_ref, lse_ref, m_s, l_s, acc_s):
        p = pl.program_id(1)

        @pl.when(fs[p] == 1)
        def _():
            m_s[...] = jnp.full_like(m_s, -jnp.inf)
            l_s[...] = jnp.zeros_like(l_s)
            acc_s[...] = jnp.zeros_like(acc_s)

        def step(masked):
            s = _dot(q_ref[...], k_ref[...], NT) * scale + dc_ref[0] - dr_ref[0]
            if masked:
                s = _causal(s, qs[p] * tq, ks[p] * tk)
            m_new = jnp.maximum(m_s[...], jnp.max(s, axis=1, keepdims=True))
            alpha = jnp.exp(m_s[...] - m_new)
            pr = jnp.exp(s - m_new)
            l_s[...] = alpha * l_s[...] + jnp.sum(pr, axis=1, keepdims=True)
            acc_s[...] = alpha * acc_s[...] + _dot(pr.astype(BF16), v_ref[...], NN)
            m_s[...] = m_new

        _on_diagonal(dg[p], step)

        @pl.when(ls[p] == 1)
        def _():
            o_ref[...] = (acc_s[...] / l_s[...]).astype(o_ref.dtype)
            lse_ref[0] = m_s[...] + jnp.log(l_s[...])

    hd = FOX_HEAD_DIM
    qcol = pl.BlockSpec((1, tq, 1), lambda h, p, qs, ks, *_: (h, qs[p], 0))
    grid_spec = pltpu.PrefetchScalarGridSpec(
        num_scalar_prefetch=5, grid=(nh, pairs[0].shape[0]),
        in_specs=[
            pl.BlockSpec((tq, hd), lambda h, p, qs, ks, *_: (qs[p], h)),
            pl.BlockSpec((tk, hd), lambda h, p, qs, ks, *_: (ks[p], nh + h)),
            pl.BlockSpec((tk, hd), lambda h, p, qs, ks, *_: (ks[p], 2 * nh + h)),
            qcol,
            pl.BlockSpec((1, 1, tk), lambda h, p, qs, ks, *_: (h, 0, ks[p])),
        ],
        out_specs=[pl.BlockSpec((tq, hd), lambda h, p, qs, ks, *_: (qs[p], h)), qcol],
        scratch_shapes=[pltpu.VMEM((tq, 1), F32), pltpu.VMEM((tq, 1), F32), pltpu.VMEM((tq, hd), F32)])
    return pl.pallas_call(
        body, name=name, grid_spec=grid_spec,
        out_shape=[jax.ShapeDtypeStruct((t, nh * hd), BF16), jax.ShapeDtypeStruct((nh, t, 1), F32)],
        compiler_params=_cparams(("parallel", "arbitrary")),
    )(*pairs, qkv, qkv, qkv, dcol, drow)


def fox_attn_dq(qkv, do, dcol, drow, lse_col, delta_col, name):
    t = qkv.shape[0]
    nh = dcol.shape[0]
    tq, tk = _fox_tiles(t)
    pairs = _fox_pairs(t, True)
    scale = FOX_HEAD_DIM ** -0.5

    def body(qs, ks, fs, ls, dg, q_ref, k_ref, v_ref, do_ref, dc_ref, dr_ref, lse_ref, dl_ref, dq_ref, db_ref, acc_s, db_s):
        p = pl.program_id(1)

        @pl.when(fs[p] == 1)
        def _():
            acc_s[...] = jnp.zeros_like(acc_s)
            db_s[...] = jnp.zeros_like(db_s)

        def step(masked):
            s = _dot(q_ref[...], k_ref[...], NT) * scale + dc_ref[0] - dr_ref[0]
            if masked:
                s = _causal(s, qs[p] * tq, ks[p] * tk)
            pr = jnp.exp(s - lse_ref[0])
            dp = _dot(do_ref[...], v_ref[...], NT)
            ds = pr * (dp - dl_ref[0])
            acc_s[...] += _dot(ds.astype(BF16), k_ref[...], NN)
            db_s[...] += jnp.sum(ds, axis=1, keepdims=True)

        _on_diagonal(dg[p], step)

        @pl.when(ls[p] == 1)
        def _():
            dq_ref[...] = (acc_s[...] * scale).astype(dq_ref.dtype)
            db_ref[0] = db_s[...]

    hd = FOX_HEAD_DIM
    qblk = pl.BlockSpec((tq, hd), lambda h, p, qs, ks, *_: (qs[p], h))
    qcol = pl.BlockSpec((1, tq, 1), lambda h, p, qs, ks, *_: (h, qs[p], 0))
    grid_spec = pltpu.PrefetchScalarGridSpec(
        num_scalar_prefetch=5, grid=(nh, pairs[0].shape[0]),
        in_specs=[
            qblk,
            pl.BlockSpec((tk, hd), lambda h, p, qs, ks, *_: (ks[p], nh + h)),
            pl.BlockSpec((tk, hd), lambda h, p, qs, ks, *_: (ks[p], 2 * nh + h)),
            qblk, qcol,
            pl.BlockSpec((1, 1, tk), lambda h, p, qs, ks, *_: (h, 0, ks[p])),
            qcol, qcol,
        ],
        out_specs=[qblk, qcol],
        scratch_shapes=[pltpu.VMEM((tq, hd), F32), pltpu.VMEM((tq, 1), F32)])
    return pl.pallas_call(
        body, name=name, grid_spec=grid_spec,
        out_shape=[jax.ShapeDtypeStruct((t, nh * hd), BF16), jax.ShapeDtypeStruct((nh, t, 1), F32)],
        compiler_params=_cparams(("parallel", "arbitrary")),
    )(*pairs, qkv, qkv, qkv, do, dcol, drow, lse_col, delta_col)


def fox_attn_dkv(qkv, do, dcol, drow, lse_row, delta_row, name):
    t = qkv.shape[0]
    nh = dcol.shape[0]
    tk, tq = _fox_tiles(t)
    pairs = _fox_pairs(t, False)
    scale = FOX_HEAD_DIM ** -0.5

    def body(ks, qs, fs, ls, dg, q_ref, k_ref, v_ref, do_ref, dc_ref, dr_ref, lse_ref, dl_ref, dk_ref, dv_ref, db_ref, dk_s, dv_s, db_s):
        p = pl.program_id(1)

        @pl.when(fs[p] == 1)
        def _():
            dk_s[...] = jnp.zeros_like(dk_s)
            dv_s[...] = jnp.zeros_like(dv_s)
            db_s[...] = jnp.zeros_like(db_s)

        def step(masked):
            st = _dot(k_ref[...], q_ref[...], NT) * scale + dr_ref[0] - dc_ref[0]
            if masked:
                st = _causal(st, ks[p] * tk, qs[p] * tq, transposed=True)
            pt = jnp.exp(st - lse_ref[0])
            dv_s[...] += _dot(pt.astype(BF16), do_ref[...], NN)
            dpt = _dot(v_ref[...], do_ref[...], NT)
            dst = pt * (dpt - dl_ref[0])
            dk_s[...] += _dot(dst.astype(BF16), q_ref[...], NN)
            db_s[...] -= jnp.sum(dst, axis=1, keepdims=True)

        _on_diagonal(dg[p], step)

        @pl.when(ls[p] == 1)
        def _():
            dk_ref[...] = (dk_s[...] * scale).astype(dk_ref.dtype)
            dv_ref[...] = dv_s[...].astype(dv_ref.dtype)
            db_ref[0] = db_s[...]

    hd = FOX_HEAD_DIM
    qblk = pl.BlockSpec((tq, hd), lambda h, p, ks, qs, *_: (qs[p], h))
    qrow = pl.BlockSpec((1, 1, tq), lambda h, p, ks, qs, *_: (h, 0, qs[p]))
    kcol = pl.BlockSpec((1, tk, 1), lambda h, p, ks, qs, *_: (h, ks[p], 0))
    kv_out = pl.BlockSpec((tk, hd), lambda h, p, ks, qs, *_: (ks[p], h))
    grid_spec = pltpu.PrefetchScalarGridSpec(
        num_scalar_prefetch=5, grid=(nh, pairs[0].shape[0]),
        in_specs=[
            qblk,
            pl.BlockSpec((tk, hd), lambda h, p, ks, qs, *_: (ks[p], nh + h)),
            pl.BlockSpec((tk, hd), lambda h, p, ks, qs, *_: (ks[p], 2 * nh + h)),
            qblk, kcol, qrow, qrow, qrow,
        ],
        out_specs=[kv_out, kv_out, kcol],
        scratch_shapes=[pltpu.VMEM((tk, hd), F32), pltpu.VMEM((tk, hd), F32), pltpu.VMEM((tk, 1), F32)])
    return pl.pallas_call(
        body, name=name, grid_spec=grid_spec,
        out_shape=[jax.ShapeDtypeStruct((t, nh * hd), BF16), jax.ShapeDtypeStruct((t, nh * hd), BF16),
                   jax.ShapeDtypeStruct((nh, t, 1), F32)],
        compiler_params=_cparams(("parallel", "arbitrary")),
    )(*pairs, qkv, qkv, qkv, do, dcol, drow, lse_row, delta_row)


def _head_selector(d, hd):
    return (jnp.arange(d)[:, None] // hd == jnp.arange(LANE)[None, :]).astype(F32)


def _log_sigmoid(x):
    return jnp.minimum(x, 0.0) - jnp.log1p(jnp.exp(-jnp.abs(x)))


def fox_fwd(h, norm_w, w_qkv, w_f, b_f, w_out, tag):
    t, d = h.shape
    nh = d // FOX_HEAD_DIM
    hn = norm_fwd(h, norm_w, f"{tag}_norm")
    qkv = matmul(hn, w_qkv, "nn", BF16, f"{tag}_qkv")
    fr = matmul(hn, w_f, "nn", F32, f"{tag}_f")
    logf = rowwise(lambda x, b: _log_sigmoid(x + b), f"{tag}_logf", [fr], [b_f], [(LANE, F32)])[0]
    dcum = cumsum_rows(logf, f"{tag}_cum").T[:nh]
    dcol, drow = dcum[:, :, None], dcum[:, None, :]
    o, lse = fox_attn_fwd(qkv, dcol, drow, f"{tag}_attn")
    out = matmul(o, w_out, "nn", F32, f"{tag}_out", residual=h)
    return out, (h, hn, qkv, fr, dcol, drow, o, lse)


def fox_bwd(dout, saved, norm_w, w_qkv, w_f, b_f, w_out, tag):
    h, hn, qkv, fr, dcol, drow, o, lse = saved
    t, d = h.shape
    nh = d // FOX_HEAD_DIM
    do = matmul(dout, w_out, "nt", BF16, f"{tag}_do")
    dw_out = matmul(o, dout, "tn", F32, f"{tag}_dwout")
    sel = _head_selector(d, FOX_HEAD_DIM)
    delta = rowwise(lambda a, b, s: _dot(a.astype(F32) * b.astype(F32), s, NN, HI), f"{tag}_delta",
                    [do, o], [sel], [(LANE, F32)])[0].T[:nh]
    dq, dbias_q = fox_attn_dq(qkv, do, dcol, drow, lse, delta[:, :, None], f"{tag}_dq")
    dk, dv, dbias_k = fox_attn_dkv(qkv, do, dcol, drow, jnp.swapaxes(lse, 1, 2), delta[:, None, :], f"{tag}_dkv")
    dlogf_q = cumsum_rows(jnp.pad(dbias_q[:, :, 0].T, ((0, 0), (0, LANE - nh))), f"{tag}_dcum_q", reverse=True)
    dlogf_k = cumsum_rows(jnp.pad(dbias_k[:, :, 0].T, ((0, 0), (0, LANE - nh))), f"{tag}_dcum_k", reverse=True)

    def dlogf_fn(gq, gk, x, b):
        r = (gq + gk) * _sigmoid(-(x + b))
        return r, jnp.sum(r, axis=0, keepdims=True)

    dfr, db_f = rowwise(dlogf_fn, f"{tag}_dlogf", [dlogf_q, dlogf_k, fr], [b_f], [(LANE, BF16)], [(1, LANE)])
    dqkv = jnp.concatenate([dq, dk, dv], axis=1)
    dhn = matmul(dfr, w_f, "nt", F32, f"{tag}_dhn_f")
    dhn = matmul(dqkv, w_qkv, "nt", F32, f"{tag}_dhn", residual=dhn)
    dw_qkv = matmul(hn, dqkv, "tn", F32, f"{tag}_dwqkv")
    dw_f = matmul(hn, dfr, "tn", F32, f"{tag}_dwf")
    dh, dnw = norm_bwd_res(h, norm_w, dhn, dout, f"{tag}_dnorm")
    return dh, dnw, dw_qkv, dw_f, db_f, dw_out


CONV_ROWS = 256


def _shift_rows(cur, halo, shift, up=False):
    if shift == 0:
        return cur
    n = cur.shape[0]
    row = lax.broadcasted_iota(jnp.int32, cur.shape, 0)
    if up:
        return jnp.where(row >= n - shift, pltpu.roll(halo, n - shift, 0), pltpu.roll(cur, n - shift, 0))
    return jnp.where(row < shift, pltpu.roll(halo, shift, 0), pltpu.roll(cur, shift, 0))


def _conv_pre(x, halo, w, b):
    acc = b + w[SSD_CONV - 1:SSD_CONV] * x
    for k in range(SSD_CONV - 1):
        acc = acc + w[k:k + 1] * _shift_rows(x, halo, SSD_CONV - 1 - k)
    return acc


def conv_fwd(x, w, b, name):
    t, cw = x.shape
    tb = min(CONV_ROWS, t)

    def body(x_ref, w_ref, b_ref, o_ref, halo):
        @pl.when(pl.program_id(0) == 0)
        def _():
            halo[...] = jnp.zeros_like(halo)

        xv = x_ref[...]
        o_ref[...] = _silu(_conv_pre(xv, halo[...], w_ref[...], b_ref[...]))
        halo[...] = xv

    blk = pl.BlockSpec((tb, cw), lambda i: (i, 0))
    return pl.pallas_call(
        body, name=name, grid=(t // tb,),
        in_specs=[blk, pl.BlockSpec(w.shape, lambda i: (0, 0)), pl.BlockSpec(b.shape, lambda i: (0, 0))],
        out_specs=blk, out_shape=jax.ShapeDtypeStruct((t, cw), F32), scratch_shapes=[pltpu.VMEM((tb, cw), F32)],
        compiler_params=_cparams(("arbitrary",)),
    )(x, w, b)


def conv_bwd(x, w, b, dact, name):
    t, cw = x.shape
    tb = min(CONV_ROWS, t)
    nb = t // tb

    def body_pre(x_ref, w_ref, b_ref, da_ref, dpre_ref, dw_ref, db_ref, halo):
        first = pl.program_id(0) == 0

        @pl.when(first)
        def _():
            halo[...] = jnp.zeros_like(halo)

        xv, hv = x_ref[...], halo[...]
        dpre = da_ref[...] * _dsilu(_conv_pre(xv, hv, w_ref[...], b_ref[...]))
        dpre_ref[...] = dpre
        dw = jnp.concatenate([jnp.sum(dpre * _shift_rows(xv, hv, SSD_CONV - 1 - k), axis=0, keepdims=True)
                              for k in range(SSD_CONV)], axis=0)
        db = jnp.sum(dpre, axis=0, keepdims=True)

        @pl.when(first)
        def _():
            dw_ref[...] = dw
            db_ref[...] = db

        @pl.when(jnp.logical_not(first))
        def _():
            dw_ref[...] += dw
            db_ref[...] += db

        halo[...] = xv

    blk = pl.BlockSpec((tb, cw), lambda i: (i, 0))
    wspec = pl.BlockSpec(w.shape, lambda i: (0, 0))
    bspec = pl.BlockSpec(b.shape, lambda i: (0, 0))
    dpre, dw, db = pl.pallas_call(
        body_pre, name=f"{name}_pre", grid=(nb,), in_specs=[blk, wspec, bspec, blk], out_specs=[blk, wspec, bspec],
        out_shape=[jax.ShapeDtypeStruct((t, cw), F32), jax.ShapeDtypeStruct(w.shape, F32), jax.ShapeDtypeStruct(b.shape, F32)],
        scratch_shapes=[pltpu.VMEM((tb, cw), F32)], compiler_params=_cparams(("arbitrary",)),
    )(x, w, b, dact)

    def body_dx(dp_ref, w_ref, dx_ref, halo):
        @pl.when(pl.program_id(0) == 0)
        def _():
            halo[...] = jnp.zeros_like(halo)

        dp, wv = dp_ref[...], w_ref[...]
        acc = wv[SSD_CONV - 1:SSD_CONV] * dp
        for k in range(SSD_CONV - 1):
            acc = acc + wv[k:k + 1] * _shift_rows(dp, halo[...], SSD_CONV - 1 - k, up=True)
        dx_ref[...] = acc.astype(dx_ref.dtype)
        halo[...] = dp

    rblk = pl.BlockSpec((tb, cw), lambda i: (nb - 1 - i, 0))
    dx = pl.pallas_call(
        body_dx, name=f"{name}_dx", grid=(nb,), in_specs=[rblk, wspec], out_specs=rblk,
        out_shape=jax.ShapeDtypeStruct((t, cw), BF16), scratch_shapes=[pltpu.VMEM((tb, cw), F32)],
        compiler_params=_cparams(("arbitrary",)),
    )(dpre, w)
    return dx, dw, db


def _tri(n, upper=False):
    r = lax.broadcasted_iota(jnp.int32, (n, n), 0)
    c = lax.broadcasted_iota(jnp.int32, (n, n), 1)
    return (c >= r) if upper else (r >= c)


def _ssd_decay(dtc, dtr, a):
    low = _tri(CHUNK)[None]
    cumc = jnp.sum(jnp.where(low, dtr * a, 0.0), axis=2, keepdims=True)
    cumr = jnp.sum(jnp.where(_tri(CHUNK, upper=True)[None], dtc * a, 0.0), axis=1, keepdims=True)
    return cumc, cumr


def _bdot(a, b, nt=False):
    dims = (((2,), (2,)), ((0,), (0,))) if nt else (((2,), (1,)), ((0,), (0,)))
    return lax.dot_general(a.astype(BF16), b.astype(BF16), dims, preferred_element_type=F32)


def _ssd_specs(d, hpg):
    l, n, p = CHUNK, SSD_STATE, SSD_HEAD_DIM
    ng = d // LANE
    x3 = pl.BlockSpec((hpg, l, p), lambda g, c: (g, c, 0))
    bsp = pl.BlockSpec((l, n), lambda g, c: (c, ng + g))
    csp = pl.BlockSpec((l, n), lambda g, c: (c, ng + SSD_GROUPS + g))
    dtc = pl.BlockSpec((hpg, l, 1), lambda g, c: (g, c, 0))
    dtr = pl.BlockSpec((hpg, 1, 1, l), lambda g, c: (g, c, 0, 0))
    per_head = pl.BlockSpec((hpg, 1, 1), lambda g, c: (g, 0, 0))
    return x3, bsp, csp, dtc, dtr, per_head


def ssd_intra_fwd(x3, xbc, dtc, dtr, a_log, d_skip, name):
    nh, t, p = x3.shape
    hpg = nh // SSD_GROUPS
    d = nh * p

    def body(x_ref, b_ref, c_ref, dtc_ref, dtr_ref, al_ref, ds_ref, y_ref):
        a = -jnp.exp(al_ref[...])
        cumc, cumr = _ssd_decay(dtc_ref[...], dtr_ref[:, 0], a)
        mdec = jnp.exp(jnp.where(_tri(CHUNK)[None], cumc - cumr, -jnp.inf))
        g = _dot(c_ref[...].astype(BF16), b_ref[...].astype(BF16), NT)
        xv = x_ref[...]
        y_ref[...] = _bdot(g[None] * mdec, xv * dtc_ref[...]) + xv * ds_ref[...]

    x3s, bsp, csp, dtcs, dtrs, ph = _ssd_specs(d, hpg)
    return pl.pallas_call(
        body, name=name, grid=(SSD_GROUPS, t // CHUNK), in_specs=[x3s, bsp, csp, dtcs, dtrs, ph, ph], out_specs=x3s,
        out_shape=jax.ShapeDtypeStruct((nh, t, p), F32), compiler_params=_cparams(("parallel", "parallel")),
    )(x3, xbc, xbc, dtc, dtr, a_log, d_skip)


def ssd_intra_bwd(x3, xbc, dtc, dtr, a_log, d_skip, dy3, name):
    nh, t, p = x3.shape
    hpg = nh // SSD_GROUPS
    d = nh * p
    l, n = CHUNK, SSD_STATE

    def body(x_ref, b_ref, c_ref, dtc_ref, dtr_ref, al_ref, ds_ref, dy_ref,
             dx_ref, ddt_ref, db_ref, dc_ref, dal_ref, dds_ref):
        first = pl.program_id(1) == 0
        a = -jnp.exp(al_ref[...])
        dtc_v = dtc_ref[...]
        cumc, cumr = _ssd_decay(dtc_v, dtr_ref[:, 0], a)
        low = _tri(l)[None]
        mdec = jnp.exp(jnp.where(low, cumc - cumr, -jnp.inf))
        up = _tri(l, upper=True)[None]
        mdec_t = jnp.exp(jnp.where(up, cumr - cumc, -jnp.inf))
        bv, cv = b_ref[...].astype(BF16), c_ref[...].astype(BF16)
        g = _dot(cv, bv, NT)
        g_t = _dot(bv, cv, NT)
        xv, dy = x_ref[...], dy_ref[...]
        xd = xv * dtc_v
        dw = _bdot(dy, xd, nt=True)
        dw_t = _bdot(xd, dy, nt=True)
        dxd = _bdot(g_t[None] * mdec_t, dy)
        dx_ref[...] = dy * ds_ref[...] + dxd * dtc_v
        dg = jnp.sum(dw * mdec, axis=0)
        dg_t = jnp.sum(dw_t * mdec_t, axis=0)
        dc_ref[0] = _dot(dg.astype(BF16), bv, NN)
        db_ref[0] = _dot(dg_t.astype(BF16), cv, NN)
        e = dw * mdec * g[None]
        e_t = dw_t * mdec_t * g_t[None]
        dcum_r = jnp.sum(e_t, axis=1, keepdims=True) - jnp.sum(e, axis=1, keepdims=True)
        dda = jnp.sum(jnp.where(up, dcum_r, 0.0), axis=2, keepdims=True)
        ddt_ref[...] = jnp.sum(dxd * xv, axis=2, keepdims=True) + dda * a
        dal = jnp.sum(dda * dtc_v, axis=1, keepdims=True) * a
        dds = jnp.sum(jnp.sum(dy * xv, axis=2, keepdims=True), axis=1, keepdims=True)

        @pl.when(first)
        def _():
            dal_ref[...] = dal
            dds_ref[...] = dds

        @pl.when(jnp.logical_not(first))
        def _():
            dal_ref[...] += dal
            dds_ref[...] += dds

    x3s, bsp, csp, dtcs, dtrs, ph = _ssd_specs(d, hpg)
    grp = pl.BlockSpec((1, l, n), lambda g, c: (g, c, 0))
    return pl.pallas_call(
        body, name=name, grid=(SSD_GROUPS, t // l),
        in_specs=[x3s, bsp, csp, dtcs, dtrs, ph, ph, x3s], out_specs=[x3s, dtcs, grp, grp, ph, ph],
        out_shape=[jax.ShapeDtypeStruct((nh, t, p), F32), jax.ShapeDtypeStruct((nh, t, 1), F32),
                   jax.ShapeDtypeStruct((SSD_GROUPS, t, n), F32), jax.ShapeDtypeStruct((SSD_GROUPS, t, n), F32),
                   jax.ShapeDtypeStruct((nh, 1, 1), F32), jax.ShapeDtypeStruct((nh, 1, 1), F32)],
        compiler_params=_cparams(("parallel", "arbitrary")),
    )(x3, xbc, xbc, dtc, dtr, a_log, d_skip, dy3)


def _ssd_state_common(x, dt, a_lane):
    l = CHUNK
    da = dt * a_lane
    cum = _dot(_tri(l).astype(F32), da, NN, HI)
    cend = cum[l - 1:l]
    ec = jnp.exp(cum)
    te = jnp.exp(cend - cum)
    cd_col = jnp.exp(_dot(da, jnp.ones((l, SSD_STATE), F32), TN, HI))
    return cum, cend, ec, te, cd_col


def ssd_state_fwd(xbc, dt_lane, a_lane, name):
    t = xbc.shape[0]
    d = dt_lane.shape[1]
    gw = d // SSD_GROUPS
    l, n = CHUNK, SSD_STATE
    nc = t // l
    ng = d // LANE

    def body(x_ref, b_ref, c_ref, dt_ref, a_ref, y_ref, sp_ref, s_s):
        @pl.when(pl.program_id(1) == 0)
        def _():
            s_s[...] = jnp.zeros_like(s_s)

        s_prev = s_s[...]
        for j in range(cb):
            rows = slice(j * l, (j + 1) * l)
            xv, dt = x_ref[rows, :], dt_ref[rows, :]
            cum, cend, ec, te, cd_col = _ssd_state_common(xv, dt, a_ref[...])
            sp_ref[j] = s_prev.astype(BF16)
            y_ref[rows, :] = _dot(c_ref[rows, :].astype(BF16), s_prev.astype(BF16), NT) * ec
            xt = (xv * dt * te).astype(BF16)
            s_prev = s_prev * cd_col + _dot(xt, b_ref[rows, :].astype(BF16), TN)
        s_s[...] = s_prev

    cb = _pick(nc, SCAN_CHUNKS)
    xs = pl.BlockSpec((cb * l, gw), lambda g, c: (c, g))
    return pl.pallas_call(
        body, name=name, grid=(SSD_GROUPS, nc // cb),
        in_specs=[xs, pl.BlockSpec((cb * l, n), lambda g, c: (c, ng + g)),
                  pl.BlockSpec((cb * l, n), lambda g, c: (c, ng + SSD_GROUPS + g)),
                  xs, pl.BlockSpec((1, gw), lambda g, c: (0, g))],
        out_specs=[xs, pl.BlockSpec((cb, gw, n), lambda g, c: (c, g, 0))],
        out_shape=[jax.ShapeDtypeStruct((t, d), F32), jax.ShapeDtypeStruct((nc, d, n), BF16)],
        scratch_shapes=[pltpu.VMEM((gw, n), F32)],
        compiler_params=_cparams(("parallel", "arbitrary")),
    )(xbc, xbc, xbc, dt_lane, a_lane)


def ssd_state_bwd(xbc, dt_lane, a_lane, s_prev_all, dy, name):
    t = xbc.shape[0]
    d = dt_lane.shape[1]
    gw = d // SSD_GROUPS
    l, n = CHUNK, SSD_STATE
    nc = t // l
    ng = d // LANE

    def body(x_ref, b_ref, c_ref, dt_ref, a_ref, sp_ref, dy_ref, dx_ref, ddt_ref, db_ref, dc_ref, da_ref, ds_s):
        first = pl.program_id(1) == 0

        @pl.when(first)
        def _():
            ds_s[...] = jnp.zeros_like(ds_s)

        a_lane_v = a_ref[...]
        ds_next = ds_s[...]
        da = jnp.zeros_like(a_lane_v)
        last = lax.broadcasted_iota(jnp.int32, (l, gw), 0) == l - 1
        for j in reversed(range(cb)):
            rows = slice(j * l, (j + 1) * l)
            xv, dt = x_ref[rows, :], dt_ref[rows, :]
            cum, cend, ec, te, cd_col = _ssd_state_common(xv, dt, a_lane_v)
            bv, cv = b_ref[rows, :].astype(BF16), c_ref[rows, :].astype(BF16)
            s_prev = sp_ref[j]
            dyv = dy_ref[rows, :]
            z = _dot(cv, s_prev, NT)
            dz = (dyv * ec).astype(BF16)
            dc_ref[rows, :] = _dot(dz, s_prev, NN)
            xd = xv * dt
            dxt = _dot(bv, ds_next.astype(BF16), NT)
            db_ref[rows, :] = _dot((xd * te).astype(BF16), ds_next.astype(BF16), NN)
            dcd = _dot(jnp.ones((8, n), F32), ds_next * s_prev.astype(F32), NT, HI)[0:1]
            dte_te = dxt * xd * te
            dcum = dyv * z * ec - dte_te + jnp.where(last, jnp.sum(dte_te, axis=0, keepdims=True) + dcd * jnp.exp(cend), 0.0)
            dda = _dot(_tri(l, upper=True).astype(F32), dcum, NN, HI)
            dxd = dxt * te
            dx_ref[rows, :] = dxd * dt
            ddt_ref[rows, :] = dxd * xv + dda * a_lane_v
            da = da + jnp.sum(dda * dt, axis=0, keepdims=True)
            ds_next = ds_next * cd_col + _dot(dz, cv, TN)
        ds_s[...] = ds_next

        @pl.when(first)
        def _():
            da_ref[...] = da

        @pl.when(jnp.logical_not(first))
        def _():
            da_ref[...] += da

    cb = _pick(nc, SCAN_CHUNKS)
    nb = nc // cb
    rc = lambda c: nb - 1 - c
    xs = pl.BlockSpec((cb * l, gw), lambda g, c: (rc(c), g))
    gs = pl.BlockSpec((cb * l, n), lambda g, c: (rc(c), g))
    return pl.pallas_call(
        body, name=name, grid=(SSD_GROUPS, nb),
        in_specs=[xs, pl.BlockSpec((cb * l, n), lambda g, c: (rc(c), ng + g)),
                  pl.BlockSpec((cb * l, n), lambda g, c: (rc(c), ng + SSD_GROUPS + g)),
                  xs, pl.BlockSpec((1, gw), lambda g, c: (0, g)),
                  pl.BlockSpec((cb, gw, n), lambda g, c: (rc(c), g, 0)), xs],
        out_specs=[xs, xs, gs, gs, pl.BlockSpec((1, gw), lambda g, c: (0, g))],
        out_shape=[jax.ShapeDtypeStruct((t, d), F32), jax.ShapeDtypeStruct((t, d), F32),
                   jax.ShapeDtypeStruct((t, SSD_GROUPS * n), F32), jax.ShapeDtypeStruct((t, SSD_GROUPS * n), F32),
                   jax.ShapeDtypeStruct((1, d), F32)],
        scratch_shapes=[pltpu.VMEM((gw, n), F32)],
        compiler_params=_cparams(("parallel", "arbitrary")),
    )(xbc, xbc, xbc, dt_lane, a_lane, s_prev_all, dy)


SCAN_CHUNKS = (4, 2, 1)


def _hgrn_common(q, fr, lb):
    l = CHUNK
    sig = _sigmoid(fr)
    f = lb + (1.0 - lb) * sig
    kk = 1.0 - f
    lf = jnp.log(f)
    cum = _dot(_tri(l).astype(F32), lf, NN, HI)
    mid = cum[l // 2 - 1:l // 2]
    cend = cum[l - 1:l]
    qf = _silu(q)
    eq, ek, ee, ec = jnp.exp(cum - mid), jnp.exp(mid - cum), jnp.exp(cend - cum), jnp.exp(cum)
    cd_col = jnp.exp(_dot(lf, jnp.ones((l, HGRN_KDIM), F32), TN, HI))
    return sig, f, kk, qf, eq, ek, ee, ec, cend, cd_col


def hgrn_fwd(qfvg, lb, name):
    t = qfvg.shape[0]
    d = lb.shape[1]
    l, kd = CHUNK, HGRN_KDIM
    nh = d // kd
    nc = t // l

    cb = _pick(nc, SCAN_CHUNKS)

    def body(q_ref, f_ref, v_ref, lb_ref, o_ref, sp_ref, s_s):
        @pl.when(pl.program_id(1) == 0)
        def _():
            s_s[...] = jnp.zeros_like(s_s)

        s_prev = s_s[...]
        for j in range(cb):
            rows = slice(j * l, (j + 1) * l)
            sig, f, kk, qf, eq, ek, ee, ec, cend, cd_col = _hgrn_common(q_ref[rows, :], f_ref[rows, :], lb_ref[...])
            v = v_ref[rows, :].astype(BF16)
            sp_ref[j] = s_prev.astype(BF16)
            att = jnp.where(_tri(l), _dot((qf * eq).astype(BF16), (kk * ek).astype(BF16), NT), 0.0)
            o_ref[rows, :] = _dot(att.astype(BF16), v, NN) + _dot((qf * ec).astype(BF16), s_prev.astype(BF16), NN)
            s_prev = s_prev * cd_col + _dot((kk * ee).astype(BF16), v, TN)
        s_s[...] = s_prev

    def col(j):
        return pl.BlockSpec((cb * l, kd), lambda h, c: (c, j * nh + h))

    return pl.pallas_call(
        body, name=name, grid=(nh, nc // cb),
        in_specs=[col(0), col(1), col(2), pl.BlockSpec((1, kd), lambda h, c: (0, h))],
        out_specs=[col(0), pl.BlockSpec((cb, kd, kd), lambda h, c: (c, h, 0))],
        out_shape=[jax.ShapeDtypeStruct((t, d), F32), jax.ShapeDtypeStruct((nc, d, kd), BF16)],
        scratch_shapes=[pltpu.VMEM((kd, kd), F32)],
        compiler_params=_cparams(("parallel", "arbitrary")),
    )(qfvg, qfvg, qfvg, lb)


def hgrn_bwd(qfvg, lb, s_prev_all, do, name):
    t = qfvg.shape[0]
    d = lb.shape[1]
    l, kd = CHUNK, HGRN_KDIM
    nh = d // kd
    nc = t // l
    cb = _pick(nc, SCAN_CHUNKS)
    nb = nc // cb

    def body(q_ref, f_ref, v_ref, lb_ref, sp_ref, do_ref, dq_ref, df_ref, dv_ref, dlb_ref, ds_s):
        first = pl.program_id(1) == 0

        @pl.when(first)
        def _():
            ds_s[...] = jnp.zeros_like(ds_s)

        lbv = lb_ref[...]
        ds_next = ds_s[...]
        dlb = jnp.zeros_like(lbv)
        low = _tri(l)
        row = lax.broadcasted_iota(jnp.int32, (l, kd), 0)
        for j in reversed(range(cb)):
            rows = slice(j * l, (j + 1) * l)
            q = q_ref[rows, :]
            sig, f, kk, qf, eq, ek, ee, ec, cend, cd_col = _hgrn_common(q, f_ref[rows, :], lbv)
            v = v_ref[rows, :].astype(BF16)
            dov = do_ref[rows, :].astype(BF16)
            s_prev = sp_ref[j]
            ds_b = ds_next.astype(BF16)
            qr, kr, ke, qe = qf * eq, kk * ek, kk * ee, qf * ec
            att = jnp.where(low, _dot(qr.astype(BF16), kr.astype(BF16), NT), 0.0).astype(BF16)
            datt = jnp.where(low, _dot(dov, v, NT), 0.0).astype(BF16)
            dqe = _dot(dov, s_prev, NT)
            dke = _dot(v, ds_b, NT)
            dqr = _dot(datt, kr.astype(BF16), NN)
            dkr = _dot(datt, qr.astype(BF16), TN)
            dv_ref[rows, :] = (_dot(ke.astype(BF16), ds_b, NN) + _dot(att, dov, TN)).astype(dv_ref.dtype)
            dcd = _dot(jnp.ones((8, kd), F32), ds_next * s_prev.astype(F32), NT, HI)[0:1]
            a_q, a_k, a_e, a_c = dqr * qr, dkr * kr, dke * ke, dqe * qe
            dmid = jnp.sum(a_k - a_q, axis=0, keepdims=True)
            dcend = jnp.sum(a_e, axis=0, keepdims=True) + dcd * jnp.exp(cend)
            dcum = a_q - a_k - a_e + a_c + jnp.where(row == l // 2 - 1, dmid, 0.0) + jnp.where(row == l - 1, dcend, 0.0)
            dlf = _dot(_tri(l, upper=True).astype(F32), dcum, NN, HI)
            df = dlf / f - (dkr * ek + dke * ee)
            df_ref[rows, :] = (df * (1.0 - lbv) * sig * (1.0 - sig)).astype(df_ref.dtype)
            dq_ref[rows, :] = ((dqr * eq + dqe * ec) * _dsilu(q)).astype(dq_ref.dtype)
            dlb = dlb + jnp.sum(df * (1.0 - sig), axis=0, keepdims=True)
            ds_next = ds_next * cd_col + _dot(qe.astype(BF16), dov, TN)
        ds_s[...] = ds_next

        @pl.when(first)
        def _():
            dlb_ref[...] = dlb

        @pl.when(jnp.logical_not(first))
        def _():
            dlb_ref[...] += dlb

    def col(j):
        return pl.BlockSpec((cb * l, kd), lambda h, c: (nb - 1 - c, j * nh + h))

    head = pl.BlockSpec((1, kd), lambda h, c: (0, h))
    return pl.pallas_call(
        body, name=name, grid=(nh, nb),
        in_specs=[col(0), col(1), col(2), head, pl.BlockSpec((cb, kd, kd), lambda h, c: (nb - 1 - c, h, 0)), col(0)],
        out_specs=[col(0), col(0), col(0), head],
        out_shape=[jax.ShapeDtypeStruct((t, d), BF16)] * 3 + [jax.ShapeDtypeStruct((1, d), F32)],
        scratch_shapes=[pltpu.VMEM((kd, kd), F32)],
        compiler_params=_cparams(("parallel", "arbitrary")),
    )(qfvg, qfvg, qfvg, lb, s_prev_all, do)


def _softplus(x):
    return jnp.maximum(x, 0.0) + jnp.log1p(jnp.exp(-jnp.abs(x)))


def _grouped(fn, width, *arrs):
    n = arrs[0].shape[1] // width
    outs = [fn(*[a[:, i * width:(i + 1) * width] for a in arrs]) for i in range(n)]
    if isinstance(outs[0], tuple):
        return tuple(jnp.concatenate([o[j] for o in outs], axis=1) for j in range(len(outs[0])))
    return jnp.concatenate(outs, axis=1)


def mixer_fwd(h, norm_w, wts, conv_w, conv_b, dt_bias, a_log, d_skip, ssd_norm_w, lb, hgrn_norm_w, w_out, tag):
    t, d = h.shape
    nh = d // SSD_HEAD_DIM
    p = SSD_HEAD_DIM
    w_z, w_xbc, w_qfvg, w_dt = wts
    hn = norm_fwd(h, norm_w, f"{tag}_norm")
    z = matmul(hn, w_z, "nn", F32, f"{tag}_z")
    xbc_raw = matmul(hn, w_xbc, "nn", F32, f"{tag}_xbc")
    qfvg = matmul(hn, w_qfvg, "nn", F32, f"{tag}_qfvg")
    dt_raw = matmul(hn, w_dt, "nn", F32, f"{tag}_dt")
    xbc = conv_fwd(xbc_raw, conv_w, conv_b, f"{tag}_conv")
    dt = rowwise(lambda x, b: _softplus(x + b), f"{tag}_softplus", [dt_raw], [dt_bias], [(LANE, F32)])[0]
    dt_h = dt[:, :nh]
    dtr = dt_h.T.reshape(nh, t // CHUNK, 1, CHUNK)
    dtc = dt_h.T[:, :, None]
    dt_lane = jnp.repeat(dt_h, p, axis=1)
    a_lane = jnp.repeat(-jnp.exp(a_log[:, :nh]), p, axis=1)
    al3 = a_log[0, :nh].reshape(nh, 1, 1)
    ds3 = d_skip.reshape(nh, 1, 1)
    x3 = xbc[:, :d].reshape(t, nh, p).transpose(1, 0, 2)
    y3 = ssd_intra_fwd(x3, xbc, dtc, dtr, al3, ds3, f"{tag}_ssd_intra")
    y_off, s_ssd = ssd_state_fwd(xbc, dt_lane, a_lane, f"{tag}_ssd_state")
    y_diag = y3.transpose(1, 0, 2).reshape(t, d)
    o_b, s_hgrn = hgrn_fwd(qfvg, lb, f"{tag}_hgrn")
    gw = d // SSD_GROUPS

    def gate(yd, yo, z_, o, g_, nw_a, nw_b):
        ya = (yd + yo) * _silu(z_)
        ya = _grouped(lambda a, w: a * _rms(a) * w, gw, ya, nw_a)
        yb = _grouped(lambda a, w: a * _rms(a) * w, HGRN_KDIM, o, nw_b) * _silu(g_)
        return jnp.concatenate([ya, yb], axis=1)

    cat = rowwise(gate, f"{tag}_gate", [y_diag, y_off, z, o_b, (qfvg, d, 3)], [ssd_norm_w, hgrn_norm_w], [(2 * d, BF16)], tm=128)[0]
    out = matmul(cat, w_out, "nn", F32, f"{tag}_out", residual=h)
    saved = (h, hn, z, xbc_raw, qfvg, dt_raw, xbc, dtc, dtr, dt_lane, a_lane, al3, ds3, x3, y_diag, y_off, s_ssd, o_b, s_hgrn, cat)
    return out, saved


def mixer_bwd(dout, saved, norm_w, wts, conv_w, conv_b, dt_bias, a_log, ssd_norm_w, lb, hgrn_norm_w, w_out, tag):
    (h, hn, z, xbc_raw, qfvg, dt_raw, xbc, dtc, dtr, dt_lane, a_lane, al3, ds3, x3, y_diag, y_off, s_ssd, o_b, s_hgrn, cat) = saved
    t, d = h.shape
    nh = d // SSD_HEAD_DIM
    p = SSD_HEAD_DIM
    gw = d // SSD_GROUPS
    w_z, w_xbc, w_qfvg, w_dt = wts
    dcat = matmul(dout, w_out, "nt", F32, f"{tag}_dcat")
    dw_out = matmul(cat, dout, "tn", F32, f"{tag}_dwout")

    def gate_bwd(dya_n, dyb_g, yd, yo, z_, o, g_, nw_a, nw_b):
        y = yd + yo
        sz = _silu(z_)
        dya, dnw_a = _grouped(lambda a, w, dy: _norm_bwd(a, w, dy), gw, y * sz, nw_a, dya_n)
        sg = _silu(g_)
        tb = _grouped(lambda a, w: a * _rms(a) * w, HGRN_KDIM, o, nw_b)
        do_, dnw_b = _grouped(lambda a, w, dy: _norm_bwd(a, w, dy), HGRN_KDIM, o, nw_b, dyb_g * sg)
        return dya * sz, dya * y * _dsilu(z_), do_, dyb_g * tb * _dsilu(g_), dnw_a, dnw_b

    dy, dz, do_b, dg, dnw_a, dnw_b = rowwise(
        gate_bwd, f"{tag}_dgate", [(dcat, d, 0), (dcat, d, 1), y_diag, y_off, z, o_b, (qfvg, d, 3)],
        [ssd_norm_w, hgrn_norm_w], [(d, F32), (d, BF16), (d, F32), (d, BF16)], [(1, d), (1, d)], tm=128)
    dq, dfr, dv, dlb = hgrn_bwd(qfvg, lb, s_hgrn, do_b, f"{tag}_dhgrn")
    dqfvg = jnp.concatenate([dq, dfr, dv, dg], axis=1)
    dy3 = dy.reshape(t, nh, p).transpose(1, 0, 2)
    dx3, ddt3, db_a, dc_a, dal_a, dds = ssd_intra_bwd(x3, xbc, dtc, dtr, al3, ds3, dy3, f"{tag}_dssd_intra")
    dx_s, ddt_lane, db_s, dc_s, da_lane = ssd_state_bwd(xbc, dt_lane, a_lane, s_ssd, dy, f"{tag}_dssd_state")
    n = SSD_STATE
    dxbc_act_parts = (dx3.transpose(1, 0, 2).reshape(t, d), dx_s,
                      db_a.transpose(1, 0, 2).reshape(t, SSD_GROUPS * n), db_s,
                      dc_a.transpose(1, 0, 2).reshape(t, SSD_GROUPS * n), dc_s)
    sel = _head_selector(d, p)
    ddt_a = jnp.pad(ddt3[:, :, 0].T, ((0, 0), (0, LANE - nh)))

    def dt_bwd(ddl, dda, x, b, s):
        r = (_dot(ddl, s, NN, HI) + dda) * _sigmoid(x + b)
        return r, jnp.sum(r, axis=0, keepdims=True)

    ddt_raw, ddt_bias = rowwise(dt_bwd, f"{tag}_ddt", [ddt_lane, ddt_a, dt_raw], [dt_bias, sel], [(LANE, BF16)], [(1, LANE)])
    a_pad = -jnp.exp(a_log)
    dal_a_row = jnp.pad(dal_a.reshape(1, nh), ((0, 0), (0, LANE - nh)))
    dalog = rowwise(lambda dal, da, a, s: dal + _dot(da, s, NN, HI) * a, f"{tag}_dalog",
                    [dal_a_row, da_lane, a_pad], [sel], [(LANE, F32)])[0]
    dxbc_act = rowwise(lambda x1, x2, b1, b2, c1, c2: jnp.concatenate([x1 + x2, b1 + b2, c1 + c2], axis=1),
                       f"{tag}_dxbc_sum", list(dxbc_act_parts), [], [(d + 2 * SSD_GROUPS * n, F32)], tm=128)[0]
    dxbc_raw, dconv_w, dconv_b = conv_bwd(xbc_raw, conv_w, conv_b, dxbc_act, f"{tag}_dconv")
    dhn = matmul(dz, w_z, "nt", F32, f"{tag}_dhn_z")
    dhn = matmul(dxbc_raw, w_xbc, "nt", F32, f"{tag}_dhn_xbc", residual=dhn)
    dhn = matmul(dqfvg, w_qfvg, "nt", F32, f"{tag}_dhn_qfvg", residual=dhn)
    dhn = matmul(ddt_raw, w_dt, "nt", F32, f"{tag}_dhn_dt", residual=dhn)
    dw_z = matmul(hn, dz, "tn", F32, f"{tag}_dwz")
    dw_xbc = matmul(hn, dxbc_raw, "tn", F32, f"{tag}_dwxbc")
    dw_qfvg = matmul(hn, dqfvg, "tn", F32, f"{tag}_dwqfvg")
    dw_dt = matmul(hn, ddt_raw, "tn", F32, f"{tag}_dwdt")
    dh, dnw = norm_bwd_res(h, norm_w, dhn, dout, f"{tag}_dnorm")
    grads = dict(mix_norm=dnw, w_z=dw_z, w_xbc=dw_xbc, w_qfvg=dw_qfvg, w_dt=dw_dt, conv_w=dconv_w, conv_b=dconv_b,
                 dt_bias=ddt_bias, a_log=dalog, d_skip=dds.reshape(1, nh), ssd_norm=dnw_a, lb=dlb, hgrn_norm=dnw_b,
                 w_out=dw_out)
    return dh, grads


ANY = pl.BlockSpec(memory_space=pl.ANY)


def _place():
    x, y, c = lax.axis_index("x"), lax.axis_index("y"), lax.axis_index("c")
    chips = [(1 - x, y), (x, 1 - y), (1 - x, 1 - y)]
    return x, y, c, chips


def _rcopy(src, dst, send_sems, recv_sems, k, dev):
    return pltpu.make_async_remote_copy(src_ref=src, dst_ref=dst, send_sem=send_sems.at[k], recv_sem=recv_sems.at[k],
                                        device_id=dev, device_id_type=MESH_ID)


def gather_weights(wsh):
    def body(w_ref, o_ref, send_sems, recv_sems):
        x, y, c, chips = _place()
        me = 2 * x + y
        sib = (x, y, 1 - c)
        first = [_rcopy(w_ref.at[c], o_ref.at[me, c], send_sems, recv_sems, j, (cx, cy, c)) for j, (cx, cy) in enumerate(chips)]
        for cp in first:
            cp.start()
        passed = []
        for j, (cx, cy) in enumerate(chips):
            blk = o_ref.at[2 * cx + cy, c]
            _rcopy(blk, blk, send_sems, recv_sems, j, sib).wait_recv()
            cp = _rcopy(blk, blk, send_sems, recv_sems, 3 + j, sib)
            cp.start()
            passed.append(cp)
        for j, (cx, cy) in enumerate(chips):
            blk = o_ref.at[2 * cx + cy, 1 - c]
            _rcopy(blk, blk, send_sems, recv_sems, 3 + j, sib).wait_recv()
        for cp in first + passed:
            cp.wait_send()

    return pl.pallas_call(
        body, name="gather_weights", in_specs=[ANY], out_specs=ANY,
        out_shape=jax.ShapeDtypeStruct((4,) + wsh.shape, wsh.dtype),
        scratch_shapes=[pltpu.SemaphoreType.DMA((6,)), pltpu.SemaphoreType.DMA((6,))],
    )(wsh)


def exchange_halves(gh):
    def body(g_ref, o_ref, send_sems, recv_sems):
        x, y, c, _ = _place()
        cp = _rcopy(g_ref.at[1 - c], o_ref, send_sems, recv_sems, 0, (x, y, 1 - c))
        cp.start()
        cp.wait()

    return pl.pallas_call(
        body, name="rs_exchange_halves", in_specs=[ANY], out_specs=ANY,
        out_shape=jax.ShapeDtypeStruct(gh.shape[1:], gh.dtype),
        scratch_shapes=[pltpu.SemaphoreType.DMA((1,)), pltpu.SemaphoreType.DMA((1,))],
    )(gh)


RS_ROWS = 512


def add_own_half(gh, other, c_idx):
    _, nchip, hr, w = gh.shape
    tr = _pick(hr, (RS_ROWS, 128, 64, 32, 16, 8))

    def body(c_ref, a_ref, b_ref, o_ref):
        o_ref[...] = (a_ref[0] + b_ref[...]).astype(o_ref.dtype)

    grid_spec = pltpu.PrefetchScalarGridSpec(
        num_scalar_prefetch=1, grid=(nchip, hr // tr),
        in_specs=[pl.BlockSpec((1, 1, tr, w), lambda k, i, c_ref: (c_ref[0], k, i, 0)),
                  pl.BlockSpec((1, tr, w), lambda k, i, c_ref: (k, i, 0))],
        out_specs=pl.BlockSpec((1, tr, w), lambda k, i, c_ref: (k, i, 0)))
    return pl.pallas_call(
        body, name="rs_add_own_half", grid_spec=grid_spec, out_shape=jax.ShapeDtypeStruct((nchip, hr, w), BF16),
        compiler_params=_cparams(("parallel", "parallel")),
    )(c_idx, gh, other)


def scatter_to_chips(part):
    def body(p_ref, o_ref, send_sems, recv_sems, local_sem):
        x, y, c, chips = _place()
        me = 2 * x + y
        mine = pltpu.make_async_copy(p_ref.at[me], o_ref.at[me], local_sem)
        mine.start()
        cps = [_rcopy(p_ref.at[2 * cx + cy], o_ref.at[me], send_sems, recv_sems, j, (cx, cy, c)) for j, (cx, cy) in enumerate(chips)]
        for cp in cps:
            cp.start()
        for j, (cx, cy) in enumerate(chips):
            blk = o_ref.at[2 * cx + cy]
            _rcopy(blk, blk, send_sems, recv_sems, j, (cx, cy, c)).wait_recv()
        for cp in cps:
            cp.wait_send()
        mine.wait()

    return pl.pallas_call(
        body, name="rs_scatter_to_chips", in_specs=[ANY], out_specs=ANY, out_shape=jax.ShapeDtypeStruct(part.shape, part.dtype),
        scratch_shapes=[pltpu.SemaphoreType.DMA((3,)), pltpu.SemaphoreType.DMA((3,)), pltpu.SemaphoreType.DMA],
    )(part)


def sum_chips(slots):
    nchip, hr, w = slots.shape
    tr = _pick(hr, (RS_ROWS, 128, 64, 32, 16, 8))

    def body(s_ref, o_ref):
        acc = s_ref[0].astype(F32)
        for k in range(1, nchip):
            acc = acc + s_ref[k].astype(F32)
        o_ref[...] = acc

    return pl.pallas_call(
        body, name="rs_sum_chips", grid=(hr // tr,), in_specs=[pl.BlockSpec((nchip, tr, w), lambda i: (0, i, 0))],
        out_specs=pl.BlockSpec((tr, w), lambda i: (i, 0)), out_shape=jax.ShapeDtypeStruct((hr, w), F32),
        compiler_params=_cparams(("parallel",)),
    )(slots)


def share_with_sibling(half):
    def body(h_ref, o_ref, send_sems, recv_sems):
        x, y, c, _ = _place()
        cp = _rcopy(h_ref, o_ref.at[c], send_sems, recv_sems, 0, (x, y, 1 - c))
        cp.start()
        blk = o_ref.at[1 - c]
        _rcopy(blk, blk, send_sems, recv_sems, 0, (x, y, 1 - c)).wait_recv()
        cp.wait_send()

    return pl.pallas_call(
        body, name="rs_share_with_sibling", in_specs=[ANY], out_specs=ANY,
        out_shape=jax.ShapeDtypeStruct((2,) + half.shape, half.dtype),
        scratch_shapes=[pltpu.SemaphoreType.DMA((1,)), pltpu.SemaphoreType.DMA((1,))],
    )(half)


def reduce_scatter_grads(gh, c_idx):
    other = exchange_halves(gh)
    part = add_own_half(gh, other, c_idx)
    slots = scatter_to_chips(part)
    half = sum_chips(slots)
    both = lax.dynamic_update_slice(share_with_sibling(half), half[None], (c_idx[0], 0, 0))
    return both.reshape(2 * half.shape[0], half.shape[1])


def allreduce_small(v, name):
    rows, w = v.shape

    def body(v_ref, o_ref, slots, send_sems, recv_sems):
        x, y, c, _ = _place()
        me = 4 * x + 2 * y + c
        slots[me] = v_ref[...]
        cps = []
        for r in range(1, 8):
            dev = (x ^ (r >> 2), y ^ ((r >> 1) & 1), c ^ (r & 1))
            cp = _rcopy(v_ref, slots.at[me], send_sems, recv_sems, r - 1, dev)
            cp.start()
            cps.append(cp)
        for r in range(1, 8):
            blk = slots.at[me ^ r]
            _rcopy(blk, blk, send_sems, recv_sems, r - 1, (x, y, c)).wait_recv()
        for cp in cps:
            cp.wait_send()
        acc = slots[0]
        for k in range(1, 8):
            acc = acc + slots[k]
        o_ref[...] = acc

    vm = pl.BlockSpec(memory_space=pltpu.VMEM)
    return pl.pallas_call(
        body, name=name, in_specs=[vm], out_specs=vm, out_shape=jax.ShapeDtypeStruct((rows, w), F32),
        scratch_shapes=[pltpu.VMEM((8, rows, w), F32), pltpu.SemaphoreType.DMA((7,)), pltpu.SemaphoreType.DMA((7,))],
    )(v)


def adamw(w, g, m, v, name):
    def fn(w_, g_, m_, v_):
        m2 = ADAM_B1 * m_ + (1.0 - ADAM_B1) * g_
        v2 = ADAM_B2 * v_ + (1.0 - ADAM_B2) * (g_ * g_)
        m_hat = m2 / (1.0 - ADAM_B1 ** ADAM_STEP)
        v_hat = v2 / (1.0 - ADAM_B2 ** ADAM_STEP)
        return -ADAM_LR * (m_hat / (jnp.sqrt(v_hat) + ADAM_EPS) + ADAM_WD * w_), m2, v2

    cols = w.shape[1]
    tm = _pick(w.shape[0], (64, 32, 16, 8) if cols > 1024 else (256, 128, 64, 32, 16, 8))
    return rowwise(fn, name, [w, g, m, v], [], [(cols, F32)] * 3, tm=tm)


def _pack_rows(arrs, row_align, total_align=1):
    parts, offs, r = [], [], 0
    for a in arrs:
        n = a.size
        nr = -(-n // (FLAT_W * row_align)) * row_align
        parts.append(jnp.pad(a.reshape(-1), (0, nr * FLAT_W - n)).reshape(nr, FLAT_W))
        offs.append(r)
        r += nr
    if r % total_align:
        parts.append(jnp.zeros((-r % total_align, FLAT_W), arrs[0].dtype))
    return jnp.concatenate(parts, axis=0), offs


def _unpack_rows(packed, offs, shapes):
    out = []
    for o, s in zip(offs, shapes):
        n = 1
        for k in s:
            n *= k
        nr = -(-n // FLAT_W)
        out.append(packed[..., o:o + nr, :].reshape(packed.shape[:-2] + (nr * FLAT_W,))[..., :n].reshape(packed.shape[:-2] + tuple(s)))
    return out


BIG = ("ffn1_w_in", "ffn1_w_out", "ab_w_in", "ab_w_out", "fox_w_in", "fox_w_out", "ffn2_w_in", "ffn2_w_out",
       "ple_w_gate", "ple_w_up")
COL_SHARDED = ("ffn1_w_in", "ab_w_in", "fox_w_in", "ffn2_w_in", "ple_w_up")
SMALL = ("ffn1_norm", "mix_norm", "ssd_conv_w", "ssd_conv_b", "ssd_dt_bias", "ssd_a_log", "ssd_d", "ssd_norm",
         "hgrn_lb_logits", "hgrn_norm", "fox_b_f", "ffn2_norm", "ple_gate_norm", "ple_norm", "final_norm")
WEIGHTS = ("ffn1_norm", "ffn1_w_in", "ffn1_w_out", "mix_norm", "ab_w_in", "ssd_conv_w", "ssd_conv_b", "ssd_dt_bias",
           "ssd_a_log", "ssd_d", "ssd_norm", "hgrn_lb_logits", "hgrn_norm", "ab_w_out", "fox_w_in", "fox_b_f", "fox_w_out",
           "ffn2_norm", "ffn2_w_in", "ffn2_w_out", "ple_gate_norm", "ple_w_gate", "ple_w_up", "ple_norm", "final_norm")


def _full_from_shards(name, g4):
    if name in COL_SHARDED:
        return jnp.moveaxis(g4, 0, 2).reshape(g4.shape[1], g4.shape[2], 4 * g4.shape[3])
    return jnp.moveaxis(g4, 0, 1).reshape(g4.shape[1], 4 * g4.shape[2], g4.shape[3])


def _shards_from_full(name, full):
    ly, r, c = full.shape
    if name in COL_SHARDED:
        return jnp.moveaxis(full.reshape(ly, r, 4, c // 4), 2, 0)
    return jnp.moveaxis(full.reshape(ly, 4, r // 4, c), 1, 0)


def _pad_to(a, axis, n):
    pad = [(0, 0)] * a.ndim
    pad[axis] = (0, n - a.shape[axis])
    return jnp.pad(a, pad)


def _lane_row(v):
    return _pad_to(v.reshape(1, -1), 1, LANE)


def kernel(x, p, ffn1_norm, ffn1_w_in, ffn1_w_out, mix_norm, ab_w_in, ssd_conv_w, ssd_conv_b, ssd_dt_bias, ssd_a_log, ssd_d, ssd_norm, hgrn_lb_logits, hgrn_norm, ab_w_out, fox_w_in, fox_b_f, fox_w_out, ffn2_norm, ffn2_w_in, ffn2_w_out, ple_gate_norm, ple_w_gate, ple_w_up, ple_norm, final_norm, loss_target, m_ffn1_norm, m_ffn1_w_in, m_ffn1_w_out, m_mix_norm, m_ab_w_in, m_ssd_conv_w, m_ssd_conv_b, m_ssd_dt_bias, m_ssd_a_log, m_ssd_d, m_ssd_norm, m_hgrn_lb_logits, m_hgrn_norm, m_ab_w_out, m_fox_w_in, m_fox_b_f, m_fox_w_out, m_ffn2_norm, m_ffn2_w_in, m_ffn2_w_out, m_ple_gate_norm, m_ple_w_gate, m_ple_w_up, m_ple_norm, m_final_norm, v_ffn1_norm, v_ffn1_w_in, v_ffn1_w_out, v_mix_norm, v_ab_w_in, v_ssd_conv_w, v_ssd_conv_b, v_ssd_dt_bias, v_ssd_a_log, v_ssd_d, v_ssd_norm, v_hgrn_lb_logits, v_hgrn_norm, v_ab_w_out, v_fox_w_in, v_fox_b_f, v_fox_w_out, v_ffn2_norm, v_ffn2_w_in, v_ffn2_w_out, v_ple_gate_norm, v_ple_w_gate, v_ple_w_up, v_ple_norm, v_final_norm):
    given = dict(locals())
    w = {n: given[n] for n in WEIGHTS}
    mom = {n: given["m_" + n] for n in WEIGHTS}
    var = {n: given["v_" + n] for n in WEIGHTS}
    h = x[0]
    t, d = h.shape
    depth = p.shape[0]
    nh_ssd = d // SSD_HEAD_DIM
    nh_fox = d // FOX_HEAD_DIM
    conv_dim = d + 2 * SSD_GROUPS * SSD_STATE
    d_ff = ffn1_w_out.shape[1] * 4
    fp = -(-d_ff // FF_ALIGN) * FF_ALIGN
    xi, yi, ci = lax.axis_index("x"), lax.axis_index("y"), lax.axis_index("c")
    chip = 2 * xi + yi

    big_local = [w[n].astype(BF16) for n in BIG]
    packed, offs = _pack_rows(big_local, ROW_ALIGN, PACK_ALIGN)
    rows = packed.shape[0]
    halves = packed.reshape(2, rows // 2, FLAT_W)
    gathered = lax.dynamic_update_slice(gather_weights(halves), halves[None], (chip, 0, 0, 0)).reshape(4, rows, FLAT_W)
    full = {n: _full_from_shards(n, g4) for n, g4 in zip(BIG, _unpack_rows(gathered, offs, [a.shape for a in big_local]))}
    cw_local = ssd_conv_w[0]
    cshard = cw_local.shape[1]
    cw_rows = -(-SSD_CONV * conv_dim // FLAT_W)
    cw_placed = lax.dynamic_update_slice(jnp.zeros((SSD_CONV, conv_dim), F32), cw_local, (0, chip * cshard))
    cw_placed = jnp.where(ci == 0, cw_placed, 0.0)
    cw_packed, _ = _pack_rows([cw_placed], 8)
    conv_w = allreduce_small(cw_packed, "gather_conv_w")[:cw_rows].reshape(-1)[:SSD_CONV * conv_dim].reshape(SSD_CONV, conv_dim)

    def ffn_weights(w_in, w_out):
        gate, up = w_in[:, :d_ff], w_in[:, d_ff:]
        return jnp.concatenate([_pad_to(gate, 1, fp), _pad_to(up, 1, fp)], axis=1), _pad_to(w_out, 0, fp)

    ffn_w = {(k, i): ffn_weights(full[f"ffn{k}_w_in"][i], full[f"ffn{k}_w_out"][i]) for k in (1, 2) for i in range(depth)}
    ab = full["ab_w_in"][0]
    s = [0, d, d + conv_dim, d + conv_dim + nh_ssd]
    ab_wts = (ab[:, s[0]:s[1]], ab[:, s[1]:s[2]], ab[:, s[3]:], _pad_to(ab[:, s[2]:s[3]], 1, LANE))
    fox = full["fox_w_in"][0]
    fox_qkv, fox_f = fox[:, :3 * d], _pad_to(fox[:, 3 * d:], 1, LANE)
    fox_bias = _lane_row(fox_b_f[0])
    dt_bias, a_log = _lane_row(ssd_dt_bias[0]), _lane_row(ssd_a_log[0])
    lb_soft = rowwise(lambda z: (lambda e: e / jnp.sum(e, axis=0, keepdims=True))(jnp.exp(z - jnp.max(z, axis=0, keepdims=True))),
                      "lb_softmax", [hgrn_lb_logits], [], [(d, F32)], tm=hgrn_lb_logits.shape[0])[0]
    lb = lb_soft[0:1]
    conv_b = ssd_conv_b

    saved = []
    for i in range(depth):
        h, s1 = ffn_fwd(h, ffn1_norm[i:i + 1], *ffn_w[(1, i)], f"l{i}_ffn1")
        if i % 2 == 0:
            h, s2 = mixer_fwd(h, mix_norm[i:i + 1], ab_wts, conv_w, conv_b, dt_bias, a_log, ssd_d[0], ssd_norm, lb, hgrn_norm,
                              full["ab_w_out"][0], f"l{i}_mix")
        else:
            h, s2 = fox_fwd(h, mix_norm[i:i + 1], fox_qkv, fox_f, fox_bias, full["fox_w_out"][0], f"l{i}_fox")
        h, s3 = ffn_fwd(h, ffn2_norm[i:i + 1], *ffn_w[(2, i)], f"l{i}_ffn2")
        h, s4 = ple_fwd(h, p[i, 0], ple_gate_norm[i:i + 1], full["ple_w_gate"][i], full["ple_w_up"][i], ple_norm[i:i + 1], f"l{i}_ple")
        saved.append((s1, s2, s3, s4))
    loss, dh, g_final = loss_head(h, loss_target[0], final_norm.reshape(1, d))

    gb = {n: [None] * w[n].shape[0] for n in BIG}
    gs = {n: [None] * w[n].shape[0] for n in SMALL}
    gs["final_norm"] = g_final[0]

    def ffn_grads(k, i, dw_in, dw_out):
        gb[f"ffn{k}_w_in"][i] = jnp.concatenate([dw_in[:, :d_ff], dw_in[:, fp:fp + d_ff]], axis=1)
        gb[f"ffn{k}_w_out"][i] = dw_out[:d_ff]

    for i in reversed(range(depth)):
        s1, s2, s3, s4 = saved[i]
        dh, dgn, dwg, dwu, dpn = ple_bwd(dh, s4, p[i, 0], ple_gate_norm[i:i + 1], full["ple_w_gate"][i], ple_norm[i:i + 1], f"l{i}_ple")
        gs["ple_gate_norm"][i], gs["ple_norm"][i] = dgn[0], dpn[0]
        gb["ple_w_gate"][i], gb["ple_w_up"][i] = dwg, dwu
        dh, dnw, dw_in, dw_out = ffn_bwd(dh, s3, ffn2_norm[i:i + 1], *ffn_w[(2, i)], f"l{i}_ffn2")
        gs["ffn2_norm"][i] = dnw[0]
        ffn_grads(2, i, dw_in, dw_out)
        if i % 2 == 0:
            dh, gm = mixer_bwd(dh, s2, mix_norm[i:i + 1], ab_wts, conv_w, conv_b, dt_bias, a_log, ssd_norm, lb, hgrn_norm,
                               full["ab_w_out"][0], f"l{i}_mix")
            gs["mix_norm"][i] = gm["mix_norm"][0]
            q4 = gm["w_qfvg"]
            gb["ab_w_in"][0] = jnp.concatenate([gm["w_z"], gm["w_xbc"], gm["w_dt"][:, :nh_ssd], q4], axis=1)
            gb["ab_w_out"][0] = gm["w_out"]
            gs["ssd_conv_w"][0], gs["ssd_conv_b"][0] = gm["conv_w"], gm["conv_b"][0]
            gs["ssd_dt_bias"][0], gs["ssd_a_log"][0] = gm["dt_bias"][0, :nh_ssd], gm["a_log"][0, :nh_ssd]
            gs["ssd_d"][0], gs["ssd_norm"][0], gs["hgrn_norm"][0] = gm["d_skip"][0], gm["ssd_norm"][0], gm["hgrn_norm"][0]
            dlb = gm["lb"]
        else:
            dh, dnw, dwqkv, dwf, dbf, dwo = fox_bwd(dh, s2, mix_norm[i:i + 1], fox_qkv, fox_f, fox_bias, full["fox_w_out"][0], f"l{i}_fox")
            gs["mix_norm"][i] = dnw[0]
            gb["fox_w_in"][0] = jnp.concatenate([dwqkv, dwf[:, :nh_fox]], axis=1)
            gb["fox_w_out"][0] = dwo
            gs["fox_b_f"][0] = dbf[0, :nh_fox]
        dh, dnw, dw_in, dw_out = ffn_bwd(dh, s1, ffn1_norm[i:i + 1], *ffn_w[(1, i)], f"l{i}_ffn1")
        gs["ffn1_norm"][i] = dnw[0]
        ffn_grads(1, i, dw_in, dw_out)
    grad_x = dh[None]
    first_row = (jnp.arange(hgrn_lb_logits.shape[0]) == 0).astype(F32)[:, None]
    gs["hgrn_lb_logits"] = rowwise(lambda sm, g, e: sm * (e - sm[0:1]) * g, "lb_softmax_bwd",
                                   [lb_soft, jnp.broadcast_to(dlb, lb_soft.shape), jnp.broadcast_to(first_row, lb_soft.shape)],
                                   [], [(d, F32)], tm=lb_soft.shape[0])[0]

    g4 = [_shards_from_full(n, jnp.stack(gb[n])) for n in BIG]
    g_packed = jnp.stack([_pack_rows([a[k] for a in g4], ROW_ALIGN, PACK_ALIGN)[0] for k in range(4)])
    gh = jnp.moveaxis(g_packed.reshape(4, 2, rows // 2, FLAT_W), 1, 0)
    g_red = reduce_scatter_grads(gh, ci.reshape(1).astype(jnp.int32))
    g_big = dict(zip(BIG, _unpack_rows(g_red, offs, [a.shape for a in big_local])))

    small_local = [jnp.stack(gs[n]) if isinstance(gs[n], list) else gs[n] for n in SMALL]
    small_local = [a.reshape(w[n].shape if n != "ssd_conv_w" else (1, SSD_CONV, conv_dim)) for n, a in zip(SMALL, small_local)]
    sp, soffs = _pack_rows(small_local + [loss.reshape(1)], 8)
    sr = allreduce_small(sp, "allreduce_small_grads")
    small_red = _unpack_rows(sr, soffs, [a.shape for a in small_local] + [(1,)])
    loss_total = small_red[-1][0]
    g_small = dict(zip(SMALL, small_red[:-1]))
    g_small["ssd_conv_w"] = lax.dynamic_slice(g_small["ssd_conv_w"], (0, 0, chip * cshard), (1, SSD_CONV, cshard))

    grads, delta, new_m, new_v = {}, {}, {}, {}
    for n in BIG:
        shp = w[n].shape
        two_d = (shp[0] * shp[1], shp[2])
        grads[n] = g_big[n]
        dl, m2, v2 = adamw(w[n].reshape(two_d), g_big[n].reshape(two_d), mom[n].reshape(two_d), var[n].reshape(two_d), f"adamw_{n}")
        delta[n], new_m[n], new_v[n] = dl.reshape(shp), m2.reshape(shp), v2.reshape(shp)
    packs = [_pack_rows([src[n] for n in SMALL], 8) for src in (w, g_small, mom, var)]
    dl, m2, v2 = adamw(packs[0][0], packs[1][0], packs[2][0], packs[3][0], "adamw_small")
    shapes = [w[n].shape for n in SMALL]
    for n, a, b, c_ in zip(SMALL, _unpack_rows(dl, packs[0][1], shapes), _unpack_rows(m2, packs[0][1], shapes), _unpack_rows(v2, packs[0][1], shapes)):
        grads[n], delta[n], new_m[n], new_v[n] = g_small[n], a, b, c_
    return (loss_total, grad_x, *[grads[n] for n in WEIGHTS], *[delta[n] for n in WEIGHTS],
            *[new_m[n] for n in WEIGHTS], *[new_v[n] for n in WEIGHTS])
```

```python
import functools

import jax
import jax.numpy as jnp
from jax import lax
from jax.experimental import pallas as pl
from jax.experimental.pallas import tpu as pltpu

F32 = jnp.float32
BF16 = jnp.bfloat16
HI = lax.Precision.HIGHEST

EPS = 1e-6
CHUNK = 64
SSD_HEAD_DIM = 64
SSD_GROUPS = 4
SSD_STATE = 128
SSD_CONV = 4
HGRN_KDIM = 128
FOX_HEAD_DIM = 128
LANE = 128
FF_ALIGN = 512
FLAT_W = 2048
ROW_ALIGN = 32
PACK_ALIGN = 1024

ADAM_LR = 0.001
ADAM_B1 = 0.9
ADAM_B2 = 0.999
ADAM_EPS = 1e-08
ADAM_WD = 0.01
ADAM_STEP = 10

VMEM_LIMIT = 56 * 1024 * 1024
MESH_ID = pl.DeviceIdType.MESH


def _cparams(sem):
    return pltpu.CompilerParams(dimension_semantics=sem, vmem_limit_bytes=VMEM_LIMIT)


def _pick(n, prefs):
    for t in prefs:
        if n % t == 0:
            return t
    return n


def _dot(a, b, dims, precision=None):
    return lax.dot_general(a, b, (dims, ((), ())), preferred_element_type=F32, precision=precision)


NN = ((1,), (0,))
NT = ((1,), (1,))
TN = ((0,), (0,))


def _sigmoid(x):
    return 1.0 / (1.0 + jnp.exp(-x))


def _silu(x):
    return x * _sigmoid(x)


def _dsilu(x):
    s = _sigmoid(x)
    return s * (1.0 + x * (1.0 - s))


def matmul(a, b, mode, out_dtype, name, scale=None, residual=None):
    if mode == "nn":
        (m, k), (k2, n) = a.shape, b.shape
    elif mode == "nt":
        (m, k), (n, k2) = a.shape, b.shape
    else:
        (k, m), (k2, n) = a.shape, b.shape
    assert k == k2, (a.shape, b.shape, mode)
    tm = _pick(m, (1024, 512, 256, 128))
    tn = _pick(n, (1024, 1408, 512, 256, 128))
    tk = _pick(k, (2048, 1408, 1024, 512, 256, 128))
    nk = k // tk
    dims = {"nn": NN, "nt": NT, "tn": TN}[mode]

    def body(*refs):
        if residual is None:
            a_ref, b_ref, o_ref, acc_ref = refs
            r_ref = None
        else:
            a_ref, b_ref, r_ref, o_ref, acc_ref = refs
        kk = pl.program_id(2)

        @pl.when(kk == 0)
        def _():
            acc_ref[...] = jnp.zeros_like(acc_ref)

        acc_ref[...] += _dot(a_ref[...].astype(BF16), b_ref[...].astype(BF16), dims)

        @pl.when(kk == nk - 1)
        def _():
            r = acc_ref[...]
            if scale is not None:
                r = r * scale
            if r_ref is not None:
                r = r + r_ref[...].astype(F32)
            o_ref[...] = r.astype(o_ref.dtype)

    if mode == "nn":
        a_spec = pl.BlockSpec((tm, tk), lambda i, j, kk: (i, kk))
        b_spec = pl.BlockSpec((tk, tn), lambda i, j, kk: (kk, j))
    elif mode == "nt":
        a_spec = pl.BlockSpec((tm, tk), lambda i, j, kk: (i, kk))
        b_spec = pl.BlockSpec((tn, tk), lambda i, j, kk: (j, kk))
    else:
        a_spec = pl.BlockSpec((tk, tm), lambda i, j, kk: (kk, i))
        b_spec = pl.BlockSpec((tk, tn), lambda i, j, kk: (kk, j))
    o_spec = pl.BlockSpec((tm, tn), lambda i, j, kk: (i, j))
    in_specs = [a_spec, b_spec]
    args = [a, b]
    if residual is not None:
        in_specs.append(o_spec)
        args.append(residual)
    return pl.pallas_call(
        body, name=name, grid=(m // tm, n // tn, nk),
        in_specs=in_specs, out_specs=o_spec,
        out_shape=jax.ShapeDtypeStruct((m, n), out_dtype),
        scratch_shapes=[pltpu.VMEM((tm, tn), F32)],
        compiler_params=_cparams(("parallel", "parallel", "arbitrary")),
    )(*args)


def rowwise(fn, name, rows, consts, outs, accs=(), tm=256):
    rows = [r if isinstance(r, tuple) else (r, r.shape[1], 0) for r in rows]
    t = rows[0][0].shape[0]
    tm = min(tm, t)
    assert t % tm == 0
    n_in = len(rows) + len(consts)
    n_out = len(outs)

    def body(*refs):
        res = fn(*[r[...] for r in refs[:n_in]])
        if not isinstance(res, tuple):
            res = (res,)
        for r, v in zip(refs[n_in:n_in + n_out], res[:n_out]):
            r[...] = v.astype(r.dtype)
        if accs:
            a_refs = refs[n_in + n_out:]
            first = pl.program_id(0) == 0

            @pl.when(first)
            def _():
                for r, v in zip(a_refs, res[n_out:]):
                    r[...] = v

            @pl.when(jnp.logical_not(first))
            def _():
                for r, v in zip(a_refs, res[n_out:]):
                    r[...] += v

    in_specs = [pl.BlockSpec((tm, w), functools.partial(lambda i, cb: (i, cb), cb=cb)) for _, w, cb in rows]
    in_specs += [pl.BlockSpec(c.shape, lambda i: (0, 0)) for c in consts]
    out_specs = [pl.BlockSpec((tm, w), lambda i: (i, 0)) for w, _ in outs]
    out_specs += [pl.BlockSpec(s, lambda i: (0, 0)) for s in accs]
    out_shape = [jax.ShapeDtypeStruct((t, w), d) for w, d in outs]
    out_shape += [jax.ShapeDtypeStruct(s, F32) for s in accs]
    res = pl.pallas_call(
        body, name=name, grid=(t // tm,), in_specs=in_specs, out_specs=out_specs, out_shape=out_shape,
        compiler_params=_cparams(("arbitrary",) if accs else ("parallel",)),
    )(*[r[0] for r in rows], *consts)
    return res


def _rms(x):
    return lax.rsqrt(jnp.mean(x * x, axis=-1, keepdims=True) + EPS)


def _norm_bwd(x, w, dy):
    r = _rms(x)
    xh = x * r
    g = dy * w
    dx = r * (g - xh * jnp.mean(g * xh, axis=-1, keepdims=True))
    return dx, jnp.sum(dy * xh, axis=0, keepdims=True)


def norm_fwd(h, w, name):
    return rowwise(lambda x, w_: x * _rms(x) * w_, name, [h], [w], [(h.shape[1], BF16)])[0]


def norm_bwd_res(h, w, dhn, dres, name):
    def fn(x, dy, dr, w_):
        dx, dw = _norm_bwd(x, w_, dy.astype(F32))
        return dr + dx, dw
    d = h.shape[1]
    return rowwise(fn, name, [h, dhn, dres], [w], [(d, F32)], [(1, d)])


def _mm(name, grid, a, a_spec, b, b_spec, out_shape, o_spec, acc_shape, dims, scale=None, residual=None):
    nk = grid[2]

    def body(*refs):
        a_ref, b_ref = refs[0], refs[1]
        r_ref = refs[2] if residual is not None else None
        o_ref, acc_ref = refs[-2], refs[-1]
        kk = pl.program_id(2)

        @pl.when(kk == 0)
        def _():
            acc_ref[...] = jnp.zeros_like(acc_ref)

        acc_ref[...] += _dot(a_ref[...].astype(BF16), b_ref[...].astype(BF16), dims)

        @pl.when(kk == nk - 1)
        def _():
            r = acc_ref[...]
            if scale is not None:
                r = r * scale
            if r_ref is not None:
                r = r + r_ref[...]
            o_ref[...] = r.astype(o_ref.dtype)

    in_specs, args = [a_spec, b_spec], [a, b]
    if residual is not None:
        in_specs.append(o_spec)
        args.append(residual)
    return pl.pallas_call(
        body, name=name, grid=grid, in_specs=in_specs, out_specs=o_spec, out_shape=out_shape,
        scratch_shapes=[pltpu.VMEM(acc_shape, F32)],
        compiler_params=_cparams(("parallel", "parallel", "arbitrary")),
    )(*args)


def ffn_fwd(h, norm_w, w_in, w_out, layer, tag):
    t, d = h.shape
    ns = w_in.shape[3]
    tm = _pick(t, (1024, 512, 256, 128))
    tk = _pick(d, (1024, 512, 256, 128))
    tn = _pick(d, (1024, 512, 256, 128))
    hn = norm_fwd(h, norm_w, f"{tag}_norm")
    u = _mm(f"{tag}_in", (t // tm, 4, d // tk),
            hn, pl.BlockSpec((tm, tk), lambda i, s, k: (i, k)),
            w_in, pl.BlockSpec((None, None, tk, ns), lambda i, s, k: (layer, s, k, 0)),
            jax.ShapeDtypeStruct((4, t, ns), BF16), pl.BlockSpec((None, tm, ns), lambda i, s, k: (s, i, 0)), (tm, ns), NN)
    u4 = u.reshape(2, 2, t, ns)
    tr = _pick(t, (256, 128))

    def act(u_ref, a_ref):
        a_ref[...] = (_silu(u_ref[0].astype(F32)) * u_ref[1].astype(F32)).astype(a_ref.dtype)

    a = pl.pallas_call(
        act, name=f"{tag}_act", grid=(2, t // tr),
        in_specs=[pl.BlockSpec((2, None, tr, ns), lambda s, i: (0, s, i, 0))],
        out_specs=pl.BlockSpec((None, tr, ns), lambda s, i: (s, i, 0)),
        out_shape=jax.ShapeDtypeStruct((2, t, ns), BF16), compiler_params=_cparams(("parallel", "parallel")),
    )(u4)
    out = _mm(f"{tag}_out", (t // tm, d // tn, 2),
              a, pl.BlockSpec((None, tm, ns), lambda i, j, k: (k, i, 0)),
              w_out, pl.BlockSpec((None, None, ns, tn), lambda i, j, k: (layer, k, 0, j)),
              jax.ShapeDtypeStruct((t, d), F32), pl.BlockSpec((tm, tn), lambda i, j, k: (i, j)), (tm, tn), NN,
              scale=0.5, residual=h)
    return out, (h, hn, u4, a)


def ffn_bwd(dout, saved, norm_w, w_in, w_out, layer, tag):
    h, hn, u4, a = saved
    t, d = h.shape
    ns = w_in.shape[3]
    tm = _pick(t, (1024, 512, 256, 128))
    tk = _pick(d, (1024, 512, 256, 128))
    tn = _pick(d, (1024, 512, 256, 128))
    tkt = _pick(t, (1024, 512, 256, 128))
    th = _pick(d, (512, 256, 128))
    da = _mm(f"{tag}_da", (t // tm, 2, d // tk),
             dout, pl.BlockSpec((tm, tk), lambda i, s, k: (i, k)),
             w_out, pl.BlockSpec((None, None, ns, tk), lambda i, s, k: (layer, s, 0, k)),
             jax.ShapeDtypeStruct((2, t, ns), BF16), pl.BlockSpec((None, tm, ns), lambda i, s, k: (s, i, 0)), (tm, ns), NT,
             scale=0.5)
    tr = _pick(t, (256, 128))

    def dact(da_ref, u_ref, du_ref):
        da_, g, up = da_ref[...].astype(F32), u_ref[0].astype(F32), u_ref[1].astype(F32)
        du_ref[0] = (da_ * up * _dsilu(g)).astype(du_ref.dtype)
        du_ref[1] = (da_ * _silu(g)).astype(du_ref.dtype)

    pair = pl.BlockSpec((2, None, tr, ns), lambda s, i: (0, s, i, 0))
    du = pl.pallas_call(
        dact, name=f"{tag}_dact", grid=(2, t // tr),
        in_specs=[pl.BlockSpec((None, tr, ns), lambda s, i: (s, i, 0)), pair], out_specs=pair,
        out_shape=jax.ShapeDtypeStruct((2, 2, t, ns), BF16), compiler_params=_cparams(("parallel", "parallel")),
    )(da, u4).reshape(4, t, ns)
    dw_out = _mm(f"{tag}_dwout", (2, d // th, t // tkt),
                 a, pl.BlockSpec((None, tkt, ns), lambda s, j, k: (s, k, 0)),
                 dout, pl.BlockSpec((tkt, th), lambda s, j, k: (k, j)),
                 jax.ShapeDtypeStruct((2, ns, d), F32), pl.BlockSpec((None, ns, th), lambda s, j, k: (s, 0, j)), (ns, th), TN,
                 scale=0.5)
    dhn = _mm(f"{tag}_dhn", (t // tm, d // tn, 4),
              du, pl.BlockSpec((None, tm, ns), lambda i, j, k: (k, i, 0)),
              w_in, pl.BlockSpec((None, None, tn, ns), lambda i, j, k: (layer, k, j, 0)),
              jax.ShapeDtypeStruct((t, d), F32), pl.BlockSpec((tm, tn), lambda i, j, k: (i, j)), (tm, tn), NT)
    dw_in = _mm(f"{tag}_dwin", (d // th, 4, t // tkt),
                hn, pl.BlockSpec((tkt, th), lambda i, s, k: (k, i)),
                du, pl.BlockSpec((None, tkt, ns), lambda i, s, k: (s, k, 0)),
                jax.ShapeDtypeStruct((4, d, ns), F32), pl.BlockSpec((None, th, ns), lambda i, s, k: (s, i, 0)), (th, ns), TN)
    dh, dnw = norm_bwd_res(h, norm_w, dhn, dout, f"{tag}_dnorm")
    return dh, dnw, dw_in, dw_out


def ple_fwd(h, p_i, gate_norm_w, w_gate, w_up, post_norm_w, tag):
    d = h.shape[1]
    e0 = matmul(p_i, w_up, "nn", F32, f"{tag}_up")
    hn = norm_fwd(h, gate_norm_w, f"{tag}_norm")
    s = matmul(hn, w_gate, "nn", F32, f"{tag}_gate")
    out = rowwise(lambda x, e, s_, pw: x + e * _rms(e) * pw * _sigmoid(s_), f"{tag}_add",
                  [h, e0, s], [post_norm_w], [(d, F32)])[0]
    return out, (h, hn, e0, s)


def ple_bwd(dout, saved, p_i, gate_norm_w, w_gate, post_norm_w, tag):
    h, hn, e0, s = saved
    d = h.shape[1]

    def fn(dy, e, s_, pw):
        gate = _sigmoid(s_)
        emb = e * _rms(e) * pw
        de0, dpw = _norm_bwd(e, pw, dy * gate)
        return de0, dy * emb * gate * (1.0 - gate), dpw

    de0, ds, dpw = rowwise(fn, f"{tag}_dadd", [dout, e0, s], [post_norm_w], [(d, BF16), (d, BF16)], [(1, d)])
    dw_up = matmul(p_i, de0, "tn", F32, f"{tag}_dwup")
    dhn = matmul(ds, w_gate, "nt", F32, f"{tag}_dhn")
    dw_gate = matmul(hn, ds, "tn", F32, f"{tag}_dwgate")
    dh, dgn = norm_bwd_res(h, gate_norm_w, dhn, dout, f"{tag}_dnorm")
    return dh, dgn, dw_gate, dw_up, dpw


def loss_head(h, target, w):
    d = h.shape[1]

    def fn(x, tgt, w_):
        r = _rms(x)
        err = x * r * w_ - tgt
        dx, dw = _norm_bwd(x, w_, err * (1.0 / d))
        part = 0.5 * jnp.sum(jnp.sum(err * err, axis=-1, keepdims=True), axis=0, keepdims=True) * (1.0 / d)
        return dx, dw, jnp.broadcast_to(part, (1, LANE))

    dh, dw, loss = rowwise(fn, "loss_head", [h, target], [w], [(d, F32)], [(1, d), (1, LANE)])
    return loss[0, 0], dh, dw


def cumsum_rows(x, name, reverse=False):
    t, w = x.shape
    tb = min(256, t)
    nb = t // tb

    def body(x_ref, o_ref, carry):
        @pl.when(pl.program_id(0) == 0)
        def _():
            carry[...] = jnp.zeros_like(carry)

        r = lax.broadcasted_iota(jnp.int32, (tb, tb), 0)
        c = lax.broadcasted_iota(jnp.int32, (tb, tb), 1)
        tri = ((c >= r) if reverse else (c <= r)).astype(F32)
        y = _dot(tri, x_ref[...], NN, HI) + carry[...]
        o_ref[...] = y
        carry[...] = y[0:1, :] if reverse else y[tb - 1:tb, :]

    idx = (lambda i: (nb - 1 - i, 0)) if reverse else (lambda i: (i, 0))
    return pl.pallas_call(
        body, name=name, grid=(nb,), in_specs=[pl.BlockSpec((tb, w), idx)], out_specs=pl.BlockSpec((tb, w), idx),
        out_shape=jax.ShapeDtypeStruct((t, w), F32), scratch_shapes=[pltpu.VMEM((1, w), F32)],
        compiler_params=_cparams(("arbitrary",)),
    )(x)


def _fox_tiles(t):
    return _pick(t, (1024, 512, 256, 128)), _pick(t, (512, 256, 128))


def _fox_pairs(t, resident_is_query):
    tr, ts = _fox_tiles(t)
    rows = []
    for ri in range(t // tr):
        if resident_is_query:
            sis = list(range((ri * tr + tr - 1) // ts + 1))
        else:
            sis = list(range((ri * tr) // ts, t // ts))
        for si in sis:
            q0, k0 = (ri * tr, si * ts) if resident_is_query else (si * ts, ri * tr)
            qn, kn = (tr, ts) if resident_is_query else (ts, tr)
            rows.append((ri, si, si == sis[0], si == sis[-1], q0 < k0 + kn - 1))
    return tuple(jnp.asarray([r[j] for r in rows], jnp.int32) for j in range(5))


def _causal(s, row0, col0, transposed=False):
    r = row0 + lax.broadcasted_iota(jnp.int32, s.shape, 0)
    c = col0 + lax.broadcasted_iota(jnp.int32, s.shape, 1)
    return jnp.where((c >= r) if transposed else (r >= c), s, -jnp.inf)


def _on_diagonal(flag, step):
    @pl.when(flag == 1)
    def _():
        step(True)

    @pl.when(flag == 0)
    def _():
        step(False)


def fox_attn_fwd(qkv, dcol, drow, name):
    t = qkv.shape[0]
    nh = dcol.shape[0]
    tq, tk = _fox_tiles(t)
    pairs = _fox_pairs(t, True)
    scale = FOX_HEAD_DIM ** -0.5

    def body(qs, ks, fs, ls, dg, q_ref, k_ref, v_ref, dc_ref, dr_ref, o_ref, lse_ref, m_s, l_s, acc_s):
        p = pl.program_id(1)

        @pl.when(fs[p] == 1)
        def _():
            m_s[...] = jnp.full_like(m_s, -jnp.inf)
            l_s[...] = jnp.zeros_like(l_s)
            acc_s[...] = jnp.zeros_like(acc_s)

        def step(masked):
            s = _dot(q_ref[...], k_ref[...], NT) * scale + dc_ref[0] - dr_ref[0]
            if masked:
                s = _causal(s, qs[p] * tq, ks[p] * tk)
            m_new = jnp.maximum(m_s[...], jnp.max(s, axis=1, keepdims=True))
            alpha = jnp.exp(m_s[...] - m_new)
            pr = jnp.exp(s - m_new)
            l_s[...] = alpha * l_s[...] + jnp.sum(pr, axis=1, keepdims=True)
            acc_s[...] = alpha * acc_s[...] + _dot(pr.astype(BF16), v_ref[...], NN)
            m_s[...] = m_new

        _on_diagonal(dg[p], step)

        @pl.when(ls[p] == 1)
        def _():
            o_ref[...] = (acc_s[...] / l_s[...]).astype(o_ref.dtype)
            lse_ref[0] = m_s[...] + jnp.log(l_s[...])

    hd = FOX_HEAD_DIM
    qcol = pl.BlockSpec((1, tq, 1), lambda h, p, qs, ks, *_: (h, qs[p], 0))
    grid_spec = pltpu.PrefetchScalarGridSpec(
        num_scalar_prefetch=5, grid=(nh, pairs[0].shape[0]),
        in_specs=[
            pl.BlockSpec((tq, hd), lambda h, p, qs, ks, *_: (qs[p], h)),
            pl.BlockSpec((tk, hd), lambda h, p, qs, ks, *_: (ks[p], nh + h)),
            pl.BlockSpec((tk, hd), lambda h, p, qs, ks, *_: (ks[p], 2 * nh + h)),
            qcol,
            pl.BlockSpec((1, 1, tk), lambda h, p, qs, ks, *_: (h, 0, ks[p])),
        ],
        out_specs=[pl.BlockSpec((tq, hd), lambda h, p, qs, ks, *_: (qs[p], h)), qcol],
        scratch_shapes=[pltpu.VMEM((tq, 1), F32), pltpu.VMEM((tq, 1), F32), pltpu.VMEM((tq, hd), F32)])
    return pl.pallas_call(
        body, name=name, grid_spec=grid_spec,
        out_shape=[jax.ShapeDtypeStruct((t, nh * hd), BF16), jax.ShapeDtypeStruct((nh, t, 1), F32)],
        compiler_params=_cparams(("parallel", "arbitrary")),
    )(*pairs, qkv, qkv, qkv, dcol, drow)


def fox_attn_dq(qkv, do, dcol, drow, lse_col, delta_col, name):
    t = qkv.shape[0]
    nh = dcol.shape[0]
    tq, tk = _fox_tiles(t)
    pairs = _fox_pairs(t, True)
    scale = FOX_HEAD_DIM ** -0.5

    def body(qs, ks, fs, ls, dg, q_ref, k_ref, v_ref, do_ref, dc_ref, dr_ref, lse_ref, dl_ref, dq_ref, db_ref, acc_s, db_s):
        p = pl.program_id(1)

        @pl.when(fs[p] == 1)
        def _():
            acc_s[...] = jnp.zeros_like(acc_s)
            db_s[...] = jnp.zeros_like(db_s)

        def step(masked):
            s = _dot(q_ref[...], k_ref[...], NT) * scale + dc_ref[0] - dr_ref[0]
            if masked:
                s = _causal(s, qs[p] * tq, ks[p] * tk)
            pr = jnp.exp(s - lse_ref[0])
            dp = _dot(do_ref[...], v_ref[...], NT)
            ds = pr * (dp - dl_ref[0])
            acc_s[...] += _dot(ds.astype(BF16), k_ref[...], NN)
            db_s[...] += jnp.sum(ds, axis=1, keepdims=True)

        _on_diagonal(dg[p], step)

        @pl.when(ls[p] == 1)
        def _():
            dq_ref[...] = (acc_s[...] * scale).astype(dq_ref.dtype)
            db_ref[0] = db_s[...]

    hd = FOX_HEAD_DIM
    qblk = pl.BlockSpec((tq, hd), lambda h, p, qs, ks, *_: (qs[p], h))
    qcol = pl.BlockSpec((1, tq, 1), lambda h, p, qs, ks, *_: (h, qs[p], 0))
    grid_spec = pltpu.PrefetchScalarGridSpec(
        num_scalar_prefetch=5, grid=(nh, pairs[0].shape[0]),
        in_specs=[
            qblk,
            pl.BlockSpec((tk, hd), lambda h, p, qs, ks, *_: (ks[p], nh + h)),
            pl.BlockSpec((tk, hd), lambda h, p, qs, ks, *_: (ks[p], 2 * nh + h)),
            qblk, qcol,
            pl.BlockSpec((1, 1, tk), lambda h, p, qs, ks, *_: (h, 0, ks[p])),
            qcol, qcol,
        ],
        out_specs=[qblk, qcol],
        scratch_shapes=[pltpu.VMEM((tq, hd), F32), pltpu.VMEM((tq, 1), F32)])
    return pl.pallas_call(
        body, name=name, grid_spec=grid_spec,
        out_shape=[jax.ShapeDtypeStruct((t, nh * hd), BF16), jax.ShapeDtypeStruct((nh, t, 1), F32)],
        compiler_params=_cparams(("parallel", "arbitrary")),
    )(*pairs, qkv, qkv, qkv, do, dcol, drow, lse_col, delta_col)


def fox_attn_dkv(qkv, do, dcol, drow, lse_row, delta_row, name):
    t = qkv.shape[0]
    nh = dcol.shape[0]
    tk, tq = _fox_tiles(t)
    pairs = _fox_pairs(t, False)
    scale = FOX_HEAD_DIM ** -0.5

    def body(ks, qs, fs, ls, dg, q_ref, k_ref, v_ref, do_ref, dc_ref, dr_ref, lse_ref, dl_ref, dk_ref, dv_ref, db_ref, dk_s, dv_s, db_s):
        p = pl.program_id(1)

        @pl.when(fs[p] == 1)
        def _():
            dk_s[...] = jnp.zeros_like(dk_s)
            dv_s[...] = jnp.zeros_like(dv_s)
            db_s[...] = jnp.zeros_like(db_s)

        def step(masked):
            st = _dot(k_ref[...], q_ref[...], NT) * scale + dr_ref[0] - dc_ref[0]
            if masked:
                st = _causal(st, ks[p] * tk, qs[p] * tq, transposed=True)
            pt = jnp.exp(st - lse_ref[0])
            dv_s[...] += _dot(pt.astype(BF16), do_ref[...], NN)
            dpt = _dot(v_ref[...], do_ref[...], NT)
            dst = pt * (dpt - dl_ref[0])
            dk_s[...] += _dot(dst.astype(BF16), q_ref[...], NN)
            db_s[...] -= jnp.sum(dst, axis=1, keepdims=True)

        _on_diagonal(dg[p], step)

        @pl.when(ls[p] == 1)
        def _():
            dk_ref[...] = (dk_s[...] * scale).astype(dk_ref.dtype)
            dv_ref[...] = dv_s[...].astype(dv_ref.dtype)
            db_ref[0] = db_s[...]

    hd = FOX_HEAD_DIM
    qblk = pl.BlockSpec((tq, hd), lambda h, p, ks, qs, *_: (qs[p], h))
    qrow = pl.BlockSpec((1, 1, tq), lambda h, p, ks, qs, *_: (h, 0, qs[p]))
    kcol = pl.BlockSpec((1, tk, 1), lambda h, p, ks, qs, *_: (h, ks[p], 0))
    kv_out = pl.BlockSpec((tk, hd), lambda h, p, ks, qs, *_: (ks[p], h))
    grid_spec = pltpu.PrefetchScalarGridSpec(
        num_scalar_prefetch=5, grid=(nh, pairs[0].shape[0]),
        in_specs=[
            qblk,
            pl.BlockSpec((tk, hd), lambda h, p, ks, qs, *_: (ks[p], nh + h)),
            pl.BlockSpec((tk, hd), lambda h, p, ks, qs, *_: (ks[p], 2 * nh + h)),
            qblk, kcol, qrow, qrow, qrow,
        ],
        out_specs=[kv_out, kv_out, kcol],
        scratch_shapes=[pltpu.VMEM((tk, hd), F32), pltpu.VMEM((tk, hd), F32), pltpu.VMEM((tk, 1), F32)])
    return pl.pallas_call(
        body, name=name, grid_spec=grid_spec,
        out_shape=[jax.ShapeDtypeStruct((t, nh * hd), BF16), jax.ShapeDtypeStruct((t, nh * hd), BF16),
                   jax.ShapeDtypeStruct((nh, t, 1), F32)],
        compiler_params=_cparams(("parallel", "arbitrary")),
    )(*pairs, qkv, qkv, qkv, do, dcol, drow, lse_row, delta_row)


def _head_selector(d, hd):
    return (jnp.arange(d)[:, None] // hd == jnp.arange(LANE)[None, :]).astype(F32)


def _log_sigmoid(x):
    return jnp.minimum(x, 0.0) - jnp.log1p(jnp.exp(-jnp.abs(x)))


def fox_fwd(h, norm_w, w_qkv, w_f, b_f, w_out, tag):
    t, d = h.shape
    nh = d // FOX_HEAD_DIM
    hn = norm_fwd(h, norm_w, f"{tag}_norm")
    qkv = matmul(hn, w_qkv, "nn", BF16, f"{tag}_qkv")
    fr = matmul(hn, w_f, "nn", F32, f"{tag}_f")
    logf = rowwise(lambda x, b: _log_sigmoid(x + b), f"{tag}_logf", [fr], [b_f], [(LANE, F32)])[0]
    dcum = cumsum_rows(logf, f"{tag}_cum").T[:nh]
    dcol, drow = dcum[:, :, None], dcum[:, None, :]
    o, lse = fox_attn_fwd(qkv, dcol, drow, f"{tag}_attn")
    out = matmul(o, w_out, "nn", F32, f"{tag}_out", residual=h)
    return out, (h, hn, qkv, fr, dcol, drow, o, lse)


def fox_bwd(dout, saved, norm_w, w_qkv, w_f, b_f, w_out, tag):
    h, hn, qkv, fr, dcol, drow, o, lse = saved
    t, d = h.shape
    nh = d // FOX_HEAD_DIM
    do = matmul(dout, w_out, "nt", BF16, f"{tag}_do")
    dw_out = matmul(o, dout, "tn", F32, f"{tag}_dwout")
    sel = _head_selector(d, FOX_HEAD_DIM)
    delta = rowwise(lambda a, b, s: _dot(a.astype(F32) * b.astype(F32), s, NN, HI), f"{tag}_delta",
                    [do, o], [sel], [(LANE, F32)])[0].T[:nh]
    dq, dbias_q = fox_attn_dq(qkv, do, dcol, drow, lse, delta[:, :, None], f"{tag}_dq")
    dk, dv, dbias_k = fox_attn_dkv(qkv, do, dcol, drow, jnp.swapaxes(lse, 1, 2), delta[:, None, :], f"{tag}_dkv")
    dlogf_q = cumsum_rows(jnp.pad(dbias_q[:, :, 0].T, ((0, 0), (0, LANE - nh))), f"{tag}_dcum_q", reverse=True)
    dlogf_k = cumsum_rows(jnp.pad(dbias_k[:, :, 0].T, ((0, 0), (0, LANE - nh))), f"{tag}_dcum_k", reverse=True)

    def dlogf_fn(gq, gk, x, b):
        r = (gq + gk) * _sigmoid(-(x + b))
        return r, jnp.sum(r, axis=0, keepdims=True)

    dfr, db_f = rowwise(dlogf_fn, f"{tag}_dlogf", [dlogf_q, dlogf_k, fr], [b_f], [(LANE, BF16)], [(1, LANE)])
    dqkv = jnp.concatenate([dq, dk, dv], axis=1)
    dhn = matmul(dfr, w_f, "nt", F32, f"{tag}_dhn_f")
    dhn = matmul(dqkv, w_qkv, "nt", F32, f"{tag}_dhn", residual=dhn)
    dw_qkv = matmul(hn, dqkv, "tn", F32, f"{tag}_dwqkv")
    dw_f = matmul(hn, dfr, "tn", F32, f"{tag}_dwf")
    dh, dnw = norm_bwd_res(h, norm_w, dhn, dout, f"{tag}_dnorm")
    return dh, dnw, dw_qkv, dw_f, db_f, dw_out


CONV_ROWS = 256


def _shift_rows(cur, halo, shift, up=False):
    if shift == 0:
        return cur
    n = cur.shape[0]
    row = lax.broadcasted_iota(jnp.int32, cur.shape, 0)
    if up:
        return jnp.where(row >= n - shift, pltpu.roll(halo, n - shift, 0), pltpu.roll(cur, n - shift, 0))
    return jnp.where(row < shift, pltpu.roll(halo, shift, 0), pltpu.roll(cur, shift, 0))


def _conv_pre(x, halo, w, b):
    acc = b + w[SSD_CONV - 1:SSD_CONV] * x
    for k in range(SSD_CONV - 1):
        acc = acc + w[k:k + 1] * _shift_rows(x, halo, SSD_CONV - 1 - k)
    return acc


def conv_fwd(x, w, b, name):
    t, cw = x.shape
    tb = min(CONV_ROWS, t)

    def body(x_ref, w_ref, b_ref, o_ref, halo):
        @pl.when(pl.program_id(0) == 0)
        def _():
            halo[...] = jnp.zeros_like(halo)

        xv = x_ref[...]
        o_ref[...] = _silu(_conv_pre(xv, halo[...], w_ref[...], b_ref[...]))
        halo[...] = xv

    blk = pl.BlockSpec((tb, cw), lambda i: (i, 0))
    return pl.pallas_call(
        body, name=name, grid=(t // tb,),
        in_specs=[blk, pl.BlockSpec(w.shape, lambda i: (0, 0)), pl.BlockSpec(b.shape, lambda i: (0, 0))],
        out_specs=blk, out_shape=jax.ShapeDtypeStruct((t, cw), F32), scratch_shapes=[pltpu.VMEM((tb, cw), F32)],
        compiler_params=_cparams(("arbitrary",)),
    )(x, w, b)


def conv_bwd(x, w, b, dact, name):
    t, cw = x.shape
    tb = min(CONV_ROWS, t)
    nb = t // tb

    def body_pre(x_ref, w_ref, b_ref, da_ref, dpre_ref, dw_ref, db_ref, halo):
        first = pl.program_id(0) == 0

        @pl.when(first)
        def _():
            halo[...] = jnp.zeros_like(halo)

        xv, hv = x_ref[...], halo[...]
        dpre = da_ref[...] * _dsilu(_conv_pre(xv, hv, w_ref[...], b_ref[...]))
        dpre_ref[...] = dpre
        dw = jnp.concatenate([jnp.sum(dpre * _shift_rows(xv, hv, SSD_CONV - 1 - k), axis=0, keepdims=True)
                              for k in range(SSD_CONV)], axis=0)
        db = jnp.sum(dpre, axis=0, keepdims=True)

        @pl.when(first)
        def _():
            dw_ref[...] = dw
            db_ref[...] = db

        @pl.when(jnp.logical_not(first))
        def _():
            dw_ref[...] += dw
            db_ref[...] += db

        halo[...] = xv

    blk = pl.BlockSpec((tb, cw), lambda i: (i, 0))
    wspec = pl.BlockSpec(w.shape, lambda i: (0, 0))
    bspec = pl.BlockSpec(b.shape, lambda i: (0, 0))
    dpre, dw, db = pl.pallas_call(
        body_pre, name=f"{name}_pre", grid=(nb,), in_specs=[blk, wspec, bspec, blk], out_specs=[blk, wspec, bspec],
        out_shape=[jax.ShapeDtypeStruct((t, cw), F32), jax.ShapeDtypeStruct(w.shape, F32), jax.ShapeDtypeStruct(b.shape, F32)],
        scratch_shapes=[pltpu.VMEM((tb, cw), F32)], compiler_params=_cparams(("arbitrary",)),
    )(x, w, b, dact)

    def body_dx(dp_ref, w_ref, dx_ref, halo):
        @pl.when(pl.program_id(0) == 0)
        def _():
            halo[...] = jnp.zeros_like(halo)

        dp, wv = dp_ref[...], w_ref[...]
        acc = wv[SSD_CONV - 1:SSD_CONV] * dp
        for k in range(SSD_CONV - 1):
            acc = acc + wv[k:k + 1] * _shift_rows(dp, halo[...], SSD_CONV - 1 - k, up=True)
        dx_ref[...] = acc.astype(dx_ref.dtype)
        halo[...] = dp

    rblk = pl.BlockSpec((tb, cw), lambda i: (nb - 1 - i, 0))
    dx = pl.pallas_call(
        body_dx, name=f"{name}_dx", grid=(nb,), in_specs=[rblk, wspec], out_specs=rblk,
        out_shape=jax.ShapeDtypeStruct((t, cw), BF16), scratch_shapes=[pltpu.VMEM((tb, cw), F32)],
        compiler_params=_cparams(("arbitrary",)),
    )(dpre, w)
    return dx, dw, db


def _tri(n, upper=False):
    r = lax.broadcasted_iota(jnp.int32, (n, n), 0)
    c = lax.broadcasted_iota(jnp.int32, (n, n), 1)
    return (c >= r) if upper else (r >= c)


def _ssd_decay(dtc, dtr, a):
    low = _tri(CHUNK)[None]
    cumc = jnp.sum(jnp.where(low, dtr * a, 0.0), axis=2, keepdims=True)
    cumr = jnp.sum(jnp.where(_tri(CHUNK, upper=True)[None], dtc * a, 0.0), axis=1, keepdims=True)
    return cumc, cumr


def _bdot(a, b, nt=False):
    dims = (((2,), (2,)), ((0,), (0,))) if nt else (((2,), (1,)), ((0,), (0,)))
    return lax.dot_general(a.astype(BF16), b.astype(BF16), dims, preferred_element_type=F32)


def _ssd_specs(d, hpg):
    l, n, p = CHUNK, SSD_STATE, SSD_HEAD_DIM
    ng = d // LANE
    x3 = pl.BlockSpec((hpg, l, p), lambda g, c: (g, c, 0))
    bsp = pl.BlockSpec((l, n), lambda g, c: (c, ng + g))
    csp = pl.BlockSpec((l, n), lambda g, c: (c, ng + SSD_GROUPS + g))
    dtc = pl.BlockSpec((hpg, l, 1), lambda g, c: (g, c, 0))
    dtr = pl.BlockSpec((hpg, 1, 1, l), lambda g, c: (g, c, 0, 0))
    per_head = pl.BlockSpec((hpg, 1, 1), lambda g, c: (g, 0, 0))
    return x3, bsp, csp, dtc, dtr, per_head


def ssd_intra_fwd(x3, xbc, dtc, dtr, a_log, d_skip, name):
    nh, t, p = x3.shape
    hpg = nh // SSD_GROUPS
    d = nh * p

    def body(x_ref, b_ref, c_ref, dtc_ref, dtr_ref, al_ref, ds_ref, y_ref):
        a = -jnp.exp(al_ref[...])
        cumc, cumr = _ssd_decay(dtc_ref[...], dtr_ref[:, 0], a)
        mdec = jnp.exp(jnp.where(_tri(CHUNK)[None], cumc - cumr, -jnp.inf))
        g = _dot(c_ref[...].astype(BF16), b_ref[...].astype(BF16), NT)
        xv = x_ref[...]
        y_ref[...] = _bdot(g[None] * mdec, xv * dtc_ref[...]) + xv * ds_ref[...]

    x3s, bsp, csp, dtcs, dtrs, ph = _ssd_specs(d, hpg)
    return pl.pallas_call(
        body, name=name, grid=(SSD_GROUPS, t // CHUNK), in_specs=[x3s, bsp, csp, dtcs, dtrs, ph, ph], out_specs=x3s,
        out_shape=jax.ShapeDtypeStruct((nh, t, p), F32), compiler_params=_cparams(("parallel", "parallel")),
    )(x3, xbc, xbc, dtc, dtr, a_log, d_skip)


def ssd_intra_bwd(x3, xbc, dtc, dtr, a_log, d_skip, dy3, name):
    nh, t, p = x3.shape
    hpg = nh // SSD_GROUPS
    d = nh * p
    l, n = CHUNK, SSD_STATE

    def body(x_ref, b_ref, c_ref, dtc_ref, dtr_ref, al_ref, ds_ref, dy_ref,
             dx_ref, ddt_ref, db_ref, dc_ref, dal_ref, dds_ref):
        first = pl.program_id(1) == 0
        a = -jnp.exp(al_ref[...])
        dtc_v = dtc_ref[...]
        cumc, cumr = _ssd_decay(dtc_v, dtr_ref[:, 0], a)
        low = _tri(l)[None]
        mdec = jnp.exp(jnp.where(low, cumc - cumr, -jnp.inf))
        up = _tri(l, upper=True)[None]
        mdec_t = jnp.exp(jnp.where(up, cumr - cumc, -jnp.inf))
        bv, cv = b_ref[...].astype(BF16), c_ref[...].astype(BF16)
        g = _dot(cv, bv, NT)
        g_t = _dot(bv, cv, NT)
        xv, dy = x_ref[...], dy_ref[...]
        xd = xv * dtc_v
        dw = _bdot(dy, xd, nt=True)
        dw_t = _bdot(xd, dy, nt=True)
        dxd = _bdot(g_t[None] * mdec_t, dy)
        dx_ref[...] = dy * ds_ref[...] + dxd * dtc_v
        dg = jnp.sum(dw * mdec, axis=0)
        dg_t = jnp.sum(dw_t * mdec_t, axis=0)
        dc_ref[0] = _dot(dg.astype(BF16), bv, NN)
        db_ref[0] = _dot(dg_t.astype(BF16), cv, NN)
        e = dw * mdec * g[None]
        e_t = dw_t * mdec_t * g_t[None]
        dcum_r = jnp.sum(e_t, axis=1, keepdims=True) - jnp.sum(e, axis=1, keepdims=True)
        dda = jnp.sum(jnp.where(up, dcum_r, 0.0), axis=2, keepdims=True)
        ddt_ref[...] = jnp.sum(dxd * xv, axis=2, keepdims=True) + dda * a
        dal = jnp.sum(dda * dtc_v, axis=1, keepdims=True) * a
        dds = jnp.sum(jnp.sum(dy * xv, axis=2, keepdims=True), axis=1, keepdims=True)

        @pl.when(first)
        def _():
            dal_ref[...] = dal
            dds_ref[...] = dds

        @pl.when(jnp.logical_not(first))
        def _():
            dal_ref[...] += dal
            dds_ref[...] += dds

    x3s, bsp, csp, dtcs, dtrs, ph = _ssd_specs(d, hpg)
    grp = pl.BlockSpec((1, l, n), lambda g, c: (g, c, 0))
    return pl.pallas_call(
        body, name=name, grid=(SSD_GROUPS, t // l),
        in_specs=[x3s, bsp, csp, dtcs, dtrs, ph, ph, x3s], out_specs=[x3s, dtcs, grp, grp, ph, ph],
        out_shape=[jax.ShapeDtypeStruct((nh, t, p), F32), jax.ShapeDtypeStruct((nh, t, 1), F32),
                   jax.ShapeDtypeStruct((SSD_GROUPS, t, n), F32), jax.ShapeDtypeStruct((SSD_GROUPS, t, n), F32),
                   jax.ShapeDtypeStruct((nh, 1, 1), F32), jax.ShapeDtypeStruct((nh, 1, 1), F32)],
        compiler_params=_cparams(("parallel", "arbitrary")),
    )(x3, xbc, xbc, dtc, dtr, a_log, d_skip, dy3)


def _ssd_state_common(x, dt, a_lane):
    l = CHUNK
    da = dt * a_lane
    cum = _dot(_tri(l).astype(F32), da, NN, HI)
    cend = cum[l - 1:l]
    ec = jnp.exp(cum)
    te = jnp.exp(cend - cum)
    cd_col = jnp.exp(_dot(da, jnp.ones((l, SSD_STATE), F32), TN, HI))
    return cum, cend, ec, te, cd_col


def ssd_state_fwd(xbc, dt_lane, a_lane, name):
    t = xbc.shape[0]
    d = dt_lane.shape[1]
    gw = d // SSD_GROUPS
    l, n = CHUNK, SSD_STATE
    nc = t // l
    ng = d // LANE

    def body(x_ref, b_ref, c_ref, dt_ref, a_ref, y_ref, sp_ref, s_s):
        @pl.when(pl.program_id(1) == 0)
        def _():
            s_s[...] = jnp.zeros_like(s_s)

        s_prev = s_s[...]
        for j in range(cb):
            rows = slice(j * l, (j + 1) * l)
            xv, dt = x_ref[rows, :], dt_ref[rows, :]
            cum, cend, ec, te, cd_col = _ssd_state_common(xv, dt, a_ref[...])
            sp_ref[j] = s_prev.astype(BF16)
            y_ref[rows, :] = _dot(c_ref[rows, :].astype(BF16), s_prev.astype(BF16), NT) * ec
            xt = (xv * dt * te).astype(BF16)
            s_prev = s_prev * cd_col + _dot(xt, b_ref[rows, :].astype(BF16), TN)
        s_s[...] = s_prev

    cb = _pick(nc, SCAN_CHUNKS)
    xs = pl.BlockSpec((cb * l, gw), lambda g, c: (c, g))
    return pl.pallas_call(
        body, name=name, grid=(SSD_GROUPS, nc // cb),
        in_specs=[xs, pl.BlockSpec((cb * l, n), lambda g, c: (c, ng + g)),
                  pl.BlockSpec((cb * l, n), lambda g, c: (c, ng + SSD_GROUPS + g)),
                  xs, pl.BlockSpec((1, gw), lambda g, c: (0, g))],
        out_specs=[xs, pl.BlockSpec((cb, gw, n), lambda g, c: (c, g, 0))],
        out_shape=[jax.ShapeDtypeStruct((t, d), F32), jax.ShapeDtypeStruct((nc, d, n), BF16)],
        scratch_shapes=[pltpu.VMEM((gw, n), F32)],
        compiler_params=_cparams(("parallel", "arbitrary")),
    )(xbc, xbc, xbc, dt_lane, a_lane)


def ssd_state_bwd(xbc, dt_lane, a_lane, s_prev_all, dy, name):
    t = xbc.shape[0]
    d = dt_lane.shape[1]
    gw = d // SSD_GROUPS
    l, n = CHUNK, SSD_STATE
    nc = t // l
    ng = d // LANE

    def body(x_ref, b_ref, c_ref, dt_ref, a_ref, sp_ref, dy_ref, dx_ref, ddt_ref, db_ref, dc_ref, da_ref, ds_s):
        first = pl.program_id(1) == 0

        @pl.when(first)
        def _():
            ds_s[...] = jnp.zeros_like(ds_s)

        a_lane_v = a_ref[...]
        ds_next = ds_s[...]
        da = jnp.zeros_like(a_lane_v)
        last = lax.broadcasted_iota(jnp.int32, (l, gw), 0) == l - 1
        for j in reversed(range(cb)):
            rows = slice(j * l, (j + 1) * l)
            xv, dt = x_ref[rows, :], dt_ref[rows, :]
            cum, cend, ec, te, cd_col = _ssd_state_common(xv, dt, a_lane_v)
            bv, cv = b_ref[rows, :].astype(BF16), c_ref[rows, :].astype(BF16)
            s_prev = sp_ref[j]
            dyv = dy_ref[rows, :]
            z = _dot(cv, s_prev, NT)
            dz = (dyv * ec).astype(BF16)
            dc_ref[rows, :] = _dot(dz, s_prev, NN)
            xd = xv * dt
            dxt = _dot(bv, ds_next.astype(BF16), NT)
            db_ref[rows, :] = _dot((xd * te).astype(BF16), ds_next.astype(BF16), NN)
            dcd = _dot(jnp.ones((8, n), F32), ds_next * s_prev.astype(F32), NT, HI)[0:1]
            dte_te = dxt * xd * te
            dcum = dyv * z * ec - dte_te + jnp.where(last, jnp.sum(dte_te, axis=0, keepdims=True) + dcd * jnp.exp(cend), 0.0)
            dda = _dot(_tri(l, upper=True).astype(F32), dcum, NN, HI)
            dxd = dxt * te
            dx_ref[rows, :] = dxd * dt
            ddt_ref[rows, :] = dxd * xv + dda * a_lane_v
            da = da + jnp.sum(dda * dt, axis=0, keepdims=True)
            ds_next = ds_next * cd_col + _dot(dz, cv, TN)
        ds_s[...] = ds_next

        @pl.when(first)
        def _():
            da_ref[...] = da

        @pl.when(jnp.logical_not(first))
        def _():
            da_ref[...] += da

    cb = _pick(nc, SCAN_CHUNKS)
    nb = nc // cb
    rc = lambda c: nb - 1 - c
    xs = pl.BlockSpec((cb * l, gw), lambda g, c: (rc(c), g))
    gs = pl.BlockSpec((cb * l, n), lambda g, c: (rc(c), g))
    return pl.pallas_call(
        body, name=name, grid=(SSD_GROUPS, nb),
        in_specs=[xs, pl.BlockSpec((cb * l, n), lambda g, c: (rc(c), ng + g)),
                  pl.BlockSpec((cb * l, n), lambda g, c: (rc(c), ng + SSD_GROUPS + g)),
                  xs, pl.BlockSpec((1, gw), lambda g, c: (0, g)),
                  pl.BlockSpec((cb, gw, n), lambda g, c: (rc(c), g, 0)), xs],
        out_specs=[xs, xs, gs, gs, pl.BlockSpec((1, gw), lambda g, c: (0, g))],
        out_shape=[jax.ShapeDtypeStruct((t, d), F32), jax.ShapeDtypeStruct((t, d), F32),
                   jax.ShapeDtypeStruct((t, SSD_GROUPS * n), F32), jax.ShapeDtypeStruct((t, SSD_GROUPS * n), F32),
                   jax.ShapeDtypeStruct((1, d), F32)],
        scratch_shapes=[pltpu.VMEM((gw, n), F32)],
        compiler_params=_cparams(("parallel", "arbitrary")),
    )(xbc, xbc, xbc, dt_lane, a_lane, s_prev_all, dy)


SCAN_CHUNKS = (4, 2, 1)


def _hgrn_common(q, fr, lb):
    l = CHUNK
    sig = _sigmoid(fr)
    f = lb + (1.0 - lb) * sig
    kk = 1.0 - f
    lf = jnp.log(f)
    cum = _dot(_tri(l).astype(F32), lf, NN, HI)
    mid = cum[l // 2 - 1:l // 2]
    cend = cum[l - 1:l]
    qf = _silu(q)
    eq, ek, ee, ec = jnp.exp(cum - mid), jnp.exp(mid - cum), jnp.exp(cend - cum), jnp.exp(cum)
    cd_col = jnp.exp(_dot(lf, jnp.ones((l, HGRN_KDIM), F32), TN, HI))
    return sig, f, kk, qf, eq, ek, ee, ec, cend, cd_col


def hgrn_fwd(qfvg, lb, name):
    t = qfvg.shape[0]
    d = lb.shape[1]
    l, kd = CHUNK, HGRN_KDIM
    nh = d // kd
    nc = t // l

    cb = _pick(nc, SCAN_CHUNKS)

    def body(q_ref, f_ref, v_ref, lb_ref, o_ref, sp_ref, s_s):
        @pl.when(pl.program_id(1) == 0)
        def _():
            s_s[...] = jnp.zeros_like(s_s)

        s_prev = s_s[...]
        for j in range(cb):
            rows = slice(j * l, (j + 1) * l)
            sig, f, kk, qf, eq, ek, ee, ec, cend, cd_col = _hgrn_common(q_ref[rows, :], f_ref[rows, :], lb_ref[...])
            v = v_ref[rows, :].astype(BF16)
            sp_ref[j] = s_prev.astype(BF16)
            att = jnp.where(_tri(l), _dot((qf * eq).astype(BF16), (kk * ek).astype(BF16), NT), 0.0)
            o_ref[rows, :] = _dot(att.astype(BF16), v, NN) + _dot((qf * ec).astype(BF16), s_prev.astype(BF16), NN)
            s_prev = s_prev * cd_col + _dot((kk * ee).astype(BF16), v, TN)
        s_s[...] = s_prev

    def col(j):
        return pl.BlockSpec((cb * l, kd), lambda h, c: (c, j * nh + h))

    return pl.pallas_call(
        body, name=name, grid=(nh, nc // cb),
        in_specs=[col(0), col(1), col(2), pl.BlockSpec((1, kd), lambda h, c: (0, h))],
        out_specs=[col(0), pl.BlockSpec((cb, kd, kd), lambda h, c: (c, h, 0))],
        out_shape=[jax.ShapeDtypeStruct((t, d), F32), jax.ShapeDtypeStruct((nc, d, kd), BF16)],
        scratch_shapes=[pltpu.VMEM((kd, kd), F32)],
        compiler_params=_cparams(("parallel", "arbitrary")),
    )(qfvg, qfvg, qfvg, lb)


def hgrn_bwd(qfvg, lb, s_prev_all, do, name):
    t = qfvg.shape[0]
    d = lb.shape[1]
    l, kd = CHUNK, HGRN_KDIM
    nh = d // kd
    nc = t // l
    cb = _pick(nc, SCAN_CHUNKS)
    nb = nc // cb

    def body(q_ref, f_ref, v_ref, lb_ref, sp_ref, do_ref, dq_ref, df_ref, dv_ref, dlb_ref, ds_s):
        first = pl.program_id(1) == 0

        @pl.when(first)
        def _():
            ds_s[...] = jnp.zeros_like(ds_s)

        lbv = lb_ref[...]
        ds_next = ds_s[...]
        dlb = jnp.zeros_like(lbv)
        low = _tri(l)
        row = lax.broadcasted_iota(jnp.int32, (l, kd), 0)
        for j in reversed(range(cb)):
            rows = slice(j * l, (j + 1) * l)
            q = q_ref[rows, :]
            sig, f, kk, qf, eq, ek, ee, ec, cend, cd_col = _hgrn_common(q, f_ref[rows, :], lbv)
            v = v_ref[rows, :].astype(BF16)
            dov = do_ref[rows, :].astype(BF16)
            s_prev = sp_ref[j]
            ds_b = ds_next.astype(BF16)
            qr, kr, ke, qe = qf * eq, kk * ek, kk * ee, qf * ec
            att = jnp.where(low, _dot(qr.astype(BF16), kr.astype(BF16), NT), 0.0).astype(BF16)
            datt = jnp.where(low, _dot(dov, v, NT), 0.0).astype(BF16)
            dqe = _dot(dov, s_prev, NT)
            dke = _dot(v, ds_b, NT)
            dqr = _dot(datt, kr.astype(BF16), NN)
            dkr = _dot(datt, qr.astype(BF16), TN)
            dv_ref[rows, :] = (_dot(ke.astype(BF16), ds_b, NN) + _dot(att, dov, TN)).astype(dv_ref.dtype)
            dcd = _dot(jnp.ones((8, kd), F32), ds_next * s_prev.astype(F32), NT, HI)[0:1]
            a_q, a_k, a_e, a_c = dqr * qr, dkr * kr, dke * ke, dqe * qe
            dmid = jnp.sum(a_k - a_q, axis=0, keepdims=True)
            dcend = jnp.sum(a_e, axis=0, keepdims=True) + dcd * jnp.exp(cend)
            dcum = a_q - a_k - a_e + a_c + jnp.where(row == l // 2 - 1, dmid, 0.0) + jnp.where(row == l - 1, dcend, 0.0)
            dlf = _dot(_tri(l, upper=True).astype(F32), dcum, NN, HI)
            df = dlf / f - (dkr * ek + dke * ee)
            df_ref[rows, :] = (df * (1.0 - lbv) * sig * (1.0 - sig)).astype(df_ref.dtype)
            dq_ref[rows, :] = ((dqr * eq + dqe * ec) * _dsilu(q)).astype(dq_ref.dtype)
            dlb = dlb + jnp.sum(df * (1.0 - sig), axis=0, keepdims=True)
            ds_next = ds_next * cd_col + _dot(qe.astype(BF16), dov, TN)
        ds_s[...] = ds_next

        @pl.when(first)
        def _():
            dlb_ref[...] = dlb

        @pl.when(jnp.logical_not(first))
        def _():
            dlb_ref[...] += dlb

    def col(j):
        return pl.BlockSpec((cb * l, kd), lambda h, c: (nb - 1 - c, j * nh + h))

    head = pl.BlockSpec((1, kd), lambda h, c: (0, h))
    return pl.pallas_call(
        body, name=name, grid=(nh, nb),
        in_specs=[col(0), col(1), col(2), head, pl.BlockSpec((cb, kd, kd), lambda h, c: (nb - 1 - c, h, 0)), col(0)],
        out_specs=[col(0), col(0), col(0), head],
        out_shape=[jax.ShapeDtypeStruct((t, d), BF16)] * 3 + [jax.ShapeDtypeStruct((1, d), F32)],
        scratch_shapes=[pltpu.VMEM((kd, kd), F32)],
        compiler_params=_cparams(("parallel", "arbitrary")),
    )(qfvg, qfvg, qfvg, lb, s_prev_all, do)


def _softplus(x):
    return jnp.maximum(x, 0.0) + jnp.log1p(jnp.exp(-jnp.abs(x)))


def _grouped(fn, width, *arrs):
    n = arrs[0].shape[1] // width
    outs = [fn(*[a[:, i * width:(i + 1) * width] for a in arrs]) for i in range(n)]
    if isinstance(outs[0], tuple):
        return tuple(jnp.concatenate([o[j] for o in outs], axis=1) for j in range(len(outs[0])))
    return jnp.concatenate(outs, axis=1)


def mixer_fwd(h, norm_w, wts, conv_w, conv_b, dt_bias, a_log, d_skip, ssd_norm_w, lb, hgrn_norm_w, w_out, tag):
    t, d = h.shape
    nh = d // SSD_HEAD_DIM
    p = SSD_HEAD_DIM
    w_z, w_xbc, w_qfvg, w_dt = wts
    hn = norm_fwd(h, norm_w, f"{tag}_norm")
    z = matmul(hn, w_z, "nn", F32, f"{tag}_z")
    xbc_raw = matmul(hn, w_xbc, "nn", F32, f"{tag}_xbc")
    qfvg = matmul(hn, w_qfvg, "nn", F32, f"{tag}_qfvg")
    dt_raw = matmul(hn, w_dt, "nn", F32, f"{tag}_dt")
    xbc = conv_fwd(xbc_raw, conv_w, conv_b, f"{tag}_conv")
    dt = rowwise(lambda x, b: _softplus(x + b), f"{tag}_softplus", [dt_raw], [dt_bias], [(LANE, F32)])[0]
    dt_h = dt[:, :nh]
    dtr = dt_h.T.reshape(nh, t // CHUNK, 1, CHUNK)
    dtc = dt_h.T[:, :, None]
    dt_lane = jnp.repeat(dt_h, p, axis=1)
    a_lane = jnp.repeat(-jnp.exp(a_log[:, :nh]), p, axis=1)
    al3 = a_log[0, :nh].reshape(nh, 1, 1)
    ds3 = d_skip.reshape(nh, 1, 1)
    x3 = xbc[:, :d].reshape(t, nh, p).transpose(1, 0, 2)
    y3 = ssd_intra_fwd(x3, xbc, dtc, dtr, al3, ds3, f"{tag}_ssd_intra")
    y_off, s_ssd = ssd_state_fwd(xbc, dt_lane, a_lane, f"{tag}_ssd_state")
    y_diag = y3.transpose(1, 0, 2).reshape(t, d)
    o_b, s_hgrn = hgrn_fwd(qfvg, lb, f"{tag}_hgrn")
    gw = d // SSD_GROUPS

    def gate(yd, yo, z_, o, g_, nw_a, nw_b):
        ya = (yd + yo) * _silu(z_)
        ya = _grouped(lambda a, w: a * _rms(a) * w, gw, ya, nw_a)
        yb = _grouped(lambda a, w: a * _rms(a) * w, HGRN_KDIM, o, nw_b) * _silu(g_)
        return jnp.concatenate([ya, yb], axis=1)

    cat = rowwise(gate, f"{tag}_gate", [y_diag, y_off, z, o_b, (qfvg, d, 3)], [ssd_norm_w, hgrn_norm_w], [(2 * d, BF16)], tm=128)[0]
    out = matmul(cat, w_out, "nn", F32, f"{tag}_out", residual=h)
    saved = (h, hn, z, xbc_raw, qfvg, dt_raw, xbc, dtc, dtr, dt_lane, a_lane, al3, ds3, x3, y_diag, y_off, s_ssd, o_b, s_hgrn, cat)
    return out, saved


def mixer_bwd(dout, saved, norm_w, wts, conv_w, conv_b, dt_bias, a_log, ssd_norm_w, lb, hgrn_norm_w, w_out, tag):
    (h, hn, z, xbc_raw, qfvg, dt_raw, xbc, dtc, dtr, dt_lane, a_lane, al3, ds3, x3, y_diag, y_off, s_ssd, o_b, s_hgrn, cat) = saved
    t, d = h.shape
    nh = d // SSD_HEAD_DIM
    p = SSD_HEAD_DIM
    gw = d // SSD_GROUPS
    w_z, w_xbc, w_qfvg, w_dt = wts
    dcat = matmul(dout, w_out, "nt", F32, f"{tag}_dcat")
    dw_out = matmul(cat, dout, "tn", F32, f"{tag}_dwout")

    def gate_bwd(dya_n, dyb_g, yd, yo, z_, o, g_, nw_a, nw_b):
        y = yd + yo
        sz = _silu(z_)
        dya, dnw_a = _grouped(lambda a, w, dy: _norm_bwd(a, w, dy), gw, y * sz, nw_a, dya_n)
        sg = _silu(g_)
        tb = _grouped(lambda a, w: a * _rms(a) * w, HGRN_KDIM, o, nw_b)
        do_, dnw_b = _grouped(lambda a, w, dy: _norm_bwd(a, w, dy), HGRN_KDIM, o, nw_b, dyb_g * sg)
        return dya * sz, dya * y * _dsilu(z_), do_, dyb_g * tb * _dsilu(g_), dnw_a, dnw_b

    dy, dz, do_b, dg, dnw_a, dnw_b = rowwise(
        gate_bwd, f"{tag}_dgate", [(dcat, d, 0), (dcat, d, 1), y_diag, y_off, z, o_b, (qfvg, d, 3)],
        [ssd_norm_w, hgrn_norm_w], [(d, F32), (d, BF16), (d, F32), (d, BF16)], [(1, d), (1, d)], tm=128)
    dq, dfr, dv, dlb = hgrn_bwd(qfvg, lb, s_hgrn, do_b, f"{tag}_dhgrn")
    dqfvg = jnp.concatenate([dq, dfr, dv, dg], axis=1)
    dy3 = dy.reshape(t, nh, p).transpose(1, 0, 2)
    dx3, ddt3, db_a, dc_a, dal_a, dds = ssd_intra_bwd(x3, xbc, dtc, dtr, al3, ds3, dy3, f"{tag}_dssd_intra")
    dx_s, ddt_lane, db_s, dc_s, da_lane = ssd_state_bwd(xbc, dt_lane, a_lane, s_ssd, dy, f"{tag}_dssd_state")
    n = SSD_STATE
    dxbc_act_parts = (dx3.transpose(1, 0, 2).reshape(t, d), dx_s,
                      db_a.transpose(1, 0, 2).reshape(t, SSD_GROUPS * n), db_s,
                      dc_a.transpose(1, 0, 2).reshape(t, SSD_GROUPS * n), dc_s)
    sel = _head_selector(d, p)
    ddt_a = jnp.pad(ddt3[:, :, 0].T, ((0, 0), (0, LANE - nh)))

    def dt_bwd(ddl, dda, x, b, s):
        r = (_dot(ddl, s, NN, HI) + dda) * _sigmoid(x + b)
        return r, jnp.sum(r, axis=0, keepdims=True)

    ddt_raw, ddt_bias = rowwise(dt_bwd, f"{tag}_ddt", [ddt_lane, ddt_a, dt_raw], [dt_bias, sel], [(LANE, BF16)], [(1, LANE)])
    a_pad = -jnp.exp(a_log)
    dal_a_row = jnp.pad(dal_a.reshape(1, nh), ((0, 0), (0, LANE - nh)))
    dalog = rowwise(lambda dal, da, a, s: dal + _dot(da, s, NN, HI) * a, f"{tag}_dalog",
                    [dal_a_row, da_lane, a_pad], [sel], [(LANE, F32)])[0]
    dxbc_act = rowwise(lambda x1, x2, b1, b2, c1, c2: jnp.concatenate([x1 + x2, b1 + b2, c1 + c2], axis=1),
                       f"{tag}_dxbc_sum", list(dxbc_act_parts), [], [(d + 2 * SSD_GROUPS * n, F32)], tm=128)[0]
    dxbc_raw, dconv_w, dconv_b = conv_bwd(xbc_raw, conv_w, conv_b, dxbc_act, f"{tag}_dconv")
    dhn = matmul(dz, w_z, "nt", F32, f"{tag}_dhn_z")
    dhn = matmul(dxbc_raw, w_xbc, "nt", F32, f"{tag}_dhn_xbc", residual=dhn)
    dhn = matmul(dqfvg, w_qfvg, "nt", F32, f"{tag}_dhn_qfvg", residual=dhn)
    dhn = matmul(ddt_raw, w_dt, "nt", F32, f"{tag}_dhn_dt", residual=dhn)
    dw_z = matmul(hn, dz, "tn", F32, f"{tag}_dwz")
    dw_xbc = matmul(hn, dxbc_raw, "tn", F32, f"{tag}_dwxbc")
    dw_qfvg = matmul(hn, dqfvg, "tn", F32, f"{tag}_dwqfvg")
    dw_dt = matmul(hn, ddt_raw, "tn", F32, f"{tag}_dwdt")
    dh, dnw = norm_bwd_res(h, norm_w, dhn, dout, f"{tag}_dnorm")
    grads = dict(mix_norm=dnw, w_z=dw_z, w_xbc=dw_xbc, w_qfvg=dw_qfvg, w_dt=dw_dt, conv_w=dconv_w, conv_b=dconv_b,
                 dt_bias=ddt_bias, a_log=dalog, d_skip=dds.reshape(1, nh), ssd_norm=dnw_a, lb=dlb, hgrn_norm=dnw_b,
                 w_out=dw_out)
    return dh, grads


ANY = pl.BlockSpec(memory_space=pl.ANY)


def _place():
    x, y, c = lax.axis_index("x"), lax.axis_index("y"), lax.axis_index("c")
    chips = [(1 - x, y), (x, 1 - y), (1 - x, 1 - y)]
    return x, y, c, chips


def _rcopy(src, dst, send_sems, recv_sems, k, dev):
    return pltpu.make_async_remote_copy(src_ref=src, dst_ref=dst, send_sem=send_sems.at[k], recv_sem=recv_sems.at[k],
                                        device_id=dev, device_id_type=MESH_ID)


def gather_weights(wsh, name):
    def body(w_ref, o_ref, send_sems, recv_sems):
        x, y, c, chips = _place()
        me = 2 * x + y
        sib = (x, y, 1 - c)
        first = [_rcopy(w_ref.at[c], o_ref.at[c, me], send_sems, recv_sems, j, (cx, cy, c)) for j, (cx, cy) in enumerate(chips)]
        for cp in first:
            cp.start()
        passed = []
        for j, (cx, cy) in enumerate(chips):
            blk = o_ref.at[c, 2 * cx + cy]
            _rcopy(blk, blk, send_sems, recv_sems, j, sib).wait_recv()
            cp = _rcopy(blk, blk, send_sems, recv_sems, 3 + j, sib)
            cp.start()
            passed.append(cp)
        for j, (cx, cy) in enumerate(chips):
            blk = o_ref.at[1 - c, 2 * cx + cy]
            _rcopy(blk, blk, send_sems, recv_sems, 3 + j, sib).wait_recv()
        for cp in first + passed:
            cp.wait_send()

    return pl.pallas_call(
        body, name=name, in_specs=[ANY], out_specs=ANY,
        out_shape=jax.ShapeDtypeStruct((2, 4) + wsh.shape[1:], wsh.dtype),
        scratch_shapes=[pltpu.SemaphoreType.DMA((6,)), pltpu.SemaphoreType.DMA((6,))],
    )(wsh)


def gather_filled(wsh, chip, name):
    return lax.dynamic_update_slice(gather_weights(wsh, name), wsh[:, None], (0, chip) + (0,) * (wsh.ndim - 1))


def exchange_halves(g0, g1, name):
    def body(g0_ref, g1_ref, o_ref, send_sems, recv_sems):
        x, y, c, _ = _place()
        for mine_c, src in ((0, g1_ref), (1, g0_ref)):
            @pl.when(c == mine_c)
            def _():
                cp = _rcopy(src, o_ref, send_sems, recv_sems, 0, (x, y, 1 - c))
                cp.start()
                cp.wait()

    return pl.pallas_call(
        body, name=name, in_specs=[ANY, ANY], out_specs=ANY, out_shape=jax.ShapeDtypeStruct(g0.shape, g0.dtype),
        scratch_shapes=[pltpu.SemaphoreType.DMA((1,)), pltpu.SemaphoreType.DMA((1,))],
    )(g0, g1)


def _rs_tile(r, c):
    if r % 256 == 0:
        return 256, c
    return r, _pick(c, (512, 256, 128))


def add_own_half(g0, g1, other, c_idx, name):
    nchip, r, cdim = g0.shape
    tr, tc = _rs_tile(r, cdim)

    def body(c_ref, a0_ref, a1_ref, b_ref, o_ref):
        own = jnp.where(c_ref[0] == 0, a0_ref[...], a1_ref[...])
        o_ref[...] = (own + b_ref[...]).astype(o_ref.dtype)

    blk = pl.BlockSpec((None, tr, tc), lambda k, i, j, c_ref: (k, i, j))
    grid_spec = pltpu.PrefetchScalarGridSpec(
        num_scalar_prefetch=1, grid=(nchip, r // tr, cdim // tc), in_specs=[blk, blk, blk], out_specs=blk)
    return pl.pallas_call(
        body, name=name, grid_spec=grid_spec, out_shape=jax.ShapeDtypeStruct((nchip, r, cdim), BF16),
        compiler_params=_cparams(("parallel", "parallel", "parallel")),
    )(c_idx, g0, g1, other)


def scatter_to_chips(part, name):
    def body(p_ref, o_ref, send_sems, recv_sems, local_sem):
        x, y, c, chips = _place()
        me = 2 * x + y
        mine = pltpu.make_async_copy(p_ref.at[me], o_ref.at[me], local_sem)
        mine.start()
        cps = [_rcopy(p_ref.at[2 * cx + cy], o_ref.at[me], send_sems, recv_sems, j, (cx, cy, c)) for j, (cx, cy) in enumerate(chips)]
        for cp in cps:
            cp.start()
        for j, (cx, cy) in enumerate(chips):
            blk = o_ref.at[2 * cx + cy]
            _rcopy(blk, blk, send_sems, recv_sems, j, (cx, cy, c)).wait_recv()
        for cp in cps:
            cp.wait_send()
        mine.wait()

    return pl.pallas_call(
        body, name=name, in_specs=[ANY], out_specs=ANY, out_shape=jax.ShapeDtypeStruct(part.shape, part.dtype),
        scratch_shapes=[pltpu.SemaphoreType.DMA((3,)), pltpu.SemaphoreType.DMA((3,)), pltpu.SemaphoreType.DMA],
    )(part)


def sum_chips(slots, name):
    nchip, r, cdim = slots.shape
    tr, tc = _rs_tile(r, cdim)

    def body(s_ref, o_ref):
        acc = s_ref[0].astype(F32)
        for k in range(1, nchip):
            acc = acc + s_ref[k].astype(F32)
        o_ref[...] = acc

    return pl.pallas_call(
        body, name=name, grid=(r // tr, cdim // tc), in_specs=[pl.BlockSpec((nchip, tr, tc), lambda i, j: (0, i, j))],
        out_specs=pl.BlockSpec((tr, tc), lambda i, j: (i, j)), out_shape=jax.ShapeDtypeStruct((r, cdim), F32),
        compiler_params=_cparams(("parallel", "parallel")),
    )(slots)


def share_with_sibling(half, name):
    def body(h_ref, o_ref, send_sems, recv_sems):
        x, y, c, _ = _place()
        cp = _rcopy(h_ref, o_ref.at[c], send_sems, recv_sems, 0, (x, y, 1 - c))
        cp.start()
        blk = o_ref.at[1 - c]
        _rcopy(blk, blk, send_sems, recv_sems, 0, (x, y, 1 - c)).wait_recv()
        cp.wait_send()

    return pl.pallas_call(
        body, name=name, in_specs=[ANY], out_specs=ANY,
        out_shape=jax.ShapeDtypeStruct((2,) + half.shape, half.dtype),
        scratch_shapes=[pltpu.SemaphoreType.DMA((1,)), pltpu.SemaphoreType.DMA((1,))],
    )(half)


def reduce_scatter_grads(g0, g1, c_idx, tag):
    other = exchange_halves(g0, g1, f"rs_exchange_{tag}")
    part = add_own_half(g0, g1, other, c_idx, f"rs_add_{tag}")
    slots = scatter_to_chips(part, f"rs_scatter_{tag}")
    half = sum_chips(slots, f"rs_sum_{tag}")
    return lax.dynamic_update_slice(share_with_sibling(half, f"rs_share_{tag}"), half[None], (c_idx[0], 0, 0))


def allreduce_small(v, name):
    rows, w = v.shape

    def body(v_ref, o_ref, slots, send_sems, recv_sems):
        x, y, c, _ = _place()
        me = 4 * x + 2 * y + c
        slots[me] = v_ref[...]
        cps = []
        for r in range(1, 8):
            dev = (x ^ (r >> 2), y ^ ((r >> 1) & 1), c ^ (r & 1))
            cp = _rcopy(v_ref, slots.at[me], send_sems, recv_sems, r - 1, dev)
            cp.start()
            cps.append(cp)
        for r in range(1, 8):
            blk = slots.at[me ^ r]
            _rcopy(blk, blk, send_sems, recv_sems, r - 1, (x, y, c)).wait_recv()
        for cp in cps:
            cp.wait_send()
        acc = slots[0]
        for k in range(1, 8):
            acc = acc + slots[k]
        o_ref[...] = acc

    vm = pl.BlockSpec(memory_space=pltpu.VMEM)
    return pl.pallas_call(
        body, name=name, in_specs=[vm], out_specs=vm, out_shape=jax.ShapeDtypeStruct((rows, w), F32),
        scratch_shapes=[pltpu.VMEM((8, rows, w), F32), pltpu.SemaphoreType.DMA((7,)), pltpu.SemaphoreType.DMA((7,))],
    )(v)


def adamw(w, g, m, v, name):
    def fn(w_, g_, m_, v_):
        m2 = ADAM_B1 * m_ + (1.0 - ADAM_B1) * g_
        v2 = ADAM_B2 * v_ + (1.0 - ADAM_B2) * (g_ * g_)
        m_hat = m2 / (1.0 - ADAM_B1 ** ADAM_STEP)
        v_hat = v2 / (1.0 - ADAM_B2 ** ADAM_STEP)
        return -ADAM_LR * (m_hat / (jnp.sqrt(v_hat) + ADAM_EPS) + ADAM_WD * w_), m2, v2

    cols = w.shape[1]
    tm = _pick(w.shape[0], (64, 32, 16, 8) if cols > 1024 else (256, 128, 64, 32, 16, 8))
    return rowwise(fn, name, [w, g, m, v], [], [(cols, F32)] * 3, tm=tm)


def _pack_rows(arrs, row_align, total_align=1):
    parts, offs, r = [], [], 0
    for a in arrs:
        n = a.size
        nr = -(-n // (FLAT_W * row_align)) * row_align
        parts.append(jnp.pad(a.reshape(-1), (0, nr * FLAT_W - n)).reshape(nr, FLAT_W))
        offs.append(r)
        r += nr
    if r % total_align:
        parts.append(jnp.zeros((-r % total_align, FLAT_W), arrs[0].dtype))
    return jnp.concatenate(parts, axis=0), offs


def _unpack_rows(packed, offs, shapes):
    out = []
    for o, s in zip(offs, shapes):
        n = 1
        for k in s:
            n *= k
        nr = -(-n // FLAT_W)
        out.append(packed[..., o:o + nr, :].reshape(packed.shape[:-2] + (nr * FLAT_W,))[..., :n].reshape(packed.shape[:-2] + tuple(s)))
    return out


BIG = ("ffn1_w_in", "ffn1_w_out", "ab_w_in", "ab_w_out", "fox_w_in", "fox_w_out", "ffn2_w_in", "ffn2_w_out",
       "ple_w_gate", "ple_w_up")
FFN = ("ffn1_w_in", "ffn1_w_out", "ffn2_w_in", "ffn2_w_out")
REST = ("ab_w_in", "ab_w_out", "fox_w_in", "fox_w_out", "ple_w_gate", "ple_w_up")
COL_SHARDED = ("ffn1_w_in", "ab_w_in", "fox_w_in", "ffn2_w_in", "ple_w_up")
SMALL = ("ffn1_norm", "mix_norm", "ssd_conv_w", "ssd_conv_b", "ssd_dt_bias", "ssd_a_log", "ssd_d", "ssd_norm",
         "hgrn_lb_logits", "hgrn_norm", "fox_b_f", "ffn2_norm", "ple_gate_norm", "ple_norm", "final_norm")
WEIGHTS = ("ffn1_norm", "ffn1_w_in", "ffn1_w_out", "mix_norm", "ab_w_in", "ssd_conv_w", "ssd_conv_b", "ssd_dt_bias",
           "ssd_a_log", "ssd_d", "ssd_norm", "hgrn_lb_logits", "hgrn_norm", "ab_w_out", "fox_w_in", "fox_b_f", "fox_w_out",
           "ffn2_norm", "ffn2_w_in", "ffn2_w_out", "ple_gate_norm", "ple_w_gate", "ple_w_up", "ple_norm", "final_norm")


def _full_from_shards(name, g4):
    if name in COL_SHARDED:
        return jnp.moveaxis(g4, 0, 2).reshape(g4.shape[1], g4.shape[2], 4 * g4.shape[3])
    return jnp.moveaxis(g4, 0, 1).reshape(g4.shape[1], 4 * g4.shape[2], g4.shape[3])


def _shards_from_full(name, full):
    ly, r, c = full.shape
    if name in COL_SHARDED:
        return jnp.moveaxis(full.reshape(ly, r, 4, c // 4), 2, 0)
    return jnp.moveaxis(full.reshape(ly, 4, r // 4, c), 1, 0)


def _pad_to(a, axis, n):
    pad = [(0, 0)] * a.ndim
    pad[axis] = (0, n - a.shape[axis])
    return jnp.pad(a, pad)


def _lane_row(v):
    return _pad_to(v.reshape(1, -1), 1, LANE)


def kernel(x, p, ffn1_norm, ffn1_w_in, ffn1_w_out, mix_norm, ab_w_in, ssd_conv_w, ssd_conv_b, ssd_dt_bias, ssd_a_log, ssd_d, ssd_norm, hgrn_lb_logits, hgrn_norm, ab_w_out, fox_w_in, fox_b_f, fox_w_out, ffn2_norm, ffn2_w_in, ffn2_w_out, ple_gate_norm, ple_w_gate, ple_w_up, ple_norm, final_norm, loss_target, m_ffn1_norm, m_ffn1_w_in, m_ffn1_w_out, m_mix_norm, m_ab_w_in, m_ssd_conv_w, m_ssd_conv_b, m_ssd_dt_bias, m_ssd_a_log, m_ssd_d, m_ssd_norm, m_hgrn_lb_logits, m_hgrn_norm, m_ab_w_out, m_fox_w_in, m_fox_b_f, m_fox_w_out, m_ffn2_norm, m_ffn2_w_in, m_ffn2_w_out, m_ple_gate_norm, m_ple_w_gate, m_ple_w_up, m_ple_norm, m_final_norm, v_ffn1_norm, v_ffn1_w_in, v_ffn1_w_out, v_mix_norm, v_ab_w_in, v_ssd_conv_w, v_ssd_conv_b, v_ssd_dt_bias, v_ssd_a_log, v_ssd_d, v_ssd_norm, v_hgrn_lb_logits, v_hgrn_norm, v_ab_w_out, v_fox_w_in, v_fox_b_f, v_fox_w_out, v_ffn2_norm, v_ffn2_w_in, v_ffn2_w_out, v_ple_gate_norm, v_ple_w_gate, v_ple_w_up, v_ple_norm, v_final_norm):
    given = dict(locals())
    w = {n: given[n] for n in WEIGHTS}
    mom = {n: given["m_" + n] for n in WEIGHTS}
    var = {n: given["v_" + n] for n in WEIGHTS}
    h = x[0]
    t, d = h.shape
    depth = p.shape[0]
    nh_ssd = d // SSD_HEAD_DIM
    nh_fox = d // FOX_HEAD_DIM
    conv_dim = d + 2 * SSD_GROUPS * SSD_STATE
    d_ff = ffn1_w_out.shape[1] * 4
    fp = -(-d_ff // FF_ALIGN) * FF_ALIGN
    xi, yi, ci = lax.axis_index("x"), lax.axis_index("y"), lax.axis_index("c")
    chip = 2 * xi + yi

    assert depth == 2
    ffn_w = {n: gather_filled(w[n].astype(BF16), chip, f"gather_{n}") for n in FFN}
    for n in ("ffn1_w_out", "ffn2_w_out"):
        g = ffn_w[n]
        ffn_w[n] = g.reshape(g.shape[0], 2, 2 * g.shape[2], g.shape[3])
    big_local = [w[n].astype(BF16) for n in REST]
    packed, offs = _pack_rows(big_local, ROW_ALIGN, PACK_ALIGN)
    rows = packed.shape[0]
    gathered = gather_filled(packed.reshape(2, rows // 2, FLAT_W), chip, "gather_rest")
    gathered = jnp.moveaxis(gathered, 0, 1).reshape(4, rows, FLAT_W)
    full = {n: _full_from_shards(n, g4) for n, g4 in zip(REST, _unpack_rows(gathered, offs, [a.shape for a in big_local]))}
    cw_local = ssd_conv_w[0]
    cshard = cw_local.shape[1]
    cw_rows = -(-SSD_CONV * conv_dim // FLAT_W)
    cw_placed = lax.dynamic_update_slice(jnp.zeros((SSD_CONV, conv_dim), F32), cw_local, (0, chip * cshard))
    cw_placed = jnp.where(ci == 0, cw_placed, 0.0)
    cw_packed, _ = _pack_rows([cw_placed], 8)
    conv_w = allreduce_small(cw_packed, "gather_conv_w")[:cw_rows].reshape(-1)[:SSD_CONV * conv_dim].reshape(SSD_CONV, conv_dim)

    ab = full["ab_w_in"][0]
    s = [0, d, d + conv_dim, d + conv_dim + nh_ssd]
    ab_wts = (ab[:, s[0]:s[1]], ab[:, s[1]:s[2]], ab[:, s[3]:], _pad_to(ab[:, s[2]:s[3]], 1, LANE))
    fox = full["fox_w_in"][0]
    fox_qkv, fox_f = fox[:, :3 * d], _pad_to(fox[:, 3 * d:], 1, LANE)
    fox_bias = _lane_row(fox_b_f[0])
    dt_bias, a_log = _lane_row(ssd_dt_bias[0]), _lane_row(ssd_a_log[0])
    lb_soft = rowwise(lambda z: (lambda e: e / jnp.sum(e, axis=0, keepdims=True))(jnp.exp(z - jnp.max(z, axis=0, keepdims=True))),
                      "lb_softmax", [hgrn_lb_logits], [], [(d, F32)], tm=hgrn_lb_logits.shape[0])[0]
    lb = lb_soft[0:1]
    conv_b = ssd_conv_b

    saved = []
    for i in range(depth):
        h, s1 = ffn_fwd(h, ffn1_norm[i:i + 1], ffn_w["ffn1_w_in"], ffn_w["ffn1_w_out"], i, f"l{i}_ffn1")
        if i % 2 == 0:
            h, s2 = mixer_fwd(h, mix_norm[i:i + 1], ab_wts, conv_w, conv_b, dt_bias, a_log, ssd_d[0], ssd_norm, lb, hgrn_norm,
                              full["ab_w_out"][0], f"l{i}_mix")
        else:
            h, s2 = fox_fwd(h, mix_norm[i:i + 1], fox_qkv, fox_f, fox_bias, full["fox_w_out"][0], f"l{i}_fox")
        h, s3 = ffn_fwd(h, ffn2_norm[i:i + 1], ffn_w["ffn2_w_in"], ffn_w["ffn2_w_out"], i, f"l{i}_ffn2")
        h, s4 = ple_fwd(h, p[i, 0], ple_gate_norm[i:i + 1], full["ple_w_gate"][i], full["ple_w_up"][i], ple_norm[i:i + 1], f"l{i}_ple")
        saved.append((s1, s2, s3, s4))
    loss, dh, g_final = loss_head(h, loss_target[0], final_norm.reshape(1, d))

    gb = {n: [None] * w[n].shape[0] for n in BIG}
    gs = {n: [None] * w[n].shape[0] for n in SMALL}
    gs["final_norm"] = g_final[0]

    def ffn_grads(k, i, dw_in, dw_out):
        gb[f"ffn{k}_w_in"][i] = dw_in
        gb[f"ffn{k}_w_out"][i] = dw_out.reshape(4, dw_out.shape[1] // 2, d)

    for i in reversed(range(depth)):
        s1, s2, s3, s4 = saved[i]
        dh, dgn, dwg, dwu, dpn = ple_bwd(dh, s4, p[i, 0], ple_gate_norm[i:i + 1], full["ple_w_gate"][i], ple_norm[i:i + 1], f"l{i}_ple")
        gs["ple_gate_norm"][i], gs["ple_norm"][i] = dgn[0], dpn[0]
        gb["ple_w_gate"][i], gb["ple_w_up"][i] = dwg, dwu
        dh, dnw, dw_in, dw_out = ffn_bwd(dh, s3, ffn2_norm[i:i + 1], ffn_w["ffn2_w_in"], ffn_w["ffn2_w_out"], i, f"l{i}_ffn2")
        gs["ffn2_norm"][i] = dnw[0]
        ffn_grads(2, i, dw_in, dw_out)
        if i % 2 == 0:
            dh, gm = mixer_bwd(dh, s2, mix_norm[i:i + 1], ab_wts, conv_w, conv_b, dt_bias, a_log, ssd_norm, lb, hgrn_norm,
                               full["ab_w_out"][0], f"l{i}_mix")
            gs["mix_norm"][i] = gm["mix_norm"][0]
            q4 = gm["w_qfvg"]
            gb["ab_w_in"][0] = jnp.concatenate([gm["w_z"], gm["w_xbc"], gm["w_dt"][:, :nh_ssd], q4], axis=1)
            gb["ab_w_out"][0] = gm["w_out"]
            gs["ssd_conv_w"][0], gs["ssd_conv_b"][0] = gm["conv_w"], gm["conv_b"][0]
            gs["ssd_dt_bias"][0], gs["ssd_a_log"][0] = gm["dt_bias"][0, :nh_ssd], gm["a_log"][0, :nh_ssd]
            gs["ssd_d"][0], gs["ssd_norm"][0], gs["hgrn_norm"][0] = gm["d_skip"][0], gm["ssd_norm"][0], gm["hgrn_norm"][0]
            dlb = gm["lb"]
        else:
            dh, dnw, dwqkv, dwf, dbf, dwo = fox_bwd(dh, s2, mix_norm[i:i + 1], fox_qkv, fox_f, fox_bias, full["fox_w_out"][0], f"l{i}_fox")
            gs["mix_norm"][i] = dnw[0]
            gb["fox_w_in"][0] = jnp.concatenate([dwqkv, dwf[:, :nh_fox]], axis=1)
            gb["fox_w_out"][0] = dwo
            gs["fox_b_f"][0] = dbf[0, :nh_fox]
        dh, dnw, dw_in, dw_out = ffn_bwd(dh, s1, ffn1_norm[i:i + 1], ffn_w["ffn1_w_in"], ffn_w["ffn1_w_out"], i, f"l{i}_ffn1")
        gs["ffn1_norm"][i] = dnw[0]
        ffn_grads(1, i, dw_in, dw_out)
    grad_x = dh[None]
    first_row = (jnp.arange(hgrn_lb_logits.shape[0]) == 0).astype(F32)[:, None]
    gs["hgrn_lb_logits"] = rowwise(lambda sm, g, e: sm * (e - sm[0:1]) * g, "lb_softmax_bwd",
                                   [lb_soft, jnp.broadcast_to(dlb, lb_soft.shape), jnp.broadcast_to(first_row, lb_soft.shape)],
                                   [], [(d, F32)], tm=lb_soft.shape[0])[0]

    c_idx = ci.reshape(1).astype(jnp.int32)
    g_big = {n: reduce_scatter_grads(gb[n][0], gb[n][1], c_idx, n) for n in FFN}
    g4 = [_shards_from_full(n, jnp.stack(gb[n])) for n in REST]
    g_packed = jnp.stack([_pack_rows([a[k] for a in g4], ROW_ALIGN, PACK_ALIGN)[0] for k in range(4)])
    g_halves = g_packed.reshape(4, 2, rows // 2, FLAT_W)
    g_red = reduce_scatter_grads(g_halves[:, 0], g_halves[:, 1], c_idx, "rest").reshape(rows, FLAT_W)
    g_big.update(zip(REST, _unpack_rows(g_red, offs, [a.shape for a in big_local])))

    small_local = [jnp.stack(gs[n]) if isinstance(gs[n], list) else gs[n] for n in SMALL]
    small_local = [a.reshape(w[n].shape if n != "ssd_conv_w" else (1, SSD_CONV, conv_dim)) for n, a in zip(SMALL, small_local)]
    sp, soffs = _pack_rows(small_local + [loss.reshape(1)], 8)
    sr = allreduce_small(sp, "allreduce_small_grads")
    small_red = _unpack_rows(sr, soffs, [a.shape for a in small_local] + [(1,)])
    loss_total = small_red[-1][0]
    g_small = dict(zip(SMALL, small_red[:-1]))
    g_small["ssd_conv_w"] = lax.dynamic_slice(g_small["ssd_conv_w"], (0, 0, chip * cshard), (1, SSD_CONV, cshard))

    grads, delta, new_m, new_v = {}, {}, {}, {}
    for n in BIG:
        shp = w[n].shape
        two_d = (shp[0] * shp[1], shp[2])
        grads[n] = g_big[n]
        dl, m2, v2 = adamw(w[n].reshape(two_d), g_big[n].reshape(two_d), mom[n].reshape(two_d), var[n].reshape(two_d), f"adamw_{n}")
        delta[n], new_m[n], new_v[n] = dl.reshape(shp), m2.reshape(shp), v2.reshape(shp)
    packs = [_pack_rows([src[n] for n in SMALL], 8) for src in (w, g_small, mom, var)]
    dl, m2, v2 = adamw(packs[0][0], packs[1][0], packs[2][0], packs[3][0], "adamw_small")
    shapes = [w[n].shape for n in SMALL]
    for n, a, b, c_ in zip(SMALL, _unpack_rows(dl, packs[0][1], shapes), _unpack_rows(m2, packs[0][1], shapes), _unpack_rows(v2, packs[0][1], shapes)):
        grads[n], delta[n], new_m[n], new_v[n] = g_small[n], a, b, c_
    return (loss_total, grad_x, *[grads[n] for n in WEIGHTS], *[delta[n] for n in WEIGHTS],
            *[new_m[n] for n in WEIGHTS], *[new_v[n] for n in WEIGHTS])
```

```python
import functools

import jax
import jax.numpy as jnp
from jax import lax
from jax.experimental import pallas as pl
from jax.experimental.pallas import tpu as pltpu

F32 = jnp.float32
BF16 = jnp.bfloat16
HI = lax.Precision.HIGHEST

EPS = 1e-6
CHUNK = 64
SSD_HEAD_DIM = 64
SSD_GROUPS = 4
SSD_STATE = 128
SSD_CONV = 4
HGRN_KDIM = 128
FOX_HEAD_DIM = 128
LANE = 128
FF_ALIGN = 512
FLAT_W = 2048
ROW_ALIGN = 32
PACK_ALIGN = 1024

ADAM_LR = 0.001
ADAM_B1 = 0.9
ADAM_B2 = 0.999
ADAM_EPS = 1e-08
ADAM_WD = 0.01
ADAM_STEP = 10

VMEM_LIMIT = 56 * 1024 * 1024
MESH_ID = pl.DeviceIdType.MESH


def _cparams(sem):
    return pltpu.CompilerParams(dimension_semantics=sem, vmem_limit_bytes=VMEM_LIMIT)


def _pick(n, prefs):
    for t in prefs:
        if n % t == 0:
            return t
    return n


def _dot(a, b, dims, precision=None):
    return lax.dot_general(a, b, (dims, ((), ())), preferred_element_type=F32, precision=precision)


NN = ((1,), (0,))
NT = ((1,), (1,))
TN = ((0,), (0,))


def _sigmoid(x):
    return 1.0 / (1.0 + jnp.exp(-x))


def _silu(x):
    return x * _sigmoid(x)


def _dsilu(x):
    s = _sigmoid(x)
    return s * (1.0 + x * (1.0 - s))


def matmul(a, b, mode, out_dtype, name, scale=None, residual=None):
    if mode == "nn":
        (m, k), (k2, n) = a.shape, b.shape
    elif mode == "nt":
        (m, k), (n, k2) = a.shape, b.shape
    else:
        (k, m), (k2, n) = a.shape, b.shape
    assert k == k2, (a.shape, b.shape, mode)
    tm = _pick(m, (1024, 512, 256, 128))
    tn = _pick(n, (1024, 1408, 512, 256, 128))
    tk = _pick(k, (2048, 1408, 1024, 512, 256, 128))
    nk = k // tk
    dims = {"nn": NN, "nt": NT, "tn": TN}[mode]

    def body(*refs):
        if residual is None:
            a_ref, b_ref, o_ref, acc_ref = refs
            r_ref = None
        else:
            a_ref, b_ref, r_ref, o_ref, acc_ref = refs
        kk = pl.program_id(2)

        @pl.when(kk == 0)
        def _():
            acc_ref[...] = jnp.zeros_like(acc_ref)

        acc_ref[...] += _dot(a_ref[...].astype(BF16), b_ref[...].astype(BF16), dims)

        @pl.when(kk == nk - 1)
        def _():
            r = acc_ref[...]
            if scale is not None:
                r = r * scale
            if r_ref is not None:
                r = r + r_ref[...].astype(F32)
            o_ref[...] = r.astype(o_ref.dtype)

    if mode == "nn":
        a_spec = pl.BlockSpec((tm, tk), lambda i, j, kk: (i, kk))
        b_spec = pl.BlockSpec((tk, tn), lambda i, j, kk: (kk, j))
    elif mode == "nt":
        a_spec = pl.BlockSpec((tm, tk), lambda i, j, kk: (i, kk))
        b_spec = pl.BlockSpec((tn, tk), lambda i, j, kk: (j, kk))
    else:
        a_spec = pl.BlockSpec((tk, tm), lambda i, j, kk: (kk, i))
        b_spec = pl.BlockSpec((tk, tn), lambda i, j, kk: (kk, j))
    o_spec = pl.BlockSpec((tm, tn), lambda i, j, kk: (i, j))
    in_specs = [a_spec, b_spec]
    args = [a, b]
    if residual is not None:
        in_specs.append(o_spec)
        args.append(residual)
    return pl.pallas_call(
        body, name=name, grid=(m // tm, n // tn, nk),
        in_specs=in_specs, out_specs=o_spec,
        out_shape=jax.ShapeDtypeStruct((m, n), out_dtype),
        scratch_shapes=[pltpu.VMEM((tm, tn), F32)],
        compiler_params=_cparams(("parallel", "parallel", "arbitrary")),
    )(*args)


def rowwise(fn, name, rows, consts, outs, accs=(), tm=256):
    rows = [r if isinstance(r, tuple) else (r, r.shape[1], 0) for r in rows]
    t = rows[0][0].shape[0]
    tm = min(tm, t)
    assert t % tm == 0
    n_in = len(rows) + len(consts)
    n_out = len(outs)

    def body(*refs):
        res = fn(*[r[...] for r in refs[:n_in]])
        if not isinstance(res, tuple):
            res = (res,)
        for r, v in zip(refs[n_in:n_in + n_out], res[:n_out]):
            r[...] = v.astype(r.dtype)
        if accs:
            a_refs = refs[n_in + n_out:]
            first = pl.program_id(0) == 0

            @pl.when(first)
            def _():
                for r, v in zip(a_refs, res[n_out:]):
                    r[...] = v

            @pl.when(jnp.logical_not(first))
            def _():
                for r, v in zip(a_refs, res[n_out:]):
                    r[...] += v

    in_specs = [pl.BlockSpec((tm, w), functools.partial(lambda i, cb: (i, cb), cb=cb)) for _, w, cb in rows]
    in_specs += [pl.BlockSpec(c.shape, lambda i: (0, 0)) for c in consts]
    out_specs = [pl.BlockSpec((tm, w), lambda i: (i, 0)) for w, _ in outs]
    out_specs += [pl.BlockSpec(s, lambda i: (0, 0)) for s in accs]
    out_shape = [jax.ShapeDtypeStruct((t, w), d) for w, d in outs]
    out_shape += [jax.ShapeDtypeStruct(s, F32) for s in accs]
    res = pl.pallas_call(
        body, name=name, grid=(t // tm,), in_specs=in_specs, out_specs=out_specs, out_shape=out_shape,
        compiler_params=_cparams(("arbitrary",) if accs else ("parallel",)),
    )(*[r[0] for r in rows], *consts)
    return res


def _rms(x):
    return lax.rsqrt(jnp.mean(x * x, axis=-1, keepdims=True) + EPS)


def _norm_bwd(x, w, dy):
    r = _rms(x)
    xh = x * r
    g = dy * w
    dx = r * (g - xh * jnp.mean(g * xh, axis=-1, keepdims=True))
    return dx, jnp.sum(dy * xh, axis=0, keepdims=True)


def norm_fwd(h, w, name):
    return rowwise(lambda x, w_: x * _rms(x) * w_, name, [h], [w], [(h.shape[1], BF16)])[0]


def norm_bwd_res(h, w, dhn, dres, name):
    def fn(x, dy, dr, w_):
        dx, dw = _norm_bwd(x, w_, dy.astype(F32))
        return dr + dx, dw
    d = h.shape[1]
    return rowwise(fn, name, [h, dhn, dres], [w], [(d, F32)], [(1, d)])


def _mm(name, grid, a, a_spec, b, b_spec, out_shape, o_spec, acc_shape, dims, scale=None, residual=None):
    nk = grid[2]

    def body(*refs):
        a_ref, b_ref = refs[0], refs[1]
        r_ref = refs[2] if residual is not None else None
        o_ref, acc_ref = refs[-2], refs[-1]
        kk = pl.program_id(2)

        @pl.when(kk == 0)
        def _():
            acc_ref[...] = jnp.zeros_like(acc_ref)

        acc_ref[...] += _dot(a_ref[...].astype(BF16), b_ref[...].astype(BF16), dims)

        @pl.when(kk == nk - 1)
        def _():
            r = acc_ref[...]
            if scale is not None:
                r = r * scale
            if r_ref is not None:
                r = r + r_ref[...]
            o_ref[...] = r.astype(o_ref.dtype)

    in_specs, args = [a_spec, b_spec], [a, b]
    if residual is not None:
        in_specs.append(o_spec)
        args.append(residual)
    return pl.pallas_call(
        body, name=name, grid=grid, in_specs=in_specs, out_specs=o_spec, out_shape=out_shape,
        scratch_shapes=[pltpu.VMEM(acc_shape, F32)],
        compiler_params=_cparams(("parallel", "parallel", "arbitrary")),
    )(*args)


def ffn_fwd(h, norm_w, w_in, w_out, layer, tag):
    t, d = h.shape
    ns = w_in.shape[3]
    tm = _pick(t, (1024, 512, 256, 128))
    tk = _pick(d, (1024, 512, 256, 128))
    tn = _pick(d, (1024, 512, 256, 128))
    hn = norm_fwd(h, norm_w, f"{tag}_norm")
    u = _mm(f"{tag}_in", (t // tm, 4, d // tk),
            hn, pl.BlockSpec((tm, tk), lambda i, s, k: (i, k)),
            w_in, pl.BlockSpec((None, None, tk, ns), lambda i, s, k: (layer, s, k, 0)),
            jax.ShapeDtypeStruct((4, t, ns), BF16), pl.BlockSpec((None, tm, ns), lambda i, s, k: (s, i, 0)), (tm, ns), NN)
    u4 = u.reshape(2, 2, t, ns)
    tr = _pick(t, (256, 128))

    def act(u_ref, a_ref):
        a_ref[...] = (_silu(u_ref[0].astype(F32)) * u_ref[1].astype(F32)).astype(a_ref.dtype)

    a = pl.pallas_call(
        act, name=f"{tag}_act", grid=(2, t // tr),
        in_specs=[pl.BlockSpec((2, None, tr, ns), lambda s, i: (0, s, i, 0))],
        out_specs=pl.BlockSpec((None, tr, ns), lambda s, i: (s, i, 0)),
        out_shape=jax.ShapeDtypeStruct((2, t, ns), BF16), compiler_params=_cparams(("parallel", "parallel")),
    )(u4)
    out = _mm(f"{tag}_out", (t // tm, d // tn, 2),
              a, pl.BlockSpec((None, tm, ns), lambda i, j, k: (k, i, 0)),
              w_out, pl.BlockSpec((None, None, ns, tn), lambda i, j, k: (layer, k, 0, j)),
              jax.ShapeDtypeStruct((t, d), F32), pl.BlockSpec((tm, tn), lambda i, j, k: (i, j)), (tm, tn), NN,
              scale=0.5, residual=h)
    return out, (h, hn, u4, a)


def ffn_bwd(dout, saved, norm_w, w_in, w_out, layer, tag):
    h, hn, u4, a = saved
    t, d = h.shape
    ns = w_in.shape[3]
    tm = _pick(t, (1024, 512, 256, 128))
    tk = _pick(d, (1024, 512, 256, 128))
    tn = _pick(d, (1024, 512, 256, 128))
    tkt = _pick(t, (1024, 512, 256, 128))
    th = _pick(d, (512, 256, 128))
    da = _mm(f"{tag}_da", (t // tm, 2, d // tk),
             dout, pl.BlockSpec((tm, tk), lambda i, s, k: (i, k)),
             w_out, pl.BlockSpec((None, None, ns, tk), lambda i, s, k: (layer, s, 0, k)),
             jax.ShapeDtypeStruct((2, t, ns), BF16), pl.BlockSpec((None, tm, ns), lambda i, s, k: (s, i, 0)), (tm, ns), NT,
             scale=0.5)
    tr = _pick(t, (256, 128))

    def dact(da_ref, u_ref, du_ref):
        da_, g, up = da_ref[...].astype(F32), u_ref[0].astype(F32), u_ref[1].astype(F32)
        du_ref[0] = (da_ * up * _dsilu(g)).astype(du_ref.dtype)
        du_ref[1] = (da_ * _silu(g)).astype(du_ref.dtype)

    pair = pl.BlockSpec((2, None, tr, ns), lambda s, i: (0, s, i, 0))
    du = pl.pallas_call(
        dact, name=f"{tag}_dact", grid=(2, t // tr),
        in_specs=[pl.BlockSpec((None, tr, ns), lambda s, i: (s, i, 0)), pair], out_specs=pair,
        out_shape=jax.ShapeDtypeStruct((2, 2, t, ns), BF16), compiler_params=_cparams(("parallel", "parallel")),
    )(da, u4).reshape(4, t, ns)
    dw_out = _mm(f"{tag}_dwout", (2, d // th, t // tkt),
                 a, pl.BlockSpec((None, tkt, ns), lambda s, j, k: (s, k, 0)),
                 dout, pl.BlockSpec((tkt, th), lambda s, j, k: (k, j)),
                 jax.ShapeDtypeStruct((2, ns, d), F32), pl.BlockSpec((None, ns, th), lambda s, j, k: (s, 0, j)), (ns, th), TN,
                 scale=0.5)
    dhn = _mm(f"{tag}_dhn", (t // tm, d // tn, 4),
              du, pl.BlockSpec((None, tm, ns), lambda i, j, k: (k, i, 0)),
              w_in, pl.BlockSpec((None, None, tn, ns), lambda i, j, k: (layer, k, j, 0)),
              jax.ShapeDtypeStruct((t, d), F32), pl.BlockSpec((tm, tn), lambda i, j, k: (i, j)), (tm, tn), NT)
    dw_in = _mm(f"{tag}_dwin", (d // th, 4, t // tkt),
                hn, pl.BlockSpec((tkt, th), lambda i, s, k: (k, i)),
                du, pl.BlockSpec((None, tkt, ns), lambda i, s, k: (s, k, 0)),
                jax.ShapeDtypeStruct((4, d, ns), F32), pl.BlockSpec((None, th, ns), lambda i, s, k: (s, i, 0)), (th, ns), TN)
    dh, dnw = norm_bwd_res(h, norm_w, dhn, dout, f"{tag}_dnorm")
    return dh, dnw, dw_in, dw_out


def ple_fwd(h, p_i, gate_norm_w, w_gate, w_up, post_norm_w, tag):
    d = h.shape[1]
    e0 = matmul(p_i, w_up, "nn", F32, f"{tag}_up")
    hn = norm_fwd(h, gate_norm_w, f"{tag}_norm")
    s = matmul(hn, w_gate, "nn", F32, f"{tag}_gate")
    out = rowwise(lambda x, e, s_, pw: x + e * _rms(e) * pw * _sigmoid(s_), f"{tag}_add",
                  [h, e0, s], [post_norm_w], [(d, F32)])[0]
    return out, (h, hn, e0, s)


def ple_bwd(dout, saved, p_i, gate_norm_w, w_gate, post_norm_w, tag):
    h, hn, e0, s = saved
    d = h.shape[1]

    def fn(dy, e, s_, pw):
        gate = _sigmoid(s_)
        emb = e * _rms(e) * pw
        de0, dpw = _norm_bwd(e, pw, dy * gate)
        return de0, dy * emb * gate * (1.0 - gate), dpw

    de0, ds, dpw = rowwise(fn, f"{tag}_dadd", [dout, e0, s], [post_norm_w], [(d, BF16), (d, BF16)], [(1, d)])
    dw_up = matmul(p_i, de0, "tn", F32, f"{tag}_dwup")
    dhn = matmul(ds, w_gate, "nt", F32, f"{tag}_dhn")
    dw_gate = matmul(hn, ds, "tn", F32, f"{tag}_dwgate")
    dh, dgn = norm_bwd_res(h, gate_norm_w, dhn, dout, f"{tag}_dnorm")
    return dh, dgn, dw_gate, dw_up, dpw


def loss_head(h, target, w):
    d = h.shape[1]

    def fn(x, tgt, w_):
        r = _rms(x)
        err = x * r * w_ - tgt
        dx, dw = _norm_bwd(x, w_, err * (1.0 / d))
        part = 0.5 * jnp.sum(jnp.sum(err * err, axis=-1, keepdims=True), axis=0, keepdims=True) * (1.0 / d)
        return dx, dw, jnp.broadcast_to(part, (1, LANE))

    dh, dw, loss = rowwise(fn, "loss_head", [h, target], [w], [(d, F32)], [(1, d), (1, LANE)])
    return loss[0, 0], dh, dw


def cumsum_rows(x, name, reverse=False):
    t, w = x.shape
    tb = min(256, t)
    nb = t // tb

    def body(x_ref, o_ref, carry):
        @pl.when(pl.program_id(0) == 0)
        def _():
            carry[...] = jnp.zeros_like(carry)

        r = lax.broadcasted_iota(jnp.int32, (tb, tb), 0)
        c = lax.broadcasted_iota(jnp.int32, (tb, tb), 1)
        tri = ((c >= r) if reverse else (c <= r)).astype(F32)
        y = _dot(tri, x_ref[...], NN, HI) + carry[...]
        o_ref[...] = y
        carry[...] = y[0:1, :] if reverse else y[tb - 1:tb, :]

    idx = (lambda i: (nb - 1 - i, 0)) if reverse else (lambda i: (i, 0))
    return pl.pallas_call(
        body, name=name, grid=(nb,), in_specs=[pl.BlockSpec((tb, w), idx)], out_specs=pl.BlockSpec((tb, w), idx),
        out_shape=jax.ShapeDtypeStruct((t, w), F32), scratch_shapes=[pltpu.VMEM((1, w), F32)],
        compiler_params=_cparams(("arbitrary",)),
    )(x)


def _fox_tiles(t):
    return _pick(t, (1024, 512, 256, 128)), _pick(t, (512, 256, 128))


def _fox_pairs(t, resident_is_query):
    tr, ts = _fox_tiles(t)
    rows = []
    for ri in range(t // tr):
        if resident_is_query:
            sis = list(range((ri * tr + tr - 1) // ts + 1))
        else:
            sis = list(range((ri * tr) // ts, t // ts))
        for si in sis:
            q0, k0 = (ri * tr, si * ts) if resident_is_query else (si * ts, ri * tr)
            qn, kn = (tr, ts) if resident_is_query else (ts, tr)
            rows.append((ri, si, si == sis[0], si == sis[-1], q0 < k0 + kn - 1))
    return tuple(jnp.asarray([r[j] for r in rows], jnp.int32) for j in range(5))


def _causal(s, row0, col0, transposed=False):
    r = row0 + lax.broadcasted_iota(jnp.int32, s.shape, 0)
    c = col0 + lax.broadcasted_iota(jnp.int32, s.shape, 1)
    return jnp.where((c >= r) if transposed else (r >= c), s, -jnp.inf)


def _on_diagonal(flag, step):
    @pl.when(flag == 1)
    def _():
        step(True)

    @pl.when(flag == 0)
    def _():
        step(False)


def fox_attn_fwd(qkv, dcol, drow, name):
    t = qkv.shape[0]
    nh = dcol.shape[0]
    tq, tk = _fox_tiles(t)
    pairs = _fox_pairs(t, True)
    scale = FOX_HEAD_DIM ** -0.5

    def body(qs, ks, fs, ls, dg, q_ref, k_ref, v_ref, dc_ref, dr_ref, o_ref, lse_ref, m_s, l_s, acc_s):
        p = pl.program_id(1)

        @pl.when(fs[p] == 1)
        def _():
            m_s[...] = jnp.full_like(m_s, -jnp.inf)
            l_s[...] = jnp.zeros_like(l_s)
            acc_s[...] = jnp.zeros_like(acc_s)

        def step(masked):
            s = _dot(q_ref[...], k_ref[...], NT) * scale + dc_ref[0] - dr_ref[0]
            if masked:
                s = _causal(s, qs[p] * tq, ks[p] * tk)
            m_new = jnp.maximum(m_s[...], jnp.max(s, axis=1, keepdims=True))
            alpha = jnp.exp(m_s[...] - m_new)
            pr = jnp.exp(s - m_new)
            l_s[...] = alpha * l_s[...] + jnp.sum(pr, axis=1, keepdims=True)
            acc_s[...] = alpha * acc_s[...] + _dot(pr.astype(BF16), v_ref[...], NN)
            m_s[...] = m_new

        _on_diagonal(dg[p], step)

        @pl.when(ls[p] == 1)
        def _():
            o_ref[...] = (acc_s[...] / l_s[...]).astype(o_ref.dtype)
            lse_ref[0] = m_s[...] + jnp.log(l_s[...])

    hd = FOX_HEAD_DIM
    qcol = pl.BlockSpec((1, tq, 1), lambda h, p, qs, ks, *_: (h, qs[p], 0))
    grid_spec = pltpu.PrefetchScalarGridSpec(
        num_scalar_prefetch=5, grid=(nh, pairs[0].shape[0]),
        in_specs=[
            pl.BlockSpec((tq, hd), lambda h, p, qs, ks, *_: (qs[p], h)),
            pl.BlockSpec((tk, hd), lambda h, p, qs, ks, *_: (ks[p], nh + h)),
            pl.BlockSpec((tk, hd), lambda h, p, qs, ks, *_: (ks[p], 2 * nh + h)),
            qcol,
            pl.BlockSpec((1, 1, tk), lambda h, p, qs, ks, *_: (h, 0, ks[p])),
        ],
        out_specs=[pl.BlockSpec((tq, hd), lambda h, p, qs, ks, *_: (qs[p], h)), qcol],
        scratch_shapes=[pltpu.VMEM((tq, 1), F32), pltpu.VMEM((tq, 1), F32), pltpu.VMEM((tq, hd), F32)])
    return pl.pallas_call(
        body, name=name, grid_spec=grid_spec,
        out_shape=[jax.ShapeDtypeStruct((t, nh * hd), BF16), jax.ShapeDtypeStruct((nh, t, 1), F32)],
        compiler_params=_cparams(("parallel", "arbitrary")),
    )(*pairs, qkv, qkv, qkv, dcol, drow)


def fox_attn_dq(qkv, do, dcol, drow, lse_col, delta_col, name):
    t = qkv.shape[0]
    nh = dcol.shape[0]
    tq, tk = _fox_tiles(t)
    pairs = _fox_pairs(t, True)
    scale = FOX_HEAD_DIM ** -0.5

    def body(qs, ks, fs, ls, dg, q_ref, k_ref, v_ref, do_ref, dc_ref, dr_ref, lse_ref, dl_ref, dq_ref, db_ref, acc_s, db_s):
        p = pl.program_id(1)

        @pl.when(fs[p] == 1)
        def _():
            acc_s[...] = jnp.zeros_like(acc_s)
            db_s[...] = jnp.zeros_like(db_s)

        def step(masked):
            s = _dot(q_ref[...], k_ref[...], NT) * scale + dc_ref[0] - dr_ref[0]
            if masked:
                s = _causal(s, qs[p] * tq, ks[p] * tk)
            pr = jnp.exp(s - lse_ref[0])
            dp = _dot(do_ref[...], v_ref[...], NT)
            ds = pr * (dp - dl_ref[0])
            acc_s[...] += _dot(ds.astype(BF16), k_ref[...], NN)
            db_s[...] += jnp.sum(ds, axis=1, keepdims=True)

        _on_diagonal(dg[p], step)

        @pl.when(ls[p] == 1)
        def _():
            dq_ref[...] = (acc_s[...] * scale).astype(dq_ref.dtype)
            db_ref[0] = db_s[...]

    hd = FOX_HEAD_DIM
    qblk = pl.BlockSpec((tq, hd), lambda h, p, qs, ks, *_: (qs[p], h))
    qcol = pl.BlockSpec((1, tq, 1), lambda h, p, qs, ks, *_: (h, qs[p], 0))
    grid_spec = pltpu.PrefetchScalarGridSpec(
        num_scalar_prefetch=5, grid=(nh, pairs[0].shape[0]),
        in_specs=[
            qblk,
            pl.BlockSpec((tk, hd), lambda h, p, qs, ks, *_: (ks[p], nh + h)),
            pl.BlockSpec((tk, hd), lambda h, p, qs, ks, *_: (ks[p], 2 * nh + h)),
            qblk, qcol,
            pl.BlockSpec((1, 1, tk), lambda h, p, qs, ks, *_: (h, 0, ks[p])),
            qcol, qcol,
        ],
        out_specs=[qblk, qcol],
        scratch_shapes=[pltpu.VMEM((tq, hd), F32), pltpu.VMEM((tq, 1), F32)])
    return pl.pallas_call(
        body, name=name, grid_spec=grid_spec,
        out_shape=[jax.ShapeDtypeStruct((t, nh * hd), BF16), jax.ShapeDtypeStruct((nh, t, 1), F32)],
        compiler_params=_cparams(("parallel", "arbitrary")),
    )(*pairs, qkv, qkv, qkv, do, dcol, drow, lse_col, delta_col)


def fox_attn_dkv(qkv, do, dcol, drow, lse_row, delta_row, name):
    t = qkv.shape[0]
    nh = dcol.shape[0]
    tk, tq = _fox_tiles(t)
    pairs = _fox_pairs(t, False)
    scale = FOX_HEAD_DIM ** -0.5

    def body(ks, qs, fs, ls, dg, q_ref, k_ref, v_ref, do_ref, dc_ref, dr_ref, lse_ref, dl_ref, dk_ref, dv_ref, db_ref, dk_s, dv_s, db_s):
        p = pl.program_id(1)

        @pl.when(fs[p] == 1)
        def _():
            dk_s[...] = jnp.zeros_like(dk_s)
            dv_s[...] = jnp.zeros_like(dv_s)
            db_s[...] = jnp.zeros_like(db_s)

        def step(masked):
            st = _dot(k_ref[...], q_ref[...], NT) * scale + dr_ref[0] - dc_ref[0]
            if masked:
                st = _causal(st, ks[p] * tk, qs[p] * tq, transposed=True)
            pt = jnp.exp(st - lse_ref[0])
            dv_s[...] += _dot(pt.astype(BF16), do_ref[...], NN)
            dpt = _dot(v_ref[...], do_ref[...], NT)
            dst = pt * (dpt - dl_ref[0])
            dk_s[...] += _dot(dst.astype(BF16), q_ref[...], NN)
            db_s[...] -= jnp.sum(dst, axis=1, keepdims=True)

        _on_diagonal(dg[p], step)

        @pl.when(ls[p] == 1)
        def _():
            dk_ref[...] = (dk_s[...] * scale).astype(dk_ref.dtype)
            dv_ref[...] = dv_s[...].astype(dv_ref.dtype)
            db_ref[0] = db_s[...]

    hd = FOX_HEAD_DIM
    qblk = pl.BlockSpec((tq, hd), lambda h, p, ks, qs, *_: (qs[p], h))
    qrow = pl.BlockSpec((1, 1, tq), lambda h, p, ks, qs, *_: (h, 0, qs[p]))
    kcol = pl.BlockSpec((1, tk, 1), lambda h, p, ks, qs, *_: (h, ks[p], 0))
    kv_out = pl.BlockSpec((tk, hd), lambda h, p, ks, qs, *_: (ks[p], h))
    grid_spec = pltpu.PrefetchScalarGridSpec(
        num_scalar_prefetch=5, grid=(nh, pairs[0].shape[0]),
        in_specs=[
            qblk,
            pl.BlockSpec((tk, hd), lambda h, p, ks, qs, *_: (ks[p], nh + h)),
            pl.BlockSpec((tk, hd), lambda h, p, ks, qs, *_: (ks[p], 2 * nh + h)),
            qblk, kcol, qrow, qrow, qrow,
        ],
        out_specs=[kv_out, kv_out, kcol],
        scratch_shapes=[pltpu.VMEM((tk, hd), F32), pltpu.VMEM((tk, hd), F32), pltpu.VMEM((tk, 1), F32)])
    return pl.pallas_call(
        body, name=name, grid_spec=grid_spec,
        out_shape=[jax.ShapeDtypeStruct((t, nh * hd), BF16), jax.ShapeDtypeStruct((t, nh * hd), BF16),
                   jax.ShapeDtypeStruct((nh, t, 1), F32)],
        compiler_params=_cparams(("parallel", "arbitrary")),
    )(*pairs, qkv, qkv, qkv, do, dcol, drow, lse_row, delta_row)


def _head_selector(d, hd):
    return (jnp.arange(d)[:, None] // hd == jnp.arange(LANE)[None, :]).astype(F32)


def _log_sigmoid(x):
    return jnp.minimum(x, 0.0) - jnp.log1p(jnp.exp(-jnp.abs(x)))


def fox_fwd(h, norm_w, w_qkv, w_f, b_f, w_out, tag):
    t, d = h.shape
    nh = d // FOX_HEAD_DIM
    hn = norm_fwd(h, norm_w, f"{tag}_norm")
    qkv = matmul(hn, w_qkv, "nn", BF16, f"{tag}_qkv")
    fr = matmul(hn, w_f, "nn", F32, f"{tag}_f")
    logf = rowwise(lambda x, b: _log_sigmoid(x + b), f"{tag}_logf", [fr], [b_f], [(LANE, F32)])[0]
    dcum = cumsum_rows(logf, f"{tag}_cum").T[:nh]
    dcol, drow = dcum[:, :, None], dcum[:, None, :]
    o, lse = fox_attn_fwd(qkv, dcol, drow, f"{tag}_attn")
    out = matmul(o, w_out, "nn", F32, f"{tag}_out", residual=h)
    return out, (h, hn, qkv, fr, dcol, drow, o, lse)


def fox_bwd(dout, saved, norm_w, w_qkv, w_f, b_f, w_out, tag):
    h, hn, qkv, fr, dcol, drow, o, lse = saved
    t, d = h.shape
    nh = d // FOX_HEAD_DIM
    do = matmul(dout, w_out, "nt", BF16, f"{tag}_do")
    dw_out = matmul(o, dout, "tn", F32, f"{tag}_dwout")
    sel = _head_selector(d, FOX_HEAD_DIM)
    delta = rowwise(lambda a, b, s: _dot(a.astype(F32) * b.astype(F32), s, NN, HI), f"{tag}_delta",
                    [do, o], [sel], [(LANE, F32)])[0].T[:nh]
    dq, dbias_q = fox_attn_dq(qkv, do, dcol, drow, lse, delta[:, :, None], f"{tag}_dq")
    dk, dv, dbias_k = fox_attn_dkv(qkv, do, dcol, drow, jnp.swapaxes(lse, 1, 2), delta[:, None, :], f"{tag}_dkv")
    dlogf_q = cumsum_rows(jnp.pad(dbias_q[:, :, 0].T, ((0, 0), (0, LANE - nh))), f"{tag}_dcum_q", reverse=True)
    dlogf_k = cumsum_rows(jnp.pad(dbias_k[:, :, 0].T, ((0, 0), (0, LANE - nh))), f"{tag}_dcum_k", reverse=True)

    def dlogf_fn(gq, gk, x, b):
        r = (gq + gk) * _sigmoid(-(x + b))
        return r, jnp.sum(r, axis=0, keepdims=True)

    dfr, db_f = rowwise(dlogf_fn, f"{tag}_dlogf", [dlogf_q, dlogf_k, fr], [b_f], [(LANE, BF16)], [(1, LANE)])
    dqkv = jnp.concatenate([dq, dk, dv], axis=1)
    dhn = matmul(dfr, w_f, "nt", F32, f"{tag}_dhn_f")
    dhn = matmul(dqkv, w_qkv, "nt", F32, f"{tag}_dhn", residual=dhn)
    dw_qkv = matmul(hn, dqkv, "tn", F32, f"{tag}_dwqkv")
    dw_f = matmul(hn, dfr, "tn", F32, f"{tag}_dwf")
    dh, dnw = norm_bwd_res(h, norm_w, dhn, dout, f"{tag}_dnorm")
    return dh, dnw, dw_qkv, dw_f, db_f, dw_out


CONV_ROWS = 256


def _shift_rows(cur, halo, shift, up=False):
    if shift == 0:
        return cur
    n = cur.shape[0]
    row = lax.broadcasted_iota(jnp.int32, cur.shape, 0)
    if up:
        return jnp.where(row >= n - shift, pltpu.roll(halo, n - shift, 0), pltpu.roll(cur, n - shift, 0))
    return jnp.where(row < shift, pltpu.roll(halo, shift, 0), pltpu.roll(cur, shift, 0))


def _conv_pre(x, halo, w, b):
    acc = b + w[SSD_CONV - 1:SSD_CONV] * x
    for k in range(SSD_CONV - 1):
        acc = acc + w[k:k + 1] * _shift_rows(x, halo, SSD_CONV - 1 - k)
    return acc


def conv_fwd(x, w, b, name):
    t, cw = x.shape
    tb = min(CONV_ROWS, t)

    def body(x_ref, w_ref, b_ref, o_ref, halo):
        @pl.when(pl.program_id(0) == 0)
        def _():
            halo[...] = jnp.zeros_like(halo)

        xv = x_ref[...]
        o_ref[...] = _silu(_conv_pre(xv, halo[...], w_ref[...], b_ref[...]))
        halo[...] = xv

    blk = pl.BlockSpec((tb, cw), lambda i: (i, 0))
    return pl.pallas_call(
        body, name=name, grid=(t // tb,),
        in_specs=[blk, pl.BlockSpec(w.shape, lambda i: (0, 0)), pl.BlockSpec(b.shape, lambda i: (0, 0))],
        out_specs=blk, out_shape=jax.ShapeDtypeStruct((t, cw), F32), scratch_shapes=[pltpu.VMEM((tb, cw), F32)],
        compiler_params=_cparams(("arbitrary",)),
    )(x, w, b)


def conv_bwd(x, w, b, dact, name):
    t, cw = x.shape
    tb = min(CONV_ROWS, t)
    nb = t // tb

    def body_pre(x_ref, w_ref, b_ref, da_ref, dpre_ref, dw_ref, db_ref, halo):
        first = pl.program_id(0) == 0

        @pl.when(first)
        def _():
            halo[...] = jnp.zeros_like(halo)

        xv, hv = x_ref[...], halo[...]
        dpre = da_ref[...] * _dsilu(_conv_pre(xv, hv, w_ref[...], b_ref[...]))
        dpre_ref[...] = dpre
        dw = jnp.concatenate([jnp.sum(dpre * _shift_rows(xv, hv, SSD_CONV - 1 - k), axis=0, keepdims=True)
                              for k in range(SSD_CONV)], axis=0)
        db = jnp.sum(dpre, axis=0, keepdims=True)

        @pl.when(first)
        def _():
            dw_ref[...] = dw
            db_ref[...] = db

        @pl.when(jnp.logical_not(first))
        def _():
            dw_ref[...] += dw
            db_ref[...] += db

        halo[...] = xv

    blk = pl.BlockSpec((tb, cw), lambda i: (i, 0))
    wspec = pl.BlockSpec(w.shape, lambda i: (0, 0))
    bspec = pl.BlockSpec(b.shape, lambda i: (0, 0))
    dpre, dw, db = pl.pallas_call(
        body_pre, name=f"{name}_pre", grid=(nb,), in_specs=[blk, wspec, bspec, blk], out_specs=[blk, wspec, bspec],
        out_shape=[jax.ShapeDtypeStruct((t, cw), F32), jax.ShapeDtypeStruct(w.shape, F32), jax.ShapeDtypeStruct(b.shape, F32)],
        scratch_shapes=[pltpu.VMEM((tb, cw), F32)], compiler_params=_cparams(("arbitrary",)),
    )(x, w, b, dact)

    def body_dx(dp_ref, w_ref, dx_ref, halo):
        @pl.when(pl.program_id(0) == 0)
        def _():
            halo[...] = jnp.zeros_like(halo)

        dp, wv = dp_ref[...], w_ref[...]
        acc = wv[SSD_CONV - 1:SSD_CONV] * dp
        for k in range(SSD_CONV - 1):
            acc = acc + wv[k:k + 1] * _shift_rows(dp, halo[...], SSD_CONV - 1 - k, up=True)
        dx_ref[...] = acc.astype(dx_ref.dtype)
        halo[...] = dp

    rblk = pl.BlockSpec((tb, cw), lambda i: (nb - 1 - i, 0))
    dx = pl.pallas_call(
        body_dx, name=f"{name}_dx", grid=(nb,), in_specs=[rblk, wspec], out_specs=rblk,
        out_shape=jax.ShapeDtypeStruct((t, cw), BF16), scratch_shapes=[pltpu.VMEM((tb, cw), F32)],
        compiler_params=_cparams(("arbitrary",)),
    )(dpre, w)
    return dx, dw, db


def _tri(n, upper=False):
    r = lax.broadcasted_iota(jnp.int32, (n, n), 0)
    c = lax.broadcasted_iota(jnp.int32, (n, n), 1)
    return (c >= r) if upper else (r >= c)


def _ssd_decay(dtc, dtr, a):
    low = _tri(CHUNK)[None]
    cumc = jnp.sum(jnp.where(low, dtr * a, 0.0), axis=2, keepdims=True)
    cumr = jnp.sum(jnp.where(_tri(CHUNK, upper=True)[None], dtc * a, 0.0), axis=1, keepdims=True)
    return cumc, cumr


def _bdot(a, b, nt=False):
    dims = (((2,), (2,)), ((0,), (0,))) if nt else (((2,), (1,)), ((0,), (0,)))
    return lax.dot_general(a.astype(BF16), b.astype(BF16), dims, preferred_element_type=F32)


def _ssd_specs(d, hpg):
    l, n, p = CHUNK, SSD_STATE, SSD_HEAD_DIM
    ng = d // LANE
    x3 = pl.BlockSpec((hpg, l, p), lambda g, c: (g, c, 0))
    bsp = pl.BlockSpec((l, n), lambda g, c: (c, ng + g))
    csp = pl.BlockSpec((l, n), lambda g, c: (c, ng + SSD_GROUPS + g))
    dtc = pl.BlockSpec((hpg, l, 1), lambda g, c: (g, c, 0))
    dtr = pl.BlockSpec((hpg, 1, 1, l), lambda g, c: (g, c, 0, 0))
    per_head = pl.BlockSpec((hpg, 1, 1), lambda g, c: (g, 0, 0))
    return x3, bsp, csp, dtc, dtr, per_head


def ssd_intra_fwd(x3, xbc, dtc, dtr, a_log, d_skip, name):
    nh, t, p = x3.shape
    hpg = nh // SSD_GROUPS
    d = nh * p

    def body(x_ref, b_ref, c_ref, dtc_ref, dtr_ref, al_ref, ds_ref, y_ref):
        a = -jnp.exp(al_ref[...])
        cumc, cumr = _ssd_decay(dtc_ref[...], dtr_ref[:, 0], a)
        mdec = jnp.exp(jnp.where(_tri(CHUNK)[None], cumc - cumr, -jnp.inf))
        g = _dot(c_ref[...].astype(BF16), b_ref[...].astype(BF16), NT)
        xv = x_ref[...]
        y_ref[...] = _bdot(g[None] * mdec, xv * dtc_ref[...]) + xv * ds_ref[...]

    x3s, bsp, csp, dtcs, dtrs, ph = _ssd_specs(d, hpg)
    return pl.pallas_call(
        body, name=name, grid=(SSD_GROUPS, t // CHUNK), in_specs=[x3s, bsp, csp, dtcs, dtrs, ph, ph], out_specs=x3s,
        out_shape=jax.ShapeDtypeStruct((nh, t, p), F32), compiler_params=_cparams(("parallel", "parallel")),
    )(x3, xbc, xbc, dtc, dtr, a_log, d_skip)


def ssd_intra_bwd(x3, xbc, dtc, dtr, a_log, d_skip, dy3, name):
    nh, t, p = x3.shape
    hpg = nh // SSD_GROUPS
    d = nh * p
    l, n = CHUNK, SSD_STATE

    def body(x_ref, b_ref, c_ref, dtc_ref, dtr_ref, al_ref, ds_ref, dy_ref,
             dx_ref, ddt_ref, db_ref, dc_ref, dal_ref, dds_ref):
        first = pl.program_id(1) == 0
        a = -jnp.exp(al_ref[...])
        dtc_v = dtc_ref[...]
        cumc, cumr = _ssd_decay(dtc_v, dtr_ref[:, 0], a)
        low = _tri(l)[None]
        mdec = jnp.exp(jnp.where(low, cumc - cumr, -jnp.inf))
        up = _tri(l, upper=True)[None]
        mdec_t = jnp.exp(jnp.where(up, cumr - cumc, -jnp.inf))
        bv, cv = b_ref[...].astype(BF16), c_ref[...].astype(BF16)
        g = _dot(cv, bv, NT)
        g_t = _dot(bv, cv, NT)
        xv, dy = x_ref[...], dy_ref[...]
        xd = xv * dtc_v
        dw = _bdot(dy, xd, nt=True)
        dw_t = _bdot(xd, dy, nt=True)
        dxd = _bdot(g_t[None] * mdec_t, dy)
        dx_ref[...] = dy * ds_ref[...] + dxd * dtc_v
        dg = jnp.sum(dw * mdec, axis=0)
        dg_t = jnp.sum(dw_t * mdec_t, axis=0)
        dc_ref[0] = _dot(dg.astype(BF16), bv, NN)
        db_ref[0] = _dot(dg_t.astype(BF16), cv, NN)
        e = dw * mdec * g[None]
        e_t = dw_t * mdec_t * g_t[None]
        dcum_r = jnp.sum(e_t, axis=1, keepdims=True) - jnp.sum(e, axis=1, keepdims=True)
        dda = jnp.sum(jnp.where(up, dcum_r, 0.0), axis=2, keepdims=True)
        ddt_ref[...] = jnp.sum(dxd * xv, axis=2, keepdims=True) + dda * a
        dal = jnp.sum(dda * dtc_v, axis=1, keepdims=True) * a
        dds = jnp.sum(jnp.sum(dy * xv, axis=2, keepdims=True), axis=1, keepdims=True)

        @pl.when(first)
        def _():
            dal_ref[...] = dal
            dds_ref[...] = dds

        @pl.when(jnp.logical_not(first))
        def _():
            dal_ref[...] += dal
            dds_ref[...] += dds

    x3s, bsp, csp, dtcs, dtrs, ph = _ssd_specs(d, hpg)
    grp = pl.BlockSpec((1, l, n), lambda g, c: (g, c, 0))
    return pl.pallas_call(
        body, name=name, grid=(SSD_GROUPS, t // l),
        in_specs=[x3s, bsp, csp, dtcs, dtrs, ph, ph, x3s], out_specs=[x3s, dtcs, grp, grp, ph, ph],
        out_shape=[jax.ShapeDtypeStruct((nh, t, p), F32), jax.ShapeDtypeStruct((nh, t, 1), F32),
                   jax.ShapeDtypeStruct((SSD_GROUPS, t, n), F32), jax.ShapeDtypeStruct((SSD_GROUPS, t, n), F32),
                   jax.ShapeDtypeStruct((nh, 1, 1), F32), jax.ShapeDtypeStruct((nh, 1, 1), F32)],
        compiler_params=_cparams(("parallel", "arbitrary")),
    )(x3, xbc, xbc, dtc, dtr, a_log, d_skip, dy3)


def _scan_rows(x, reverse=False):
    n = x.shape[0]
    row = lax.broadcasted_iota(jnp.int32, x.shape, 0)
    s = 1
    while s < n:
        if reverse:
            x = x + jnp.where(row < n - s, pltpu.roll(x, n - s, 0), 0.0)
        else:
            x = x + jnp.where(row >= s, pltpu.roll(x, s, 0), 0.0)
        s *= 2
    return x


def _ssd_state_common(dt, a_lane):
    l = CHUNK
    cum = _scan_rows(dt * a_lane)
    cend = cum[l - 1:l]
    return cum, jnp.exp(cum), jnp.exp(cend - cum), jnp.exp(cend)


def ssd_state_fwd(xbc, dt_lane, a_lane, name):
    t = xbc.shape[0]
    d = dt_lane.shape[1]
    gw = d // SSD_GROUPS
    l, n = CHUNK, SSD_STATE
    nc = t // l
    ng = d // LANE

    def body(x_ref, b_ref, c_ref, dt_ref, a_ref, y_ref, sp_ref, s_s):
        @pl.when(pl.program_id(1) == 0)
        def _():
            s_s[...] = jnp.zeros_like(s_s)

        s_prev = s_s[...]
        for j in range(cb):
            rows = slice(j * l, (j + 1) * l)
            xv, dt = x_ref[rows, :], dt_ref[rows, :]
            cum, ec, te, cd = _ssd_state_common(dt, a_ref[...])
            sp_ref[j] = s_prev.astype(BF16)
            y_ref[rows, :] = _dot(c_ref[rows, :].astype(BF16), s_prev.astype(BF16), NN) * ec
            xt = (xv * dt * te).astype(BF16)
            s_prev = s_prev * cd + _dot(b_ref[rows, :].astype(BF16), xt, TN)
        s_s[...] = s_prev

    cb = _pick(nc, SCAN_CHUNKS)
    xs = pl.BlockSpec((cb * l, gw), lambda g, c: (c, g))
    return pl.pallas_call(
        body, name=name, grid=(SSD_GROUPS, nc // cb),
        in_specs=[xs, pl.BlockSpec((cb * l, n), lambda g, c: (c, ng + g)),
                  pl.BlockSpec((cb * l, n), lambda g, c: (c, ng + SSD_GROUPS + g)),
                  xs, pl.BlockSpec((1, gw), lambda g, c: (0, g))],
        out_specs=[xs, pl.BlockSpec((cb, n, gw), lambda g, c: (c, 0, g))],
        out_shape=[jax.ShapeDtypeStruct((t, d), F32), jax.ShapeDtypeStruct((nc, n, d), BF16)],
        scratch_shapes=[pltpu.VMEM((n, gw), F32)],
        compiler_params=_cparams(("parallel", "arbitrary")),
    )(xbc, xbc, xbc, dt_lane, a_lane)


def ssd_state_bwd(xbc, dt_lane, a_lane, s_prev_all, dy, name):
    t = xbc.shape[0]
    d = dt_lane.shape[1]
    gw = d // SSD_GROUPS
    l, n = CHUNK, SSD_STATE
    nc = t // l
    ng = d // LANE

    def body(x_ref, b_ref, c_ref, dt_ref, a_ref, sp_ref, dy_ref, dx_ref, ddt_ref, db_ref, dc_ref, da_ref, ds_s):
        first = pl.program_id(1) == 0

        @pl.when(first)
        def _():
            ds_s[...] = jnp.zeros_like(ds_s)

        a_lane_v = a_ref[...]
        ds_next = ds_s[...]
        da = jnp.zeros_like(a_lane_v)
        last = lax.broadcasted_iota(jnp.int32, (l, gw), 0) == l - 1
        for j in reversed(range(cb)):
            rows = slice(j * l, (j + 1) * l)
            xv, dt = x_ref[rows, :], dt_ref[rows, :]
            cum, ec, te, cd = _ssd_state_common(dt, a_lane_v)
            bv, cv = b_ref[rows, :].astype(BF16), c_ref[rows, :].astype(BF16)
            s_prev = sp_ref[j]
            dyv = dy_ref[rows, :]
            z = _dot(cv, s_prev, NN)
            dz = (dyv * ec).astype(BF16)
            dc_ref[rows, :] = _dot(dz, s_prev, NT)
            xd = xv * dt
            dxt = _dot(bv, ds_next.astype(BF16), NN)
            db_ref[rows, :] = _dot((xd * te).astype(BF16), ds_next.astype(BF16), NT)
            dcd = jnp.sum(ds_next * s_prev.astype(F32), axis=0, keepdims=True)
            dte_te = dxt * xd * te
            dcum = dyv * z * ec - dte_te + jnp.where(last, jnp.sum(dte_te, axis=0, keepdims=True) + dcd * cd, 0.0)
            dda = _scan_rows(dcum, reverse=True)
            dxd = dxt * te
            dx_ref[rows, :] = dxd * dt
            ddt_ref[rows, :] = dxd * xv + dda * a_lane_v
            da = da + jnp.sum(dda * dt, axis=0, keepdims=True)
            ds_next = ds_next * cd + _dot(cv, dz, TN)
        ds_s[...] = ds_next

        @pl.when(first)
        def _():
            da_ref[...] = da

        @pl.when(jnp.logical_not(first))
        def _():
            da_ref[...] += da

    cb = _pick(nc, SCAN_CHUNKS)
    nb = nc // cb
    rc = lambda c: nb - 1 - c
    xs = pl.BlockSpec((cb * l, gw), lambda g, c: (rc(c), g))
    gs = pl.BlockSpec((cb * l, n), lambda g, c: (rc(c), g))
    return pl.pallas_call(
        body, name=name, grid=(SSD_GROUPS, nb),
        in_specs=[xs, pl.BlockSpec((cb * l, n), lambda g, c: (rc(c), ng + g)),
                  pl.BlockSpec((cb * l, n), lambda g, c: (rc(c), ng + SSD_GROUPS + g)),
                  xs, pl.BlockSpec((1, gw), lambda g, c: (0, g)),
                  pl.BlockSpec((cb, n, gw), lambda g, c: (rc(c), 0, g)), xs],
        out_specs=[xs, xs, gs, gs, pl.BlockSpec((1, gw), lambda g, c: (0, g))],
        out_shape=[jax.ShapeDtypeStruct((t, d), F32), jax.ShapeDtypeStruct((t, d), F32),
                   jax.ShapeDtypeStruct((t, SSD_GROUPS * n), F32), jax.ShapeDtypeStruct((t, SSD_GROUPS * n), F32),
                   jax.ShapeDtypeStruct((1, d), F32)],
        scratch_shapes=[pltpu.VMEM((n, gw), F32)],
        compiler_params=_cparams(("parallel", "arbitrary")),
    )(xbc, xbc, xbc, dt_lane, a_lane, s_prev_all, dy)


SCAN_CHUNKS = (8, 4, 2, 1)


def _hgrn_common(q, fr, lb):
    l = CHUNK
    sig = _sigmoid(fr)
    f = lb + (1.0 - lb) * sig
    kk = 1.0 - f
    cum = _scan_rows(jnp.log(f))
    mid = cum[l // 2 - 1:l // 2]
    cend = cum[l - 1:l]
    qf = _silu(q)
    eq, ek, ee, ec = jnp.exp(cum - mid), jnp.exp(mid - cum), jnp.exp(cend - cum), jnp.exp(cum)
    return sig, f, kk, qf, eq, ek, ee, ec, cend, jnp.exp(cend)


def hgrn_fwd(qfvg, lb, name):
    t = qfvg.shape[0]
    d = lb.shape[1]
    l, kd = CHUNK, HGRN_KDIM
    nh = d // kd
    nc = t // l

    cb = _pick(nc, SCAN_CHUNKS)

    def body(q_ref, f_ref, v_ref, lb_ref, o_ref, sp_ref, s_s):
        @pl.when(pl.program_id(1) == 0)
        def _():
            s_s[...] = jnp.zeros_like(s_s)

        s_prev = s_s[...]
        for j in range(cb):
            rows = slice(j * l, (j + 1) * l)
            sig, f, kk, qf, eq, ek, ee, ec, cend, cd = _hgrn_common(q_ref[rows, :], f_ref[rows, :], lb_ref[...])
            v = v_ref[rows, :].astype(BF16)
            sp_ref[j] = s_prev.astype(BF16)
            att = jnp.where(_tri(l), _dot((qf * eq).astype(BF16), (kk * ek).astype(BF16), NT), 0.0)
            o_ref[rows, :] = _dot(att.astype(BF16), v, NN) + _dot((qf * ec).astype(BF16), s_prev.astype(BF16), NT)
            s_prev = s_prev * cd + _dot(v, (kk * ee).astype(BF16), TN)
        s_s[...] = s_prev

    def col(j):
        return pl.BlockSpec((cb * l, kd), lambda h, c: (c, j * nh + h))

    return pl.pallas_call(
        body, name=name, grid=(nh, nc // cb),
        in_specs=[col(0), col(1), col(2), pl.BlockSpec((1, kd), lambda h, c: (0, h))],
        out_specs=[col(0), pl.BlockSpec((cb, kd, kd), lambda h, c: (c, h, 0))],
        out_shape=[jax.ShapeDtypeStruct((t, d), F32), jax.ShapeDtypeStruct((nc, d, kd), BF16)],
        scratch_shapes=[pltpu.VMEM((kd, kd), F32)],
        compiler_params=_cparams(("parallel", "arbitrary")),
    )(qfvg, qfvg, qfvg, lb)


def hgrn_bwd(qfvg, lb, s_prev_all, do, name):
    t = qfvg.shape[0]
    d = lb.shape[1]
    l, kd = CHUNK, HGRN_KDIM
    nh = d // kd
    nc = t // l
    cb = _pick(nc, SCAN_CHUNKS)
    nb = nc // cb

    def body(q_ref, f_ref, v_ref, lb_ref, sp_ref, do_ref, dq_ref, df_ref, dv_ref, dlb_ref, ds_s):
        first = pl.program_id(1) == 0

        @pl.when(first)
        def _():
            ds_s[...] = jnp.zeros_like(ds_s)

        lbv = lb_ref[...]
        ds_next = ds_s[...]
        dlb = jnp.zeros_like(lbv)
        low = _tri(l)
        row = lax.broadcasted_iota(jnp.int32, (l, kd), 0)
        for j in reversed(range(cb)):
            rows = slice(j * l, (j + 1) * l)
            q = q_ref[rows, :]
            sig, f, kk, qf, eq, ek, ee, ec, cend, cd = _hgrn_common(q, f_ref[rows, :], lbv)
            v = v_ref[rows, :].astype(BF16)
            dov = do_ref[rows, :].astype(BF16)
            s_prev = sp_ref[j]
            ds_b = ds_next.astype(BF16)
            qr, kr, ke, qe = qf * eq, kk * ek, kk * ee, qf * ec
            att = jnp.where(low, _dot(qr.astype(BF16), kr.astype(BF16), NT), 0.0).astype(BF16)
            datt = jnp.where(low, _dot(dov, v, NT), 0.0).astype(BF16)
            dqe = _dot(dov, s_prev, NN)
            dke = _dot(v, ds_b, NN)
            dqr = _dot(datt, kr.astype(BF16), NN)
            dkr = _dot(datt, qr.astype(BF16), TN)
            dv_ref[rows, :] = (_dot(ke.astype(BF16), ds_b, NT) + _dot(att, dov, TN)).astype(dv_ref.dtype)
            dcd = jnp.sum(ds_next * s_prev.astype(F32), axis=0, keepdims=True)
            a_q, a_k, a_e, a_c = dqr * qr, dkr * kr, dke * ke, dqe * qe
            dmid = jnp.sum(a_k - a_q, axis=0, keepdims=True)
            dcend = jnp.sum(a_e, axis=0, keepdims=True) + dcd * cd
            dcum = a_q - a_k - a_e + a_c + jnp.where(row == l // 2 - 1, dmid, 0.0) + jnp.where(row == l - 1, dcend, 0.0)
            dlf = _scan_rows(dcum, reverse=True)
            df = dlf / f - (dkr * ek + dke * ee)
            df_ref[rows, :] = (df * (1.0 - lbv) * sig * (1.0 - sig)).astype(df_ref.dtype)
            dq_ref[rows, :] = ((dqr * eq + dqe * ec) * _dsilu(q)).astype(dq_ref.dtype)
            dlb = dlb + jnp.sum(df * (1.0 - sig), axis=0, keepdims=True)
            ds_next = ds_next * cd + _dot(dov, qe.astype(BF16), TN)
        ds_s[...] = ds_next

        @pl.when(first)
        def _():
            dlb_ref[...] = dlb

        @pl.when(jnp.logical_not(first))
        def _():
            dlb_ref[...] += dlb

    def col(j):
        return pl.BlockSpec((cb * l, kd), lambda h, c: (nb - 1 - c, j * nh + h))

    head = pl.BlockSpec((1, kd), lambda h, c: (0, h))
    return pl.pallas_call(
        body, name=name, grid=(nh, nb),
        in_specs=[col(0), col(1), col(2), head, pl.BlockSpec((cb, kd, kd), lambda h, c: (nb - 1 - c, h, 0)), col(0)],
        out_specs=[col(0), col(0), col(0), head],
        out_shape=[jax.ShapeDtypeStruct((t, d), BF16)] * 3 + [jax.ShapeDtypeStruct((1, d), F32)],
        scratch_shapes=[pltpu.VMEM((kd, kd), F32)],
        compiler_params=_cparams(("parallel", "arbitrary")),
    )(qfvg, qfvg, qfvg, lb, s_prev_all, do)


def _softplus(x):
    return jnp.maximum(x, 0.0) + jnp.log1p(jnp.exp(-jnp.abs(x)))


def _grouped(fn, width, *arrs):
    n = arrs[0].shape[1] // width
    outs = [fn(*[a[:, i * width:(i + 1) * width] for a in arrs]) for i in range(n)]
    if isinstance(outs[0], tuple):
        return tuple(jnp.concatenate([o[j] for o in outs], axis=1) for j in range(len(outs[0])))
    return jnp.concatenate(outs, axis=1)


def mixer_fwd(h, norm_w, wts, conv_w, conv_b, dt_bias, a_log, d_skip, ssd_norm_w, lb, hgrn_norm_w, w_out, tag):
    t, d = h.shape
    nh = d // SSD_HEAD_DIM
    p = SSD_HEAD_DIM
    w_z, w_xbc, w_qfvg, w_dt = wts
    hn = norm_fwd(h, norm_w, f"{tag}_norm")
    z = matmul(hn, w_z, "nn", F32, f"{tag}_z")
    xbc_raw = matmul(hn, w_xbc, "nn", F32, f"{tag}_xbc")
    qfvg = matmul(hn, w_qfvg, "nn", F32, f"{tag}_qfvg")
    dt_raw = matmul(hn, w_dt, "nn", F32, f"{tag}_dt")
    xbc = conv_fwd(xbc_raw, conv_w, conv_b, f"{tag}_conv")
    dt = rowwise(lambda x, b: _softplus(x + b), f"{tag}_softplus", [dt_raw], [dt_bias], [(LANE, F32)])[0]
    dt_h = dt[:, :nh]
    dtr = dt_h.T.reshape(nh, t // CHUNK, 1, CHUNK)
    dtc = dt_h.T[:, :, None]
    dt_lane = jnp.repeat(dt_h, p, axis=1)
    a_lane = jnp.repeat(-jnp.exp(a_log[:, :nh]), p, axis=1)
    al3 = a_log[0, :nh].reshape(nh, 1, 1)
    ds3 = d_skip.reshape(nh, 1, 1)
    x3 = xbc[:, :d].reshape(t, nh, p).transpose(1, 0, 2)
    y3 = ssd_intra_fwd(x3, xbc, dtc, dtr, al3, ds3, f"{tag}_ssd_intra")
    y_off, s_ssd = ssd_state_fwd(xbc, dt_lane, a_lane, f"{tag}_ssd_state")
    y_diag = y3.transpose(1, 0, 2).reshape(t, d)
    o_b, s_hgrn = hgrn_fwd(qfvg, lb, f"{tag}_hgrn")
    gw = d // SSD_GROUPS

    def gate(yd, yo, z_, o, g_, nw_a, nw_b):
        ya = (yd + yo) * _silu(z_)
        ya = _grouped(lambda a, w: a * _rms(a) * w, gw, ya, nw_a)
        yb = _grouped(lambda a, w: a * _rms(a) * w, HGRN_KDIM, o, nw_b) * _silu(g_)
        return jnp.concatenate([ya, yb], axis=1)

    cat = rowwise(gate, f"{tag}_gate", [y_diag, y_off, z, o_b, (qfvg, d, 3)], [ssd_norm_w, hgrn_norm_w], [(2 * d, BF16)], tm=128)[0]
    out = matmul(cat, w_out, "nn", F32, f"{tag}_out", residual=h)
    saved = (h, hn, z, xbc_raw, qfvg, dt_raw, xbc, dtc, dtr, dt_lane, a_lane, al3, ds3, x3, y_diag, y_off, s_ssd, o_b, s_hgrn, cat)
    return out, saved


def mixer_bwd(dout, saved, norm_w, wts, conv_w, conv_b, dt_bias, a_log, ssd_norm_w, lb, hgrn_norm_w, w_out, tag):
    (h, hn, z, xbc_raw, qfvg, dt_raw, xbc, dtc, dtr, dt_lane, a_lane, al3, ds3, x3, y_diag, y_off, s_ssd, o_b, s_hgrn, cat) = saved
    t, d = h.shape
    nh = d // SSD_HEAD_DIM
    p = SSD_HEAD_DIM
    gw = d // SSD_GROUPS
    w_z, w_xbc, w_qfvg, w_dt = wts
    dcat = matmul(dout, w_out, "nt", F32, f"{tag}_dcat")
    dw_out = matmul(cat, dout, "tn", F32, f"{tag}_dwout")

    def gate_bwd(dya_n, dyb_g, yd, yo, z_, o, g_, nw_a, nw_b):
        y = yd + yo
        sz = _silu(z_)
        dya, dnw_a = _grouped(lambda a, w, dy: _norm_bwd(a, w, dy), gw, y * sz, nw_a, dya_n)
        sg = _silu(g_)
        tb = _grouped(lambda a, w: a * _rms(a) * w, HGRN_KDIM, o, nw_b)
        do_, dnw_b = _grouped(lambda a, w, dy: _norm_bwd(a, w, dy), HGRN_KDIM, o, nw_b, dyb_g * sg)
        return dya * sz, dya * y * _dsilu(z_), do_, dyb_g * tb * _dsilu(g_), dnw_a, dnw_b

    dy, dz, do_b, dg, dnw_a, dnw_b = rowwise(
        gate_bwd, f"{tag}_dgate", [(dcat, d, 0), (dcat, d, 1), y_diag, y_off, z, o_b, (qfvg, d, 3)],
        [ssd_norm_w, hgrn_norm_w], [(d, F32), (d, BF16), (d, F32), (d, BF16)], [(1, d), (1, d)], tm=128)
    dq, dfr, dv, dlb = hgrn_bwd(qfvg, lb, s_hgrn, do_b, f"{tag}_dhgrn")
    dqfvg = jnp.concatenate([dq, dfr, dv, dg], axis=1)
    dy3 = dy.reshape(t, nh, p).transpose(1, 0, 2)
    dx3, ddt3, db_a, dc_a, dal_a, dds = ssd_intra_bwd(x3, xbc, dtc, dtr, al3, ds3, dy3, f"{tag}_dssd_intra")
    dx_s, ddt_lane, db_s, dc_s, da_lane = ssd_state_bwd(xbc, dt_lane, a_lane, s_ssd, dy, f"{tag}_dssd_state")
    n = SSD_STATE
    dxbc_act_parts = (dx3.transpose(1, 0, 2).reshape(t, d), dx_s,
                      db_a.transpose(1, 0, 2).reshape(t, SSD_GROUPS * n), db_s,
                      dc_a.transpose(1, 0, 2).reshape(t, SSD_GROUPS * n), dc_s)
    sel = _head_selector(d, p)
    ddt_a = jnp.pad(ddt3[:, :, 0].T, ((0, 0), (0, LANE - nh)))

    def dt_bwd(ddl, dda, x, b, s):
        r = (_dot(ddl, s, NN, HI) + dda) * _sigmoid(x + b)
        return r, jnp.sum(r, axis=0, keepdims=True)

    ddt_raw, ddt_bias = rowwise(dt_bwd, f"{tag}_ddt", [ddt_lane, ddt_a, dt_raw], [dt_bias, sel], [(LANE, BF16)], [(1, LANE)])
    a_pad = -jnp.exp(a_log)
    dal_a_row = jnp.pad(dal_a.reshape(1, nh), ((0, 0), (0, LANE - nh)))
    dalog = rowwise(lambda dal, da, a, s: dal + _dot(da, s, NN, HI) * a, f"{tag}_dalog",
                    [dal_a_row, da_lane, a_pad], [sel], [(LANE, F32)])[0]
    dxbc_act = rowwise(lambda x1, x2, b1, b2, c1, c2: jnp.concatenate([x1 + x2, b1 + b2, c1 + c2], axis=1),
                       f"{tag}_dxbc_sum", list(dxbc_act_parts), [], [(d + 2 * SSD_GROUPS * n, F32)], tm=128)[0]
    dxbc_raw, dconv_w, dconv_b = conv_bwd(xbc_raw, conv_w, conv_b, dxbc_act, f"{tag}_dconv")
    dhn = matmul(dz, w_z, "nt", F32, f"{tag}_dhn_z")
    dhn = matmul(dxbc_raw, w_xbc, "nt", F32, f"{tag}_dhn_xbc", residual=dhn)
    dhn = matmul(dqfvg, w_qfvg, "nt", F32, f"{tag}_dhn_qfvg", residual=dhn)
    dhn = matmul(ddt_raw, w_dt, "nt", F32, f"{tag}_dhn_dt", residual=dhn)
    dw_z = matmul(hn, dz, "tn", F32, f"{tag}_dwz")
    dw_xbc = matmul(hn, dxbc_raw, "tn", F32, f"{tag}_dwxbc")
    dw_qfvg = matmul(hn, dqfvg, "tn", F32, f"{tag}_dwqfvg")
    dw_dt = matmul(hn, ddt_raw, "tn", F32, f"{tag}_dwdt")
    dh, dnw = norm_bwd_res(h, norm_w, dhn, dout, f"{tag}_dnorm")
    grads = dict(mix_norm=dnw, w_z=dw_z, w_xbc=dw_xbc, w_qfvg=dw_qfvg, w_dt=dw_dt, conv_w=dconv_w, conv_b=dconv_b,
                 dt_bias=ddt_bias, a_log=dalog, d_skip=dds.reshape(1, nh), ssd_norm=dnw_a, lb=dlb, hgrn_norm=dnw_b,
                 w_out=dw_out)
    return dh, grads


ANY = pl.BlockSpec(memory_space=pl.ANY)


def _place():
    x, y, c = lax.axis_index("x"), lax.axis_index("y"), lax.axis_index("c")
    chips = [(1 - x, y), (x, 1 - y), (1 - x, 1 - y)]
    return x, y, c, chips


def _rcopy(src, dst, send_sems, recv_sems, k, dev):
    return pltpu.make_async_remote_copy(src_ref=src, dst_ref=dst, send_sem=send_sems.at[k], recv_sem=recv_sems.at[k],
                                        device_id=dev, device_id_type=MESH_ID)


def gather_weights(wsh, name):
    def body(w_ref, o_ref, send_sems, recv_sems):
        x, y, c, chips = _place()
        me = 2 * x + y
        sib = (x, y, 1 - c)
        first = [_rcopy(w_ref.at[c], o_ref.at[c, me], send_sems, recv_sems, j, (cx, cy, c)) for j, (cx, cy) in enumerate(chips)]
        for cp in first:
            cp.start()
        passed = []
        for j, (cx, cy) in enumerate(chips):
            blk = o_ref.at[c, 2 * cx + cy]
            _rcopy(blk, blk, send_sems, recv_sems, j, sib).wait_recv()
            cp = _rcopy(blk, blk, send_sems, recv_sems, 3 + j, sib)
            cp.start()
            passed.append(cp)
        for j, (cx, cy) in enumerate(chips):
            blk = o_ref.at[1 - c, 2 * cx + cy]
            _rcopy(blk, blk, send_sems, recv_sems, 3 + j, sib).wait_recv()
        for cp in first + passed:
            cp.wait_send()

    return pl.pallas_call(
        body, name=name, in_specs=[ANY], out_specs=ANY,
        out_shape=jax.ShapeDtypeStruct((2, 4) + wsh.shape[1:], wsh.dtype),
        scratch_shapes=[pltpu.SemaphoreType.DMA((6,)), pltpu.SemaphoreType.DMA((6,))],
    )(wsh)


def gather_filled(wsh, chip, name):
    return lax.dynamic_update_slice(gather_weights(wsh, name), wsh[:, None], (0, chip) + (0,) * (wsh.ndim - 1))


def exchange_halves(g0, g1, name):
    def body(g0_ref, g1_ref, o_ref, send_sems, recv_sems):
        x, y, c, _ = _place()
        for mine_c, src in ((0, g1_ref), (1, g0_ref)):
            @pl.when(c == mine_c)
            def _():
                cp = _rcopy(src, o_ref, send_sems, recv_sems, 0, (x, y, 1 - c))
                cp.start()
                cp.wait()

    return pl.pallas_call(
        body, name=name, in_specs=[ANY, ANY], out_specs=ANY, out_shape=jax.ShapeDtypeStruct(g0.shape, g0.dtype),
        scratch_shapes=[pltpu.SemaphoreType.DMA((1,)), pltpu.SemaphoreType.DMA((1,))],
    )(g0, g1)


def _rs_tile(r, c):
    if r % 256 == 0:
        return 256, c
    return r, _pick(c, (512, 256, 128))


def add_own_half(g0, g1, other, c_idx, name):
    nchip, r, cdim = g0.shape
    tr, tc = _rs_tile(r, cdim)

    def body(c_ref, a0_ref, a1_ref, b_ref, o_ref):
        own = jnp.where(c_ref[0] == 0, a0_ref[...], a1_ref[...])
        o_ref[...] = (own + b_ref[...]).astype(o_ref.dtype)

    blk = pl.BlockSpec((None, tr, tc), lambda k, i, j, c_ref: (k, i, j))
    grid_spec = pltpu.PrefetchScalarGridSpec(
        num_scalar_prefetch=1, grid=(nchip, r // tr, cdim // tc), in_specs=[blk, blk, blk], out_specs=blk)
    return pl.pallas_call(
        body, name=name, grid_spec=grid_spec, out_shape=jax.ShapeDtypeStruct((nchip, r, cdim), BF16),
        compiler_params=_cparams(("parallel", "parallel", "parallel")),
    )(c_idx, g0, g1, other)


def scatter_to_chips(part, name):
    def body(p_ref, o_ref, send_sems, recv_sems, local_sem):
        x, y, c, chips = _place()
        me = 2 * x + y
        mine = pltpu.make_async_copy(p_ref.at[me], o_ref.at[me], local_sem)
        mine.start()
        cps = [_rcopy(p_ref.at[2 * cx + cy], o_ref.at[me], send_sems, recv_sems, j, (cx, cy, c)) for j, (cx, cy) in enumerate(chips)]
        for cp in cps:
            cp.start()
        for j, (cx, cy) in enumerate(chips):
            blk = o_ref.at[2 * cx + cy]
            _rcopy(blk, blk, send_sems, recv_sems, j, (cx, cy, c)).wait_recv()
        for cp in cps:
            cp.wait_send()
        mine.wait()

    return pl.pallas_call(
        body, name=name, in_specs=[ANY], out_specs=ANY, out_shape=jax.ShapeDtypeStruct(part.shape, part.dtype),
        scratch_shapes=[pltpu.SemaphoreType.DMA((3,)), pltpu.SemaphoreType.DMA((3,)), pltpu.SemaphoreType.DMA],
    )(part)


def sum_chips(slots, name):
    nchip, r, cdim = slots.shape
    tr, tc = _rs_tile(r, cdim)

    def body(s_ref, o_ref):
        acc = s_ref[0].astype(F32)
        for k in range(1, nchip):
            acc = acc + s_ref[k].astype(F32)
        o_ref[...] = acc

    return pl.pallas_call(
        body, name=name, grid=(r // tr, cdim // tc), in_specs=[pl.BlockSpec((nchip, tr, tc), lambda i, j: (0, i, j))],
        out_specs=pl.BlockSpec((tr, tc), lambda i, j: (i, j)), out_shape=jax.ShapeDtypeStruct((r, cdim), F32),
        compiler_params=_cparams(("parallel", "parallel")),
    )(slots)


def share_with_sibling(half, name):
    def body(h_ref, o_ref, send_sems, recv_sems):
        x, y, c, _ = _place()
        cp = _rcopy(h_ref, o_ref.at[c], send_sems, recv_sems, 0, (x, y, 1 - c))
        cp.start()
        blk = o_ref.at[1 - c]
        _rcopy(blk, blk, send_sems, recv_sems, 0, (x, y, 1 - c)).wait_recv()
        cp.wait_send()

    return pl.pallas_call(
        body, name=name, in_specs=[ANY], out_specs=ANY,
        out_shape=jax.ShapeDtypeStruct((2,) + half.shape, half.dtype),
        scratch_shapes=[pltpu.SemaphoreType.DMA((1,)), pltpu.SemaphoreType.DMA((1,))],
    )(half)


def reduce_scatter_grads(g0, g1, c_idx, tag):
    other = exchange_halves(g0, g1, f"rs_exchange_{tag}")
    part = add_own_half(g0, g1, other, c_idx, f"rs_add_{tag}")
    slots = scatter_to_chips(part, f"rs_scatter_{tag}")
    half = sum_chips(slots, f"rs_sum_{tag}")
    return lax.dynamic_update_slice(share_with_sibling(half, f"rs_share_{tag}"), half[None], (c_idx[0], 0, 0))


def allreduce_small(v, name):
    rows, w = v.shape

    def body(v_ref, o_ref, slots, send_sems, recv_sems):
        x, y, c, _ = _place()
        me = 4 * x + 2 * y + c
        slots[me] = v_ref[...]
        cps = []
        for r in range(1, 8):
            dev = (x ^ (r >> 2), y ^ ((r >> 1) & 1), c ^ (r & 1))
            cp = _rcopy(v_ref, slots.at[me], send_sems, recv_sems, r - 1, dev)
            cp.start()
            cps.append(cp)
        for r in range(1, 8):
            blk = slots.at[me ^ r]
            _rcopy(blk, blk, send_sems, recv_sems, r - 1, (x, y, c)).wait_recv()
        for cp in cps:
            cp.wait_send()
        acc = slots[0]
        for k in range(1, 8):
            acc = acc + slots[k]
        o_ref[...] = acc

    vm = pl.BlockSpec(memory_space=pltpu.VMEM)
    return pl.pallas_call(
        body, name=name, in_specs=[vm], out_specs=vm, out_shape=jax.ShapeDtypeStruct((rows, w), F32),
        scratch_shapes=[pltpu.VMEM((8, rows, w), F32), pltpu.SemaphoreType.DMA((7,)), pltpu.SemaphoreType.DMA((7,))],
    )(v)


def adamw(w, g, m, v, name):
    def fn(w_, g_, m_, v_):
        m2 = ADAM_B1 * m_ + (1.0 - ADAM_B1) * g_
        v2 = ADAM_B2 * v_ + (1.0 - ADAM_B2) * (g_ * g_)
        m_hat = m2 / (1.0 - ADAM_B1 ** ADAM_STEP)
        v_hat = v2 / (1.0 - ADAM_B2 ** ADAM_STEP)
        return -ADAM_LR * (m_hat / (jnp.sqrt(v_hat) + ADAM_EPS) + ADAM_WD * w_), m2, v2

    cols = w.shape[1]
    tm = _pick(w.shape[0], tuple(r for r in (256, 128, 64, 32, 16, 8) if r * cols <= 256 * 1024))
    return rowwise(fn, name, [w, g, m, v], [], [(cols, F32)] * 3, tm=tm)


def _pack_rows(arrs, row_align, total_align=1):
    parts, offs, r = [], [], 0
    for a in arrs:
        n = a.size
        nr = -(-n // (FLAT_W * row_align)) * row_align
        parts.append(jnp.pad(a.reshape(-1), (0, nr * FLAT_W - n)).reshape(nr, FLAT_W))
        offs.append(r)
        r += nr
    if r % total_align:
        parts.append(jnp.zeros((-r % total_align, FLAT_W), arrs[0].dtype))
    return jnp.concatenate(parts, axis=0), offs


def _unpack_rows(packed, offs, shapes):
    out = []
    for o, s in zip(offs, shapes):
        n = 1
        for k in s:
            n *= k
        nr = -(-n // FLAT_W)
        out.append(packed[..., o:o + nr, :].reshape(packed.shape[:-2] + (nr * FLAT_W,))[..., :n].reshape(packed.shape[:-2] + tuple(s)))
    return out


BIG = ("ffn1_w_in", "ffn1_w_out", "ab_w_in", "ab_w_out", "fox_w_in", "fox_w_out", "ffn2_w_in", "ffn2_w_out",
       "ple_w_gate", "ple_w_up")
FFN = ("ffn1_w_in", "ffn1_w_out", "ffn2_w_in", "ffn2_w_out")
REST = ("ab_w_in", "ab_w_out", "fox_w_in", "fox_w_out", "ple_w_gate", "ple_w_up")
ROWS_MINOR = ("ffn1_w_in", "ffn2_w_in", "ab_w_in")
COL_SHARDED = ("ffn1_w_in", "ab_w_in", "fox_w_in", "ffn2_w_in", "ple_w_up")
SMALL = ("ffn1_norm", "mix_norm", "ssd_conv_w", "ssd_conv_b", "ssd_dt_bias", "ssd_a_log", "ssd_d", "ssd_norm",
         "hgrn_lb_logits", "hgrn_norm", "fox_b_f", "ffn2_norm", "ple_gate_norm", "ple_norm", "final_norm")
WEIGHTS = ("ffn1_norm", "ffn1_w_in", "ffn1_w_out", "mix_norm", "ab_w_in", "ssd_conv_w", "ssd_conv_b", "ssd_dt_bias",
           "ssd_a_log", "ssd_d", "ssd_norm", "hgrn_lb_logits", "hgrn_norm", "ab_w_out", "fox_w_in", "fox_b_f", "fox_w_out",
           "ffn2_norm", "ffn2_w_in", "ffn2_w_out", "ple_gate_norm", "ple_w_gate", "ple_w_up", "ple_norm", "final_norm")


def _full_from_shards(name, g4):
    if name in COL_SHARDED:
        return jnp.moveaxis(g4, 0, 2).reshape(g4.shape[1], g4.shape[2], 4 * g4.shape[3])
    return jnp.moveaxis(g4, 0, 1).reshape(g4.shape[1], 4 * g4.shape[2], g4.shape[3])


def _shards_from_full(name, full):
    ly, r, c = full.shape
    if name in COL_SHARDED:
        return jnp.moveaxis(full.reshape(ly, r, 4, c // 4), 2, 0)
    return jnp.moveaxis(full.reshape(ly, 4, r // 4, c), 1, 0)


def _pad_to(a, axis, n):
    pad = [(0, 0)] * a.ndim
    pad[axis] = (0, n - a.shape[axis])
    return jnp.pad(a, pad)


def _lane_row(v):
    return _pad_to(v.reshape(1, -1), 1, LANE)


def kernel(x, p, ffn1_norm, ffn1_w_in, ffn1_w_out, mix_norm, ab_w_in, ssd_conv_w, ssd_conv_b, ssd_dt_bias, ssd_a_log, ssd_d, ssd_norm, hgrn_lb_logits, hgrn_norm, ab_w_out, fox_w_in, fox_b_f, fox_w_out, ffn2_norm, ffn2_w_in, ffn2_w_out, ple_gate_norm, ple_w_gate, ple_w_up, ple_norm, final_norm, loss_target, m_ffn1_norm, m_ffn1_w_in, m_ffn1_w_out, m_mix_norm, m_ab_w_in, m_ssd_conv_w, m_ssd_conv_b, m_ssd_dt_bias, m_ssd_a_log, m_ssd_d, m_ssd_norm, m_hgrn_lb_logits, m_hgrn_norm, m_ab_w_out, m_fox_w_in, m_fox_b_f, m_fox_w_out, m_ffn2_norm, m_ffn2_w_in, m_ffn2_w_out, m_ple_gate_norm, m_ple_w_gate, m_ple_w_up, m_ple_norm, m_final_norm, v_ffn1_norm, v_ffn1_w_in, v_ffn1_w_out, v_mix_norm, v_ab_w_in, v_ssd_conv_w, v_ssd_conv_b, v_ssd_dt_bias, v_ssd_a_log, v_ssd_d, v_ssd_norm, v_hgrn_lb_logits, v_hgrn_norm, v_ab_w_out, v_fox_w_in, v_fox_b_f, v_fox_w_out, v_ffn2_norm, v_ffn2_w_in, v_ffn2_w_out, v_ple_gate_norm, v_ple_w_gate, v_ple_w_up, v_ple_norm, v_final_norm):
    given = dict(locals())
    w = {n: given[n] for n in WEIGHTS}
    mom = {n: given["m_" + n] for n in WEIGHTS}
    var = {n: given["v_" + n] for n in WEIGHTS}
    h = x[0]
    t, d = h.shape
    depth = p.shape[0]
    nh_ssd = d // SSD_HEAD_DIM
    nh_fox = d // FOX_HEAD_DIM
    conv_dim = d + 2 * SSD_GROUPS * SSD_STATE
    d_ff = ffn1_w_out.shape[1] * 4
    fp = -(-d_ff // FF_ALIGN) * FF_ALIGN
    xi, yi, ci = lax.axis_index("x"), lax.axis_index("y"), lax.axis_index("c")
    chip = 2 * xi + yi

    assert depth == 2
    ffn_w = {n: gather_filled(w[n].astype(BF16), chip, f"gather_{n}") for n in FFN}
    for n in ("ffn1_w_out", "ffn2_w_out"):
        g = ffn_w[n]
        ffn_w[n] = g.reshape(g.shape[0], 2, 2 * g.shape[2], g.shape[3])
    big_local = [w[n].astype(BF16) for n in REST]
    packed, offs = _pack_rows(big_local, ROW_ALIGN, PACK_ALIGN)
    rows = packed.shape[0]
    gathered = gather_filled(packed.reshape(2, rows // 2, FLAT_W), chip, "gather_rest")
    gathered = jnp.moveaxis(gathered, 0, 1).reshape(4, rows, FLAT_W)
    full = {n: _full_from_shards(n, g4) for n, g4 in zip(REST, _unpack_rows(gathered, offs, [a.shape for a in big_local]))}
    cw_local = ssd_conv_w[0]
    cshard = cw_local.shape[1]
    cw_rows = -(-SSD_CONV * conv_dim // FLAT_W)
    cw_placed = lax.dynamic_update_slice(jnp.zeros((SSD_CONV, conv_dim), F32), cw_local, (0, chip * cshard))
    cw_placed = jnp.where(ci == 0, cw_placed, 0.0)
    cw_packed, _ = _pack_rows([cw_placed], 8)
    conv_w = allreduce_small(cw_packed, "gather_conv_w")[:cw_rows].reshape(-1)[:SSD_CONV * conv_dim].reshape(SSD_CONV, conv_dim)

    ab = full["ab_w_in"][0]
    s = [0, d, d + conv_dim, d + conv_dim + nh_ssd]
    ab_wts = (ab[:, s[0]:s[1]], ab[:, s[1]:s[2]], ab[:, s[3]:], _pad_to(ab[:, s[2]:s[3]], 1, LANE))
    fox = full["fox_w_in"][0]
    fox_qkv, fox_f = fox[:, :3 * d], _pad_to(fox[:, 3 * d:], 1, LANE)
    fox_bias = _lane_row(fox_b_f[0])
    dt_bias, a_log = _lane_row(ssd_dt_bias[0]), _lane_row(ssd_a_log[0])
    lb_soft = rowwise(lambda z: (lambda e: e / jnp.sum(e, axis=0, keepdims=True))(jnp.exp(z - jnp.max(z, axis=0, keepdims=True))),
                      "lb_softmax", [hgrn_lb_logits], [], [(d, F32)], tm=hgrn_lb_logits.shape[0])[0]
    lb = lb_soft[0:1]
    conv_b = ssd_conv_b

    saved = []
    for i in range(depth):
        h, s1 = ffn_fwd(h, ffn1_norm[i:i + 1], ffn_w["ffn1_w_in"], ffn_w["ffn1_w_out"], i, f"l{i}_ffn1")
        if i % 2 == 0:
            h, s2 = mixer_fwd(h, mix_norm[i:i + 1], ab_wts, conv_w, conv_b, dt_bias, a_log, ssd_d[0], ssd_norm, lb, hgrn_norm,
                              full["ab_w_out"][0], f"l{i}_mix")
        else:
            h, s2 = fox_fwd(h, mix_norm[i:i + 1], fox_qkv, fox_f, fox_bias, full["fox_w_out"][0], f"l{i}_fox")
        h, s3 = ffn_fwd(h, ffn2_norm[i:i + 1], ffn_w["ffn2_w_in"], ffn_w["ffn2_w_out"], i, f"l{i}_ffn2")
        h, s4 = ple_fwd(h, p[i, 0], ple_gate_norm[i:i + 1], full["ple_w_gate"][i], full["ple_w_up"][i], ple_norm[i:i + 1], f"l{i}_ple")
        saved.append((s1, s2, s3, s4))
    loss, dh, g_final = loss_head(h, loss_target[0], final_norm.reshape(1, d))

    gb = {n: [None] * w[n].shape[0] for n in BIG}
    gs = {n: [None] * w[n].shape[0] for n in SMALL}
    gs["final_norm"] = g_final[0]

    def ffn_grads(k, i, dw_in, dw_out):
        gb[f"ffn{k}_w_in"][i] = dw_in
        gb[f"ffn{k}_w_out"][i] = dw_out.reshape(4, dw_out.shape[1] // 2, d)

    for i in reversed(range(depth)):
        s1, s2, s3, s4 = saved[i]
        dh, dgn, dwg, dwu, dpn = ple_bwd(dh, s4, p[i, 0], ple_gate_norm[i:i + 1], full["ple_w_gate"][i], ple_norm[i:i + 1], f"l{i}_ple")
        gs["ple_gate_norm"][i], gs["ple_norm"][i] = dgn[0], dpn[0]
        gb["ple_w_gate"][i], gb["ple_w_up"][i] = dwg, dwu
        dh, dnw, dw_in, dw_out = ffn_bwd(dh, s3, ffn2_norm[i:i + 1], ffn_w["ffn2_w_in"], ffn_w["ffn2_w_out"], i, f"l{i}_ffn2")
        gs["ffn2_norm"][i] = dnw[0]
        ffn_grads(2, i, dw_in, dw_out)
        if i % 2 == 0:
            dh, gm = mixer_bwd(dh, s2, mix_norm[i:i + 1], ab_wts, conv_w, conv_b, dt_bias, a_log, ssd_norm, lb, hgrn_norm,
                               full["ab_w_out"][0], f"l{i}_mix")
            gs["mix_norm"][i] = gm["mix_norm"][0]
            q4 = gm["w_qfvg"]
            gb["ab_w_in"][0] = jnp.concatenate([gm["w_z"], gm["w_xbc"], gm["w_dt"][:, :nh_ssd], q4], axis=1)
            gb["ab_w_out"][0] = gm["w_out"]
            gs["ssd_conv_w"][0], gs["ssd_conv_b"][0] = gm["conv_w"], gm["conv_b"][0]
            gs["ssd_dt_bias"][0], gs["ssd_a_log"][0] = gm["dt_bias"][0, :nh_ssd], gm["a_log"][0, :nh_ssd]
            gs["ssd_d"][0], gs["ssd_norm"][0], gs["hgrn_norm"][0] = gm["d_skip"][0], gm["ssd_norm"][0], gm["hgrn_norm"][0]
            dlb = gm["lb"]
        else:
            dh, dnw, dwqkv, dwf, dbf, dwo = fox_bwd(dh, s2, mix_norm[i:i + 1], fox_qkv, fox_f, fox_bias, full["fox_w_out"][0], f"l{i}_fox")
            gs["mix_norm"][i] = dnw[0]
            gb["fox_w_in"][0] = jnp.concatenate([dwqkv, dwf[:, :nh_fox]], axis=1)
            gb["fox_w_out"][0] = dwo
            gs["fox_b_f"][0] = dbf[0, :nh_fox]
        dh, dnw, dw_in, dw_out = ffn_bwd(dh, s1, ffn1_norm[i:i + 1], ffn_w["ffn1_w_in"], ffn_w["ffn1_w_out"], i, f"l{i}_ffn1")
        gs["ffn1_norm"][i] = dnw[0]
        ffn_grads(1, i, dw_in, dw_out)
    grad_x = dh[None]
    first_row = (jnp.arange(hgrn_lb_logits.shape[0]) == 0).astype(F32)[:, None]
    gs["hgrn_lb_logits"] = rowwise(lambda sm, g, e: sm * (e - sm[0:1]) * g, "lb_softmax_bwd",
                                   [lb_soft, jnp.broadcast_to(dlb, lb_soft.shape), jnp.broadcast_to(first_row, lb_soft.shape)],
                                   [], [(d, F32)], tm=lb_soft.shape[0])[0]

    c_idx = ci.reshape(1).astype(jnp.int32)
    g_big = {n: reduce_scatter_grads(gb[n][0], gb[n][1], c_idx, n) for n in FFN}
    g4 = [_shards_from_full(n, jnp.stack(gb[n])) for n in REST]
    g_packed = jnp.stack([_pack_rows([a[k] for a in g4], ROW_ALIGN, PACK_ALIGN)[0] for k in range(4)])
    g_halves = g_packed.reshape(4, 2, rows // 2, FLAT_W)
    g_red = reduce_scatter_grads(g_halves[:, 0], g_halves[:, 1], c_idx, "rest").reshape(rows, FLAT_W)
    g_big.update(zip(REST, _unpack_rows(g_red, offs, [a.shape for a in big_local])))

    small_local = [jnp.stack(gs[n]) if isinstance(gs[n], list) else gs[n] for n in SMALL]
    small_local = [a.reshape(w[n].shape if n != "ssd_conv_w" else (1, SSD_CONV, conv_dim)) for n, a in zip(SMALL, small_local)]
    sp, soffs = _pack_rows(small_local + [loss.reshape(1)], 8)
    sr = allreduce_small(sp, "allreduce_small_grads")
    small_red = _unpack_rows(sr, soffs, [a.shape for a in small_local] + [(1,)])
    loss_total = small_red[-1][0]
    g_small = dict(zip(SMALL, small_red[:-1]))
    g_small["ssd_conv_w"] = lax.dynamic_slice(g_small["ssd_conv_w"], (0, 0, chip * cshard), (1, SSD_CONV, cshard))

    grads, delta, new_m, new_v = {}, {}, {}, {}
    for n in BIG:
        shp = w[n].shape
        grads[n] = g_big[n]
        if n in ROWS_MINOR:
            def view(a, shp=shp):
                return jnp.swapaxes(a, 1, 2).reshape(shp[0] * shp[2], shp[1])

            def back(a, shp=shp):
                return jnp.swapaxes(a.reshape(shp[0], shp[2], shp[1]), 1, 2)
        else:
            def view(a, shp=shp):
                return a.reshape(shp[0] * shp[1], shp[2])

            def back(a, shp=shp):
                return a.reshape(shp)
        dl, m2, v2 = adamw(view(w[n]), view(g_big[n]), view(mom[n]), view(var[n]), f"adamw_{n}")
        delta[n], new_m[n], new_v[n] = back(dl), back(m2), back(v2)
    packs = [_pack_rows([src[n] for n in SMALL], 8) for src in (w, g_small, mom, var)]
    dl, m2, v2 = adamw(packs[0][0], packs[1][0], packs[2][0], packs[3][0], "adamw_small")
    shapes = [w[n].shape for n in SMALL]
    for n, a, b, c_ in zip(SMALL, _unpack_rows(dl, packs[0][1], shapes), _unpack_rows(m2, packs[0][1], shapes), _unpack_rows(v2, packs[0][1], shapes)):
        grads[n], delta[n], new_m[n], new_v[n] = g_small[n], a, b, c_
    return (loss_total, grad_x, *[grads[n] for n in WEIGHTS], *[delta[n] for n in WEIGHTS],
            *[new_m[n] for n in WEIGHTS], *[new_v[n] for n in WEIGHTS])
```

```python
import functools

import jax
import jax.numpy as jnp
from jax import lax
from jax.experimental import pallas as pl
from jax.experimental.pallas import tpu as pltpu

F32 = jnp.float32
BF16 = jnp.bfloat16
HI = lax.Precision.HIGHEST

EPS = 1e-6
CHUNK = 64
SSD_HEAD_DIM = 64
SSD_GROUPS = 4
SSD_STATE = 128
SSD_CONV = 4
HGRN_KDIM = 128
FOX_HEAD_DIM = 128
LANE = 128
FF_ALIGN = 512
FLAT_W = 2048
ROW_ALIGN = 32
PACK_ALIGN = 1024

ADAM_LR = 0.001
ADAM_B1 = 0.9
ADAM_B2 = 0.999
ADAM_EPS = 1e-08
ADAM_WD = 0.01
ADAM_STEP = 10

VMEM_LIMIT = 56 * 1024 * 1024
MESH_ID = pl.DeviceIdType.MESH


def _cparams(sem):
    return pltpu.CompilerParams(dimension_semantics=sem, vmem_limit_bytes=VMEM_LIMIT)


def _pick(n, prefs):
    for t in prefs:
        if n % t == 0:
            return t
    return n


def _dot(a, b, dims, precision=None):
    return lax.dot_general(a, b, (dims, ((), ())), preferred_element_type=F32, precision=precision)


NN = ((1,), (0,))
NT = ((1,), (1,))
TN = ((0,), (0,))


def _sigmoid(x):
    return 1.0 / (1.0 + jnp.exp(-x))


def _silu(x):
    return x * _sigmoid(x)


def _dsilu(x):
    s = _sigmoid(x)
    return s * (1.0 + x * (1.0 - s))


def matmul(a, b, mode, out_dtype, name, scale=None, residual=None):
    if mode == "nn":
        (m, k), (k2, n) = a.shape, b.shape
    elif mode == "nt":
        (m, k), (n, k2) = a.shape, b.shape
    else:
        (k, m), (k2, n) = a.shape, b.shape
    assert k == k2, (a.shape, b.shape, mode)
    tm = _pick(m, (1024, 512, 256, 128))
    tn = _pick(n, (1024, 1408, 512, 256, 128))
    tk = _pick(k, (2048, 1408, 1024, 512, 256, 128))
    nk = k // tk
    dims = {"nn": NN, "nt": NT, "tn": TN}[mode]

    def body(*refs):
        if residual is None:
            a_ref, b_ref, o_ref, acc_ref = refs
            r_ref = None
        else:
            a_ref, b_ref, r_ref, o_ref, acc_ref = refs
        kk = pl.program_id(2)

        @pl.when(kk == 0)
        def _():
            acc_ref[...] = jnp.zeros_like(acc_ref)

        acc_ref[...] += _dot(a_ref[...].astype(BF16), b_ref[...].astype(BF16), dims)

        @pl.when(kk == nk - 1)
        def _():
            r = acc_ref[...]
            if scale is not None:
                r = r * scale
            if r_ref is not None:
                r = r + r_ref[...].astype(F32)
            o_ref[...] = r.astype(o_ref.dtype)

    if mode == "nn":
        a_spec = pl.BlockSpec((tm, tk), lambda i, j, kk: (i, kk))
        b_spec = pl.BlockSpec((tk, tn), lambda i, j, kk: (kk, j))
    elif mode == "nt":
        a_spec = pl.BlockSpec((tm, tk), lambda i, j, kk: (i, kk))
        b_spec = pl.BlockSpec((tn, tk), lambda i, j, kk: (j, kk))
    else:
        a_spec = pl.BlockSpec((tk, tm), lambda i, j, kk: (kk, i))
        b_spec = pl.BlockSpec((tk, tn), lambda i, j, kk: (kk, j))
    o_spec = pl.BlockSpec((tm, tn), lambda i, j, kk: (i, j))
    in_specs = [a_spec, b_spec]
    args = [a, b]
    if residual is not None:
        in_specs.append(o_spec)
        args.append(residual)
    return pl.pallas_call(
        body, name=name, grid=(m // tm, n // tn, nk),
        in_specs=in_specs, out_specs=o_spec,
        out_shape=jax.ShapeDtypeStruct((m, n), out_dtype),
        scratch_shapes=[pltpu.VMEM((tm, tn), F32)],
        compiler_params=_cparams(("parallel", "parallel", "arbitrary")),
    )(*args)


def rowwise(fn, name, rows, consts, outs, accs=(), tm=256):
    rows = [r if isinstance(r, tuple) else (r, r.shape[1], 0) for r in rows]
    t = rows[0][0].shape[0]
    tm = min(tm, t)
    assert t % tm == 0
    n_in = len(rows) + len(consts)
    n_out = len(outs)

    def body(*refs):
        res = fn(*[r[...] for r in refs[:n_in]])
        if not isinstance(res, tuple):
            res = (res,)
        for r, v in zip(refs[n_in:n_in + n_out], res[:n_out]):
            r[...] = v.astype(r.dtype)
        if accs:
            a_refs = refs[n_in + n_out:]
            first = pl.program_id(0) == 0

            @pl.when(first)
            def _():
                for r, v in zip(a_refs, res[n_out:]):
                    r[...] = v

            @pl.when(jnp.logical_not(first))
            def _():
                for r, v in zip(a_refs, res[n_out:]):
                    r[...] += v

    in_specs = [pl.BlockSpec((tm, w), functools.partial(lambda i, cb: (i, cb), cb=cb)) for _, w, cb in rows]
    in_specs += [pl.BlockSpec(c.shape, lambda i: (0, 0)) for c in consts]
    out_specs = [pl.BlockSpec((tm, w), lambda i: (i, 0)) for w, _ in outs]
    out_specs += [pl.BlockSpec(s, lambda i: (0, 0)) for s in accs]
    out_shape = [jax.ShapeDtypeStruct((t, w), d) for w, d in outs]
    out_shape += [jax.ShapeDtypeStruct(s, F32) for s in accs]
    res = pl.pallas_call(
        body, name=name, grid=(t // tm,), in_specs=in_specs, out_specs=out_specs, out_shape=out_shape,
        compiler_params=_cparams(("arbitrary",) if accs else ("parallel",)),
    )(*[r[0] for r in rows], *consts)
    return res


def _rms(x):
    return lax.rsqrt(jnp.mean(x * x, axis=-1, keepdims=True) + EPS)


def _norm_bwd(x, w, dy):
    r = _rms(x)
    xh = x * r
    g = dy * w
    dx = r * (g - xh * jnp.mean(g * xh, axis=-1, keepdims=True))
    return dx, jnp.sum(dy * xh, axis=0, keepdims=True)


def norm_fwd(h, w, name):
    return rowwise(lambda x, w_: x * _rms(x) * w_, name, [h], [w], [(h.shape[1], BF16)])[0]


def norm_bwd_res(h, w, dhn, dres, name):
    def fn(x, dy, dr, w_):
        dx, dw = _norm_bwd(x, w_, dy.astype(F32))
        return dr + dx, dw
    d = h.shape[1]
    return rowwise(fn, name, [h, dhn, dres], [w], [(d, F32)], [(1, d)])


def _mm(name, grid, a, a_spec, b, b_spec, out_shape, o_spec, acc_shape, dims, scale=None, residual=None):
    nk = grid[2]

    def body(*refs):
        a_ref, b_ref = refs[0], refs[1]
        r_ref = refs[2] if residual is not None else None
        o_ref, acc_ref = refs[-2], refs[-1]
        kk = pl.program_id(2)

        @pl.when(kk == 0)
        def _():
            acc_ref[...] = jnp.zeros_like(acc_ref)

        acc_ref[...] += _dot(a_ref[...].astype(BF16), b_ref[...].astype(BF16), dims)

        @pl.when(kk == nk - 1)
        def _():
            r = acc_ref[...]
            if scale is not None:
                r = r * scale
            if r_ref is not None:
                r = r + r_ref[...]
            o_ref[...] = r.astype(o_ref.dtype)

    in_specs, args = [a_spec, b_spec], [a, b]
    if residual is not None:
        in_specs.append(o_spec)
        args.append(residual)
    return pl.pallas_call(
        body, name=name, grid=grid, in_specs=in_specs, out_specs=o_spec, out_shape=out_shape,
        scratch_shapes=[pltpu.VMEM(acc_shape, F32)],
        compiler_params=_cparams(("parallel", "parallel", "arbitrary")),
    )(*args)


def ffn_fwd(h, norm_w, w_in, w_out, layer, tag):
    t, d = h.shape
    ns = w_in.shape[3]
    tm = _pick(t, (1024, 512, 256, 128))
    tk = _pick(d, (1024, 512, 256, 128))
    tn = _pick(d, (1024, 512, 256, 128))
    hn = norm_fwd(h, norm_w, f"{tag}_norm")
    u = _mm(f"{tag}_in", (t // tm, 4, d // tk),
            hn, pl.BlockSpec((tm, tk), lambda i, s, k: (i, k)),
            w_in, pl.BlockSpec((None, None, tk, ns), lambda i, s, k: (layer, s, k, 0)),
            jax.ShapeDtypeStruct((4, t, ns), BF16), pl.BlockSpec((None, tm, ns), lambda i, s, k: (s, i, 0)), (tm, ns), NN)
    u4 = u.reshape(2, 2, t, ns)
    tr = _pick(t, (256, 128))

    def act(u_ref, a_ref):
        a_ref[...] = (_silu(u_ref[0].astype(F32)) * u_ref[1].astype(F32)).astype(a_ref.dtype)

    a = pl.pallas_call(
        act, name=f"{tag}_act", grid=(2, t // tr),
        in_specs=[pl.BlockSpec((2, None, tr, ns), lambda s, i: (0, s, i, 0))],
        out_specs=pl.BlockSpec((None, tr, ns), lambda s, i: (s, i, 0)),
        out_shape=jax.ShapeDtypeStruct((2, t, ns), BF16), compiler_params=_cparams(("parallel", "parallel")),
    )(u4)
    out = _mm(f"{tag}_out", (t // tm, d // tn, 2),
              a, pl.BlockSpec((None, tm, ns), lambda i, j, k: (k, i, 0)),
              w_out, pl.BlockSpec((None, None, ns, tn), lambda i, j, k: (layer, k, 0, j)),
              jax.ShapeDtypeStruct((t, d), F32), pl.BlockSpec((tm, tn), lambda i, j, k: (i, j)), (tm, tn), NN,
              scale=0.5, residual=h)
    return out, (h, hn, u4, a)


def ffn_bwd(dout, saved, norm_w, w_in, w_out, layer, tag):
    h, hn, u4, a = saved
    t, d = h.shape
    ns = w_in.shape[3]
    tm = _pick(t, (1024, 512, 256, 128))
    tk = _pick(d, (1024, 512, 256, 128))
    tn = _pick(d, (1024, 512, 256, 128))
    tkt = _pick(t, (1024, 512, 256, 128))
    th = _pick(d, (512, 256, 128))
    da = _mm(f"{tag}_da", (t // tm, 2, d // tk),
             dout, pl.BlockSpec((tm, tk), lambda i, s, k: (i, k)),
             w_out, pl.BlockSpec((None, None, ns, tk), lambda i, s, k: (layer, s, 0, k)),
             jax.ShapeDtypeStruct((2, t, ns), BF16), pl.BlockSpec((None, tm, ns), lambda i, s, k: (s, i, 0)), (tm, ns), NT,
             scale=0.5)
    tr = _pick(t, (256, 128))

    def dact(da_ref, u_ref, du_ref):
        da_, g, up = da_ref[...].astype(F32), u_ref[0].astype(F32), u_ref[1].astype(F32)
        du_ref[0] = (da_ * up * _dsilu(g)).astype(du_ref.dtype)
        du_ref[1] = (da_ * _silu(g)).astype(du_ref.dtype)

    pair = pl.BlockSpec((2, None, tr, ns), lambda s, i: (0, s, i, 0))
    du = pl.pallas_call(
        dact, name=f"{tag}_dact", grid=(2, t // tr),
        in_specs=[pl.BlockSpec((None, tr, ns), lambda s, i: (s, i, 0)), pair], out_specs=pair,
        out_shape=jax.ShapeDtypeStruct((2, 2, t, ns), BF16), compiler_params=_cparams(("parallel", "parallel")),
    )(da, u4).reshape(4, t, ns)
    dw_out = _mm(f"{tag}_dwout", (2, d // th, t // tkt),
                 a, pl.BlockSpec((None, tkt, ns), lambda s, j, k: (s, k, 0)),
                 dout, pl.BlockSpec((tkt, th), lambda s, j, k: (k, j)),
                 jax.ShapeDtypeStruct((2, ns, d), F32), pl.BlockSpec((None, ns, th), lambda s, j, k: (s, 0, j)), (ns, th), TN,
                 scale=0.5)
    dhn = _mm(f"{tag}_dhn", (t // tm, d // tn, 4),
              du, pl.BlockSpec((None, tm, ns), lambda i, j, k: (k, i, 0)),
              w_in, pl.BlockSpec((None, None, tn, ns), lambda i, j, k: (layer, k, j, 0)),
              jax.ShapeDtypeStruct((t, d), F32), pl.BlockSpec((tm, tn), lambda i, j, k: (i, j)), (tm, tn), NT)
    dw_in = _mm(f"{tag}_dwin", (d // th, 4, t // tkt),
                hn, pl.BlockSpec((tkt, th), lambda i, s, k: (k, i)),
                du, pl.BlockSpec((None, tkt, ns), lambda i, s, k: (s, k, 0)),
                jax.ShapeDtypeStruct((4, d, ns), F32), pl.BlockSpec((None, th, ns), lambda i, s, k: (s, i, 0)), (th, ns), TN)
    dh, dnw = norm_bwd_res(h, norm_w, dhn, dout, f"{tag}_dnorm")
    return dh, dnw, dw_in, dw_out


def ple_fwd(h, p_i, gate_norm_w, w_gate, w_up, post_norm_w, tag):
    d = h.shape[1]
    e0 = matmul(p_i, w_up, "nn", F32, f"{tag}_up")
    hn = norm_fwd(h, gate_norm_w, f"{tag}_norm")
    s = matmul(hn, w_gate, "nn", F32, f"{tag}_gate")
    out = rowwise(lambda x, e, s_, pw: x + e * _rms(e) * pw * _sigmoid(s_), f"{tag}_add",
                  [h, e0, s], [post_norm_w], [(d, F32)])[0]
    return out, (h, hn, e0, s)


def ple_bwd(dout, saved, p_i, gate_norm_w, w_gate, post_norm_w, tag):
    h, hn, e0, s = saved
    d = h.shape[1]

    def fn(dy, e, s_, pw):
        gate = _sigmoid(s_)
        emb = e * _rms(e) * pw
        de0, dpw = _norm_bwd(e, pw, dy * gate)
        return de0, dy * emb * gate * (1.0 - gate), dpw

    de0, ds, dpw = rowwise(fn, f"{tag}_dadd", [dout, e0, s], [post_norm_w], [(d, BF16), (d, BF16)], [(1, d)])
    dw_up = matmul(p_i, de0, "tn", F32, f"{tag}_dwup")
    dhn = matmul(ds, w_gate, "nt", F32, f"{tag}_dhn")
    dw_gate = matmul(hn, ds, "tn", F32, f"{tag}_dwgate")
    dh, dgn = norm_bwd_res(h, gate_norm_w, dhn, dout, f"{tag}_dnorm")
    return dh, dgn, dw_gate, dw_up, dpw


def loss_head(h, target, w):
    d = h.shape[1]

    def fn(x, tgt, w_):
        r = _rms(x)
        err = x * r * w_ - tgt
        dx, dw = _norm_bwd(x, w_, err * (1.0 / d))
        part = 0.5 * jnp.sum(jnp.sum(err * err, axis=-1, keepdims=True), axis=0, keepdims=True) * (1.0 / d)
        return dx, dw, jnp.broadcast_to(part, (1, LANE))

    dh, dw, loss = rowwise(fn, "loss_head", [h, target], [w], [(d, F32)], [(1, d), (1, LANE)])
    return loss[0, 0], dh, dw


def cumsum_rows(x, name, reverse=False):
    t, w = x.shape
    tb = min(256, t)
    nb = t // tb

    def body(x_ref, o_ref, carry):
        @pl.when(pl.program_id(0) == 0)
        def _():
            carry[...] = jnp.zeros_like(carry)

        r = lax.broadcasted_iota(jnp.int32, (tb, tb), 0)
        c = lax.broadcasted_iota(jnp.int32, (tb, tb), 1)
        tri = ((c >= r) if reverse else (c <= r)).astype(F32)
        y = _dot(tri, x_ref[...], NN, HI) + carry[...]
        o_ref[...] = y
        carry[...] = y[0:1, :] if reverse else y[tb - 1:tb, :]

    idx = (lambda i: (nb - 1 - i, 0)) if reverse else (lambda i: (i, 0))
    return pl.pallas_call(
        body, name=name, grid=(nb,), in_specs=[pl.BlockSpec((tb, w), idx)], out_specs=pl.BlockSpec((tb, w), idx),
        out_shape=jax.ShapeDtypeStruct((t, w), F32), scratch_shapes=[pltpu.VMEM((1, w), F32)],
        compiler_params=_cparams(("arbitrary",)),
    )(x)


def _fox_tiles(t):
    return _pick(t, (1024, 512, 256, 128)), _pick(t, (512, 256, 128))


def _fox_pairs(t, resident_is_query):
    tr, ts = _fox_tiles(t)
    rows = []
    for ri in range(t // tr):
        if resident_is_query:
            sis = list(range((ri * tr + tr - 1) // ts + 1))
        else:
            sis = list(range((ri * tr) // ts, t // ts))
        for si in sis:
            q0, k0 = (ri * tr, si * ts) if resident_is_query else (si * ts, ri * tr)
            qn, kn = (tr, ts) if resident_is_query else (ts, tr)
            rows.append((ri, si, si == sis[0], si == sis[-1], q0 < k0 + kn - 1))
    return tuple(jnp.asarray([r[j] for r in rows], jnp.int32) for j in range(5))


def _causal(s, row0, col0, transposed=False):
    r = row0 + lax.broadcasted_iota(jnp.int32, s.shape, 0)
    c = col0 + lax.broadcasted_iota(jnp.int32, s.shape, 1)
    return jnp.where((c >= r) if transposed else (r >= c), s, -jnp.inf)


def _on_diagonal(flag, step):
    @pl.when(flag == 1)
    def _():
        step(True)

    @pl.when(flag == 0)
    def _():
        step(False)


def fox_attn_fwd(qkv, dcol, drow, name):
    t = qkv.shape[0]
    nh = dcol.shape[0]
    tq, tk = _fox_tiles(t)
    pairs = _fox_pairs(t, True)
    scale = FOX_HEAD_DIM ** -0.5

    def body(qs, ks, fs, ls, dg, q_ref, k_ref, v_ref, dc_ref, dr_ref, o_ref, lse_ref, m_s, l_s, acc_s):
        p = pl.program_id(1)

        @pl.when(fs[p] == 1)
        def _():
            m_s[...] = jnp.full_like(m_s, -jnp.inf)
            l_s[...] = jnp.zeros_like(l_s)
            acc_s[...] = jnp.zeros_like(acc_s)

        def step(masked):
            s = _dot(q_ref[...], k_ref[...], NT) * scale + dc_ref[0] - dr_ref[0]
            if masked:
                s = _causal(s, qs[p] * tq, ks[p] * tk)
            m_new = jnp.maximum(m_s[...], jnp.max(s, axis=1, keepdims=True))
            alpha = jnp.exp(m_s[...] - m_new)
            pr = jnp.exp(s - m_new)
            l_s[...] = alpha * l_s[...] + jnp.sum(pr, axis=1, keepdims=True)
            acc_s[...] = alpha * acc_s[...] + _dot(pr.astype(BF16), v_ref[...], NN)
            m_s[...] = m_new

        _on_diagonal(dg[p], step)

        @pl.when(ls[p] == 1)
        def _():
            o_ref[...] = (acc_s[...] / l_s[...]).astype(o_ref.dtype)
            lse_ref[0] = m_s[...] + jnp.log(l_s[...])

    hd = FOX_HEAD_DIM
    qcol = pl.BlockSpec((1, tq, 1), lambda h, p, qs, ks, *_: (h, qs[p], 0))
    grid_spec = pltpu.PrefetchScalarGridSpec(
        num_scalar_prefetch=5, grid=(nh, pairs[0].shape[0]),
        in_specs=[
            pl.BlockSpec((tq, hd), lambda h, p, qs, ks, *_: (qs[p], h)),
            pl.BlockSpec((tk, hd), lambda h, p, qs, ks, *_: (ks[p], nh + h)),
            pl.BlockSpec((tk, hd), lambda h, p, qs, ks, *_: (ks[p], 2 * nh + h)),
            qcol,
            pl.BlockSpec((1, 1, tk), lambda h, p, qs, ks, *_: (h, 0, ks[p])),
        ],
        out_specs=[pl.BlockSpec((tq, hd), lambda h, p, qs, ks, *_: (qs[p], h)), qcol],
        scratch_shapes=[pltpu.VMEM((tq, 1), F32), pltpu.VMEM((tq, 1), F32), pltpu.VMEM((tq, hd), F32)])
    return pl.pallas_call(
        body, name=name, grid_spec=grid_spec,
        out_shape=[jax.ShapeDtypeStruct((t, nh * hd), BF16), jax.ShapeDtypeStruct((nh, t, 1), F32)],
        compiler_params=_cparams(("parallel", "arbitrary")),
    )(*pairs, qkv, qkv, qkv, dcol, drow)


def fox_attn_dq(qkv, do, dcol, drow, lse_col, delta_col, name):
    t = qkv.shape[0]
    nh = dcol.shape[0]
    tq, tk = _fox_tiles(t)
    pairs = _fox_pairs(t, True)
    scale = FOX_HEAD_DIM ** -0.5

    def body(qs, ks, fs, ls, dg, q_ref, k_ref, v_ref, do_ref, dc_ref, dr_ref, lse_ref, dl_ref, dq_ref, db_ref, acc_s, db_s):
        p = pl.program_id(1)

        @pl.when(fs[p] == 1)
        def _():
            acc_s[...] = jnp.zeros_like(acc_s)
            db_s[...] = jnp.zeros_like(db_s)

        def step(masked):
            s = _dot(q_ref[...], k_ref[...], NT) * scale + dc_ref[0] - dr_ref[0]
            if masked:
                s = _causal(s, qs[p] * tq, ks[p] * tk)
            pr = jnp.exp(s - lse_ref[0])
            dp = _dot(do_ref[...], v_ref[...], NT)
            ds = pr * (dp - dl_ref[0])
            acc_s[...] += _dot(ds.astype(BF16), k_ref[...], NN)
            db_s[...] += jnp.sum(ds, axis=1, keepdims=True)

        _on_diagonal(dg[p], step)

        @pl.when(ls[p] == 1)
        def _():
            dq_ref[...] = (acc_s[...] * scale).astype(dq_ref.dtype)
            db_ref[0] = db_s[...]

    hd = FOX_HEAD_DIM
    qblk = pl.BlockSpec((tq, hd), lambda h, p, qs, ks, *_: (qs[p], h))
    qcol = pl.BlockSpec((1, tq, 1), lambda h, p, qs, ks, *_: (h, qs[p], 0))
    grid_spec = pltpu.PrefetchScalarGridSpec(
        num_scalar_prefetch=5, grid=(nh, pairs[0].shape[0]),
        in_specs=[
            qblk,
            pl.BlockSpec((tk, hd), lambda h, p, qs, ks, *_: (ks[p], nh + h)),
            pl.BlockSpec((tk, hd), lambda h, p, qs, ks, *_: (ks[p], 2 * nh + h)),
            qblk, qcol,
            pl.BlockSpec((1, 1, tk), lambda h, p, qs, ks, *_: (h, 0, ks[p])),
            qcol, qcol,
        ],
        out_specs=[qblk, qcol],
        scratch_shapes=[pltpu.VMEM((tq, hd), F32), pltpu.VMEM((tq, 1), F32)])
    return pl.pallas_call(
        body, name=name, grid_spec=grid_spec,
        out_shape=[jax.ShapeDtypeStruct((t, nh * hd), BF16), jax.ShapeDtypeStruct((nh, t, 1), F32)],
        compiler_params=_cparams(("parallel", "arbitrary")),
    )(*pairs, qkv, qkv, qkv, do, dcol, drow, lse_col, delta_col)


def fox_attn_dkv(qkv, do, dcol, drow, lse_row, delta_row, name):
    t = qkv.shape[0]
    nh = dcol.shape[0]
    tk, tq = _fox_tiles(t)
    pairs = _fox_pairs(t, False)
    scale = FOX_HEAD_DIM ** -0.5

    def body(ks, qs, fs, ls, dg, q_ref, k_ref, v_ref, do_ref, dc_ref, dr_ref, lse_ref, dl_ref, dk_ref, dv_ref, db_ref, dk_s, dv_s, db_s):
        p = pl.program_id(1)

        @pl.when(fs[p] == 1)
        def _():
            dk_s[...] = jnp.zeros_like(dk_s)
            dv_s[...] = jnp.zeros_like(dv_s)
            db_s[...] = jnp.zeros_like(db_s)

        def step(masked):
            st = _dot(k_ref[...], q_ref[...], NT) * scale + dr_ref[0] - dc_ref[0]
            if masked:
                st = _causal(st, ks[p] * tk, qs[p] * tq, transposed=True)
            pt = jnp.exp(st - lse_ref[0])
            dv_s[...] += _dot(pt.astype(BF16), do_ref[...], NN)
            dpt = _dot(v_ref[...], do_ref[...], NT)
            dst = pt * (dpt - dl_ref[0])
            dk_s[...] += _dot(dst.astype(BF16), q_ref[...], NN)
            db_s[...] -= jnp.sum(dst, axis=1, keepdims=True)

        _on_diagonal(dg[p], step)

        @pl.when(ls[p] == 1)
        def _():
            dk_ref[...] = (dk_s[...] * scale).astype(dk_ref.dtype)
            dv_ref[...] = dv_s[...].astype(dv_ref.dtype)
            db_ref[0] = db_s[...]

    hd = FOX_HEAD_DIM
    qblk = pl.BlockSpec((tq, hd), lambda h, p, ks, qs, *_: (qs[p], h))
    qrow = pl.BlockSpec((1, 1, tq), lambda h, p, ks, qs, *_: (h, 0, qs[p]))
    kcol = pl.BlockSpec((1, tk, 1), lambda h, p, ks, qs, *_: (h, ks[p], 0))
    kv_out = pl.BlockSpec((tk, hd), lambda h, p, ks, qs, *_: (ks[p], h))
    grid_spec = pltpu.PrefetchScalarGridSpec(
        num_scalar_prefetch=5, grid=(nh, pairs[0].shape[0]),
        in_specs=[
            qblk,
            pl.BlockSpec((tk, hd), lambda h, p, ks, qs, *_: (ks[p], nh + h)),
            pl.BlockSpec((tk, hd), lambda h, p, ks, qs, *_: (ks[p], 2 * nh + h)),
            qblk, kcol, qrow, qrow, qrow,
        ],
        out_specs=[kv_out, kv_out, kcol],
        scratch_shapes=[pltpu.VMEM((tk, hd), F32), pltpu.VMEM((tk, hd), F32), pltpu.VMEM((tk, 1), F32)])
    return pl.pallas_call(
        body, name=name, grid_spec=grid_spec,
        out_shape=[jax.ShapeDtypeStruct((t, nh * hd), BF16), jax.ShapeDtypeStruct((t, nh * hd), BF16),
                   jax.ShapeDtypeStruct((nh, t, 1), F32)],
        compiler_params=_cparams(("parallel", "arbitrary")),
    )(*pairs, qkv, qkv, qkv, do, dcol, drow, lse_row, delta_row)


def _head_selector(d, hd):
    return (jnp.arange(d)[:, None] // hd == jnp.arange(LANE)[None, :]).astype(F32)


def _log_sigmoid(x):
    return jnp.minimum(x, 0.0) - jnp.log1p(jnp.exp(-jnp.abs(x)))


def fox_fwd(h, norm_w, w_qkv, w_f, b_f, w_out, tag):
    t, d = h.shape
    nh = d // FOX_HEAD_DIM
    hn = norm_fwd(h, norm_w, f"{tag}_norm")
    qkv = matmul(hn, w_qkv, "nn", BF16, f"{tag}_qkv")
    fr = matmul(hn, w_f, "nn", F32, f"{tag}_f")
    logf = rowwise(lambda x, b: _log_sigmoid(x + b), f"{tag}_logf", [fr], [b_f], [(LANE, F32)])[0]
    dcum = cumsum_rows(logf, f"{tag}_cum").T[:nh]
    dcol, drow = dcum[:, :, None], dcum[:, None, :]
    o, lse = fox_attn_fwd(qkv, dcol, drow, f"{tag}_attn")
    out = matmul(o, w_out, "nn", F32, f"{tag}_out", residual=h)
    return out, (h, hn, qkv, fr, dcol, drow, o, lse)


def fox_bwd(dout, saved, norm_w, w_qkv, w_f, b_f, w_out, tag):
    h, hn, qkv, fr, dcol, drow, o, lse = saved
    t, d = h.shape
    nh = d // FOX_HEAD_DIM
    do = matmul(dout, w_out, "nt", BF16, f"{tag}_do")
    dw_out = matmul(o, dout, "tn", F32, f"{tag}_dwout")
    sel = _head_selector(d, FOX_HEAD_DIM)
    delta = rowwise(lambda a, b, s: _dot(a.astype(F32) * b.astype(F32), s, NN, HI), f"{tag}_delta",
                    [do, o], [sel], [(LANE, F32)])[0].T[:nh]
    dq, dbias_q = fox_attn_dq(qkv, do, dcol, drow, lse, delta[:, :, None], f"{tag}_dq")
    dk, dv, dbias_k = fox_attn_dkv(qkv, do, dcol, drow, jnp.swapaxes(lse, 1, 2), delta[:, None, :], f"{tag}_dkv")
    dlogf_q = cumsum_rows(jnp.pad(dbias_q[:, :, 0].T, ((0, 0), (0, LANE - nh))), f"{tag}_dcum_q", reverse=True)
    dlogf_k = cumsum_rows(jnp.pad(dbias_k[:, :, 0].T, ((0, 0), (0, LANE - nh))), f"{tag}_dcum_k", reverse=True)

    def dlogf_fn(gq, gk, x, b):
        r = (gq + gk) * _sigmoid(-(x + b))
        return r, jnp.sum(r, axis=0, keepdims=True)

    dfr, db_f = rowwise(dlogf_fn, f"{tag}_dlogf", [dlogf_q, dlogf_k, fr], [b_f], [(LANE, BF16)], [(1, LANE)])
    dqkv = jnp.concatenate([dq, dk, dv], axis=1)
    dhn = matmul(dfr, w_f, "nt", F32, f"{tag}_dhn_f")
    dhn = matmul(dqkv, w_qkv, "nt", F32, f"{tag}_dhn", residual=dhn)
    dw_qkv = matmul(hn, dqkv, "tn", F32, f"{tag}_dwqkv")
    dw_f = matmul(hn, dfr, "tn", F32, f"{tag}_dwf")
    dh, dnw = norm_bwd_res(h, norm_w, dhn, dout, f"{tag}_dnorm")
    return dh, dnw, dw_qkv, dw_f, db_f, dw_out


CONV_ROWS = 256


def _shift_rows(cur, halo, shift, up=False):
    if shift == 0:
        return cur
    n = cur.shape[0]
    row = lax.broadcasted_iota(jnp.int32, cur.shape, 0)
    if up:
        return jnp.where(row >= n - shift, pltpu.roll(halo, n - shift, 0), pltpu.roll(cur, n - shift, 0))
    return jnp.where(row < shift, pltpu.roll(halo, shift, 0), pltpu.roll(cur, shift, 0))


def _conv_pre(x, halo, w, b):
    acc = b + w[SSD_CONV - 1:SSD_CONV] * x
    for k in range(SSD_CONV - 1):
        acc = acc + w[k:k + 1] * _shift_rows(x, halo, SSD_CONV - 1 - k)
    return acc


def conv_fwd(x, w, b, name):
    t, cw = x.shape
    tb = min(CONV_ROWS, t)

    def body(x_ref, w_ref, b_ref, o_ref, halo):
        @pl.when(pl.program_id(0) == 0)
        def _():
            halo[...] = jnp.zeros_like(halo)

        xv = x_ref[...]
        o_ref[...] = _silu(_conv_pre(xv, halo[...], w_ref[...], b_ref[...]))
        halo[...] = xv

    blk = pl.BlockSpec((tb, cw), lambda i: (i, 0))
    return pl.pallas_call(
        body, name=name, grid=(t // tb,),
        in_specs=[blk, pl.BlockSpec(w.shape, lambda i: (0, 0)), pl.BlockSpec(b.shape, lambda i: (0, 0))],
        out_specs=blk, out_shape=jax.ShapeDtypeStruct((t, cw), F32), scratch_shapes=[pltpu.VMEM((tb, cw), F32)],
        compiler_params=_cparams(("arbitrary",)),
    )(x, w, b)


def conv_bwd(x, w, b, dact, name):
    t, cw = x.shape
    tb = min(CONV_ROWS, t)
    nb = t // tb

    def body_pre(x_ref, w_ref, b_ref, da_ref, dpre_ref, dw_ref, db_ref, halo):
        first = pl.program_id(0) == 0

        @pl.when(first)
        def _():
            halo[...] = jnp.zeros_like(halo)

        xv, hv = x_ref[...], halo[...]
        dpre = da_ref[...] * _dsilu(_conv_pre(xv, hv, w_ref[...], b_ref[...]))
        dpre_ref[...] = dpre
        dw = jnp.concatenate([jnp.sum(dpre * _shift_rows(xv, hv, SSD_CONV - 1 - k), axis=0, keepdims=True)
                              for k in range(SSD_CONV)], axis=0)
        db = jnp.sum(dpre, axis=0, keepdims=True)

        @pl.when(first)
        def _():
            dw_ref[...] = dw
            db_ref[...] = db

        @pl.when(jnp.logical_not(first))
        def _():
            dw_ref[...] += dw
            db_ref[...] += db

        halo[...] = xv

    blk = pl.BlockSpec((tb, cw), lambda i: (i, 0))
    wspec = pl.BlockSpec(w.shape, lambda i: (0, 0))
    bspec = pl.BlockSpec(b.shape, lambda i: (0, 0))
    dpre, dw, db = pl.pallas_call(
        body_pre, name=f"{name}_pre", grid=(nb,), in_specs=[blk, wspec, bspec, blk], out_specs=[blk, wspec, bspec],
        out_shape=[jax.ShapeDtypeStruct((t, cw), F32), jax.ShapeDtypeStruct(w.shape, F32), jax.ShapeDtypeStruct(b.shape, F32)],
        scratch_shapes=[pltpu.VMEM((tb, cw), F32)], compiler_params=_cparams(("arbitrary",)),
    )(x, w, b, dact)

    def body_dx(dp_ref, w_ref, dx_ref, halo):
        @pl.when(pl.program_id(0) == 0)
        def _():
            halo[...] = jnp.zeros_like(halo)

        dp, wv = dp_ref[...], w_ref[...]
        acc = wv[SSD_CONV - 1:SSD_CONV] * dp
        for k in range(SSD_CONV - 1):
            acc = acc + wv[k:k + 1] * _shift_rows(dp, halo[...], SSD_CONV - 1 - k, up=True)
        dx_ref[...] = acc.astype(dx_ref.dtype)
        halo[...] = dp

    rblk = pl.BlockSpec((tb, cw), lambda i: (nb - 1 - i, 0))
    dx = pl.pallas_call(
        body_dx, name=f"{name}_dx", grid=(nb,), in_specs=[rblk, wspec], out_specs=rblk,
        out_shape=jax.ShapeDtypeStruct((t, cw), BF16), scratch_shapes=[pltpu.VMEM((tb, cw), F32)],
        compiler_params=_cparams(("arbitrary",)),
    )(dpre, w)
    return dx, dw, db


def _tri(n, upper=False):
    r = lax.broadcasted_iota(jnp.int32, (n, n), 0)
    c = lax.broadcasted_iota(jnp.int32, (n, n), 1)
    return (c >= r) if upper else (r >= c)


def _ssd_decay(dtc, dtr, a):
    low = _tri(CHUNK)[None]
    cumc = jnp.sum(jnp.where(low, dtr * a, 0.0), axis=2, keepdims=True)
    cumr = jnp.sum(jnp.where(_tri(CHUNK, upper=True)[None], dtc * a, 0.0), axis=1, keepdims=True)
    return cumc, cumr


def _bdot(a, b, nt=False):
    dims = (((2,), (2,)), ((0,), (0,))) if nt else (((2,), (1,)), ((0,), (0,)))
    return lax.dot_general(a.astype(BF16), b.astype(BF16), dims, preferred_element_type=F32)


def _ssd_specs(d, hpg):
    l, n, p = CHUNK, SSD_STATE, SSD_HEAD_DIM
    ng = d // LANE
    x3 = pl.BlockSpec((hpg, l, p), lambda g, c: (g, c, 0))
    bsp = pl.BlockSpec((l, n), lambda g, c: (c, ng + g))
    csp = pl.BlockSpec((l, n), lambda g, c: (c, ng + SSD_GROUPS + g))
    dtc = pl.BlockSpec((hpg, l, 1), lambda g, c: (g, c, 0))
    dtr = pl.BlockSpec((hpg, 1, 1, l), lambda g, c: (g, c, 0, 0))
    per_head = pl.BlockSpec((hpg, 1, 1), lambda g, c: (g, 0, 0))
    return x3, bsp, csp, dtc, dtr, per_head


def ssd_intra_fwd(x3, xbc, dtc, dtr, a_log, d_skip, name):
    nh, t, p = x3.shape
    hpg = nh // SSD_GROUPS
    d = nh * p

    def body(x_ref, b_ref, c_ref, dtc_ref, dtr_ref, al_ref, ds_ref, y_ref):
        a = -jnp.exp(al_ref[...])
        cumc, cumr = _ssd_decay(dtc_ref[...], dtr_ref[:, 0], a)
        mdec = jnp.exp(jnp.where(_tri(CHUNK)[None], cumc - cumr, -jnp.inf))
        g = _dot(c_ref[...].astype(BF16), b_ref[...].astype(BF16), NT)
        xv = x_ref[...]
        y_ref[...] = _bdot(g[None] * mdec, xv * dtc_ref[...]) + xv * ds_ref[...]

    x3s, bsp, csp, dtcs, dtrs, ph = _ssd_specs(d, hpg)
    return pl.pallas_call(
        body, name=name, grid=(SSD_GROUPS, t // CHUNK), in_specs=[x3s, bsp, csp, dtcs, dtrs, ph, ph], out_specs=x3s,
        out_shape=jax.ShapeDtypeStruct((nh, t, p), F32), compiler_params=_cparams(("parallel", "parallel")),
    )(x3, xbc, xbc, dtc, dtr, a_log, d_skip)


def ssd_intra_bwd(x3, xbc, dtc, dtr, a_log, d_skip, dy3, name):
    nh, t, p = x3.shape
    hpg = nh // SSD_GROUPS
    d = nh * p
    l, n = CHUNK, SSD_STATE

    def body(x_ref, b_ref, c_ref, dtc_ref, dtr_ref, al_ref, ds_ref, dy_ref,
             dx_ref, ddt_ref, db_ref, dc_ref, dal_ref, dds_ref):
        first = pl.program_id(1) == 0
        a = -jnp.exp(al_ref[...])
        dtc_v = dtc_ref[...]
        cumc, cumr = _ssd_decay(dtc_v, dtr_ref[:, 0], a)
        low = _tri(l)[None]
        mdec = jnp.exp(jnp.where(low, cumc - cumr, -jnp.inf))
        up = _tri(l, upper=True)[None]
        mdec_t = jnp.exp(jnp.where(up, cumr - cumc, -jnp.inf))
        bv, cv = b_ref[...].astype(BF16), c_ref[...].astype(BF16)
        g = _dot(cv, bv, NT)
        g_t = _dot(bv, cv, NT)
        xv, dy = x_ref[...], dy_ref[...]
        xd = xv * dtc_v
        dw = _bdot(dy, xd, nt=True)
        dw_t = _bdot(xd, dy, nt=True)
        dxd = _bdot(g_t[None] * mdec_t, dy)
        dx_ref[...] = dy * ds_ref[...] + dxd * dtc_v
        dg = jnp.sum(dw * mdec, axis=0)
        dg_t = jnp.sum(dw_t * mdec_t, axis=0)
        dc_ref[0] = _dot(dg.astype(BF16), bv, NN)
        db_ref[0] = _dot(dg_t.astype(BF16), cv, NN)
        e = dw * mdec * g[None]
        e_t = dw_t * mdec_t * g_t[None]
        dcum_r = jnp.sum(e_t, axis=1, keepdims=True) - jnp.sum(e, axis=1, keepdims=True)
        dda = jnp.sum(jnp.where(up, dcum_r, 0.0), axis=2, keepdims=True)
        ddt_ref[...] = jnp.sum(dxd * xv, axis=2, keepdims=True) + dda * a
        dal = jnp.sum(dda * dtc_v, axis=1, keepdims=True) * a
        dds = jnp.sum(jnp.sum(dy * xv, axis=2, keepdims=True), axis=1, keepdims=True)

        @pl.when(first)
        def _():
            dal_ref[...] = dal
            dds_ref[...] = dds

        @pl.when(jnp.logical_not(first))
        def _():
            dal_ref[...] += dal
            dds_ref[...] += dds

    x3s, bsp, csp, dtcs, dtrs, ph = _ssd_specs(d, hpg)
    grp = pl.BlockSpec((1, l, n), lambda g, c: (g, c, 0))
    return pl.pallas_call(
        body, name=name, grid=(SSD_GROUPS, t // l),
        in_specs=[x3s, bsp, csp, dtcs, dtrs, ph, ph, x3s], out_specs=[x3s, dtcs, grp, grp, ph, ph],
        out_shape=[jax.ShapeDtypeStruct((nh, t, p), F32), jax.ShapeDtypeStruct((nh, t, 1), F32),
                   jax.ShapeDtypeStruct((SSD_GROUPS, t, n), F32), jax.ShapeDtypeStruct((SSD_GROUPS, t, n), F32),
                   jax.ShapeDtypeStruct((nh, 1, 1), F32), jax.ShapeDtypeStruct((nh, 1, 1), F32)],
        compiler_params=_cparams(("parallel", "arbitrary")),
    )(x3, xbc, xbc, dtc, dtr, a_log, d_skip, dy3)


def _scan_rows(x, reverse=False):
    n = x.shape[0]
    row = lax.broadcasted_iota(jnp.int32, x.shape, 0)
    s = 1
    while s < n:
        if reverse:
            x = x + jnp.where(row < n - s, pltpu.roll(x, n - s, 0), 0.0)
        else:
            x = x + jnp.where(row >= s, pltpu.roll(x, s, 0), 0.0)
        s *= 2
    return x


def _ssd_state_common(dt, a_lane):
    l = CHUNK
    cum = _scan_rows(dt * a_lane)
    cend = cum[l - 1:l]
    return cum, jnp.exp(cum), jnp.exp(cend - cum), jnp.exp(cend)


def ssd_state_fwd(xbc, dt_lane, a_lane, name):
    t = xbc.shape[0]
    d = dt_lane.shape[1]
    gw = d // SSD_GROUPS
    l, n = CHUNK, SSD_STATE
    nc = t // l
    ng = d // LANE

    def body(x_ref, b_ref, c_ref, dt_ref, a_ref, y_ref, sp_ref, s_s):
        @pl.when(pl.program_id(1) == 0)
        def _():
            s_s[...] = jnp.zeros_like(s_s)

        s_prev = s_s[...]
        for j in range(cb):
            rows = slice(j * l, (j + 1) * l)
            xv, dt = x_ref[rows, :], dt_ref[rows, :]
            cum, ec, te, cd = _ssd_state_common(dt, a_ref[...])
            sp_ref[j] = s_prev.astype(BF16)
            y_ref[rows, :] = _dot(c_ref[rows, :].astype(BF16), s_prev.astype(BF16), NN) * ec
            xt = (xv * dt * te).astype(BF16)
            s_prev = s_prev * cd + _dot(b_ref[rows, :].astype(BF16), xt, TN)
        s_s[...] = s_prev

    cb = _pick(nc, SCAN_CHUNKS)
    xs = pl.BlockSpec((cb * l, gw), lambda g, c: (c, g))
    return pl.pallas_call(
        body, name=name, grid=(SSD_GROUPS, nc // cb),
        in_specs=[xs, pl.BlockSpec((cb * l, n), lambda g, c: (c, ng + g)),
                  pl.BlockSpec((cb * l, n), lambda g, c: (c, ng + SSD_GROUPS + g)),
                  xs, pl.BlockSpec((1, gw), lambda g, c: (0, g))],
        out_specs=[xs, pl.BlockSpec((cb, n, gw), lambda g, c: (c, 0, g))],
        out_shape=[jax.ShapeDtypeStruct((t, d), F32), jax.ShapeDtypeStruct((nc, n, d), BF16)],
        scratch_shapes=[pltpu.VMEM((n, gw), F32)],
        compiler_params=_cparams(("parallel", "arbitrary")),
    )(xbc, xbc, xbc, dt_lane, a_lane)


def ssd_state_bwd(xbc, dt_lane, a_lane, s_prev_all, dy, name):
    t = xbc.shape[0]
    d = dt_lane.shape[1]
    gw = d // SSD_GROUPS
    l, n = CHUNK, SSD_STATE
    nc = t // l
    ng = d // LANE

    def body(x_ref, b_ref, c_ref, dt_ref, a_ref, sp_ref, dy_ref, dx_ref, ddt_ref, db_ref, dc_ref, da_ref, ds_s):
        first = pl.program_id(1) == 0

        @pl.when(first)
        def _():
            ds_s[...] = jnp.zeros_like(ds_s)

        a_lane_v = a_ref[...]
        ds_next = ds_s[...]
        da = jnp.zeros_like(a_lane_v)
        last = lax.broadcasted_iota(jnp.int32, (l, gw), 0) == l - 1
        for j in reversed(range(cb)):
            rows = slice(j * l, (j + 1) * l)
            xv, dt = x_ref[rows, :], dt_ref[rows, :]
            cum, ec, te, cd = _ssd_state_common(dt, a_lane_v)
            bv, cv = b_ref[rows, :].astype(BF16), c_ref[rows, :].astype(BF16)
            s_prev = sp_ref[j]
            dyv = dy_ref[rows, :]
            z = _dot(cv, s_prev, NN)
            dz = (dyv * ec).astype(BF16)
            dc_ref[rows, :] = _dot(dz, s_prev, NT)
            xd = xv * dt
            dxt = _dot(bv, ds_next.astype(BF16), NN)
            db_ref[rows, :] = _dot((xd * te).astype(BF16), ds_next.astype(BF16), NT)
            dcd = jnp.sum(ds_next * s_prev.astype(F32), axis=0, keepdims=True)
            dte_te = dxt * xd * te
            dcum = dyv * z * ec - dte_te + jnp.where(last, jnp.sum(dte_te, axis=0, keepdims=True) + dcd * cd, 0.0)
            dda = _scan_rows(dcum, reverse=True)
            dxd = dxt * te
            dx_ref[rows, :] = dxd * dt
            ddt_ref[rows, :] = dxd * xv + dda * a_lane_v
            da = da + jnp.sum(dda * dt, axis=0, keepdims=True)
            ds_next = ds_next * cd + _dot(cv, dz, TN)
        ds_s[...] = ds_next

        @pl.when(first)
        def _():
            da_ref[...] = da

        @pl.when(jnp.logical_not(first))
        def _():
            da_ref[...] += da

    cb = _pick(nc, SCAN_CHUNKS)
    nb = nc // cb
    rc = lambda c: nb - 1 - c
    xs = pl.BlockSpec((cb * l, gw), lambda g, c: (rc(c), g))
    gs = pl.BlockSpec((cb * l, n), lambda g, c: (rc(c), g))
    return pl.pallas_call(
        body, name=name, grid=(SSD_GROUPS, nb),
        in_specs=[xs, pl.BlockSpec((cb * l, n), lambda g, c: (rc(c), ng + g)),
                  pl.BlockSpec((cb * l, n), lambda g, c: (rc(c), ng + SSD_GROUPS + g)),
                  xs, pl.BlockSpec((1, gw), lambda g, c: (0, g)),
                  pl.BlockSpec((cb, n, gw), lambda g, c: (rc(c), 0, g)), xs],
        out_specs=[xs, xs, gs, gs, pl.BlockSpec((1, gw), lambda g, c: (0, g))],
        out_shape=[jax.ShapeDtypeStruct((t, d), F32), jax.ShapeDtypeStruct((t, d), F32),
                   jax.ShapeDtypeStruct((t, SSD_GROUPS * n), F32), jax.ShapeDtypeStruct((t, SSD_GROUPS * n), F32),
                   jax.ShapeDtypeStruct((1, d), F32)],
        scratch_shapes=[pltpu.VMEM((n, gw), F32)],
        compiler_params=_cparams(("parallel", "arbitrary")),
    )(xbc, xbc, xbc, dt_lane, a_lane, s_prev_all, dy)


SCAN_CHUNKS = (8, 4, 2, 1)


def _hgrn_common(q, fr, lb):
    l = CHUNK
    sig = _sigmoid(fr)
    f = lb + (1.0 - lb) * sig
    kk = 1.0 - f
    cum = _scan_rows(jnp.log(f))
    mid = cum[l // 2 - 1:l // 2]
    cend = cum[l - 1:l]
    qf = _silu(q)
    eq, ek, ee, ec = jnp.exp(cum - mid), jnp.exp(mid - cum), jnp.exp(cend - cum), jnp.exp(cum)
    return sig, f, kk, qf, eq, ek, ee, ec, cend, jnp.exp(cend)


def hgrn_fwd(qfvg, lb, name):
    t = qfvg.shape[0]
    d = lb.shape[1]
    l, kd = CHUNK, HGRN_KDIM
    nh = d // kd
    nc = t // l

    cb = _pick(nc, SCAN_CHUNKS)

    def body(q_ref, f_ref, v_ref, lb_ref, o_ref, sp_ref, s_s):
        @pl.when(pl.program_id(1) == 0)
        def _():
            s_s[...] = jnp.zeros_like(s_s)

        s_prev = s_s[...]
        for j in range(cb):
            rows = slice(j * l, (j + 1) * l)
            sig, f, kk, qf, eq, ek, ee, ec, cend, cd = _hgrn_common(q_ref[rows, :], f_ref[rows, :], lb_ref[...])
            v = v_ref[rows, :].astype(BF16)
            sp_ref[j] = s_prev.astype(BF16)
            att = jnp.where(_tri(l), _dot((qf * eq).astype(BF16), (kk * ek).astype(BF16), NT), 0.0)
            o_ref[rows, :] = _dot(att.astype(BF16), v, NN) + _dot((qf * ec).astype(BF16), s_prev.astype(BF16), NT)
            s_prev = s_prev * cd + _dot(v, (kk * ee).astype(BF16), TN)
        s_s[...] = s_prev

    def col(j):
        return pl.BlockSpec((cb * l, kd), lambda h, c: (c, j * nh + h))

    return pl.pallas_call(
        body, name=name, grid=(nh, nc // cb),
        in_specs=[col(0), col(1), col(2), pl.BlockSpec((1, kd), lambda h, c: (0, h))],
        out_specs=[col(0), pl.BlockSpec((cb, kd, kd), lambda h, c: (c, h, 0))],
        out_shape=[jax.ShapeDtypeStruct((t, d), F32), jax.ShapeDtypeStruct((nc, d, kd), BF16)],
        scratch_shapes=[pltpu.VMEM((kd, kd), F32)],
        compiler_params=_cparams(("parallel", "arbitrary")),
    )(qfvg, qfvg, qfvg, lb)


def hgrn_bwd(qfvg, lb, s_prev_all, do, name):
    t = qfvg.shape[0]
    d = lb.shape[1]
    l, kd = CHUNK, HGRN_KDIM
    nh = d // kd
    nc = t // l
    cb = _pick(nc, SCAN_CHUNKS)
    nb = nc // cb

    def body(q_ref, f_ref, v_ref, lb_ref, sp_ref, do_ref, dq_ref, df_ref, dv_ref, dlb_ref, ds_s):
        first = pl.program_id(1) == 0

        @pl.when(first)
        def _():
            ds_s[...] = jnp.zeros_like(ds_s)

        lbv = lb_ref[...]
        ds_next = ds_s[...]
        dlb = jnp.zeros_like(lbv)
        low = _tri(l)
        row = lax.broadcasted_iota(jnp.int32, (l, kd), 0)
        for j in reversed(range(cb)):
            rows = slice(j * l, (j + 1) * l)
            q = q_ref[rows, :]
            sig, f, kk, qf, eq, ek, ee, ec, cend, cd = _hgrn_common(q, f_ref[rows, :], lbv)
            v = v_ref[rows, :].astype(BF16)
            dov = do_ref[rows, :].astype(BF16)
            s_prev = sp_ref[j]
            ds_b = ds_next.astype(BF16)
            qr, kr, ke, qe = qf * eq, kk * ek, kk * ee, qf * ec
            att = jnp.where(low, _dot(qr.astype(BF16), kr.astype(BF16), NT), 0.0).astype(BF16)
            datt = jnp.where(low, _dot(dov, v, NT), 0.0).astype(BF16)
            dqe = _dot(dov, s_prev, NN)
            dke = _dot(v, ds_b, NN)
            dqr = _dot(datt, kr.astype(BF16), NN)
            dkr = _dot(datt, qr.astype(BF16), TN)
            dv_ref[rows, :] = (_dot(ke.astype(BF16), ds_b, NT) + _dot(att, dov, TN)).astype(dv_ref.dtype)
            dcd = jnp.sum(ds_next * s_prev.astype(F32), axis=0, keepdims=True)
            a_q, a_k, a_e, a_c = dqr * qr, dkr * kr, dke * ke, dqe * qe
            dmid = jnp.sum(a_k - a_q, axis=0, keepdims=True)
            dcend = jnp.sum(a_e, axis=0, keepdims=True) + dcd * cd
            dcum = a_q - a_k - a_e + a_c + jnp.where(row == l // 2 - 1, dmid, 0.0) + jnp.where(row == l - 1, dcend, 0.0)
            dlf = _scan_rows(dcum, reverse=True)
            df = dlf / f - (dkr * ek + dke * ee)
            df_ref[rows, :] = (df * (1.0 - lbv) * sig * (1.0 - sig)).astype(df_ref.dtype)
            dq_ref[rows, :] = ((dqr * eq + dqe * ec) * _dsilu(q)).astype(dq_ref.dtype)
            dlb = dlb + jnp.sum(df * (1.0 - sig), axis=0, keepdims=True)
            ds_next = ds_next * cd + _dot(dov, qe.astype(BF16), TN)
        ds_s[...] = ds_next

        @pl.when(first)
        def _():
            dlb_ref[...] = dlb

        @pl.when(jnp.logical_not(first))
        def _():
            dlb_ref[...] += dlb

    def col(j):
        return pl.BlockSpec((cb * l, kd), lambda h, c: (nb - 1 - c, j * nh + h))

    head = pl.BlockSpec((1, kd), lambda h, c: (0, h))
    return pl.pallas_call(
        body, name=name, grid=(nh, nb),
        in_specs=[col(0), col(1), col(2), head, pl.BlockSpec((cb, kd, kd), lambda h, c: (nb - 1 - c, h, 0)), col(0)],
        out_specs=[col(0), col(0), col(0), head],
        out_shape=[jax.ShapeDtypeStruct((t, d), BF16)] * 3 + [jax.ShapeDtypeStruct((1, d), F32)],
        scratch_shapes=[pltpu.VMEM((kd, kd), F32)],
        compiler_params=_cparams(("parallel", "arbitrary")),
    )(qfvg, qfvg, qfvg, lb, s_prev_all, do)


def _softplus(x):
    return jnp.maximum(x, 0.0) + jnp.log1p(jnp.exp(-jnp.abs(x)))


def _grouped(fn, width, *arrs):
    n = arrs[0].shape[1] // width
    outs = [fn(*[a[:, i * width:(i + 1) * width] for a in arrs]) for i in range(n)]
    if isinstance(outs[0], tuple):
        return tuple(jnp.concatenate([o[j] for o in outs], axis=1) for j in range(len(outs[0])))
    return jnp.concatenate(outs, axis=1)


def mixer_fwd(h, norm_w, wts, conv_w, conv_b, dt_bias, a_log, d_skip, ssd_norm_w, lb, hgrn_norm_w, w_out, tag):
    t, d = h.shape
    nh = d // SSD_HEAD_DIM
    p = SSD_HEAD_DIM
    w_z, w_xbc, w_qfvg, w_dt = wts
    hn = norm_fwd(h, norm_w, f"{tag}_norm")
    z = matmul(hn, w_z, "nn", F32, f"{tag}_z")
    xbc_raw = matmul(hn, w_xbc, "nn", F32, f"{tag}_xbc")
    qfvg = matmul(hn, w_qfvg, "nn", F32, f"{tag}_qfvg")
    dt_raw = matmul(hn, w_dt, "nn", F32, f"{tag}_dt")
    xbc = conv_fwd(xbc_raw, conv_w, conv_b, f"{tag}_conv")
    dt = rowwise(lambda x, b: _softplus(x + b), f"{tag}_softplus", [dt_raw], [dt_bias], [(LANE, F32)])[0]
    dt_h = dt[:, :nh]
    dtr = dt_h.T.reshape(nh, t // CHUNK, 1, CHUNK)
    dtc = dt_h.T[:, :, None]
    dt_lane = jnp.repeat(dt_h, p, axis=1)
    a_lane = jnp.repeat(-jnp.exp(a_log[:, :nh]), p, axis=1)
    al3 = a_log[0, :nh].reshape(nh, 1, 1)
    ds3 = d_skip.reshape(nh, 1, 1)
    x3 = xbc[:, :d].reshape(t, nh, p).transpose(1, 0, 2)
    y3 = ssd_intra_fwd(x3, xbc, dtc, dtr, al3, ds3, f"{tag}_ssd_intra")
    y_off, s_ssd = ssd_state_fwd(xbc, dt_lane, a_lane, f"{tag}_ssd_state")
    y_diag = y3.transpose(1, 0, 2).reshape(t, d)
    o_b, s_hgrn = hgrn_fwd(qfvg, lb, f"{tag}_hgrn")
    gw = d // SSD_GROUPS

    def gate(yd, yo, z_, o, g_, nw_a, nw_b):
        ya = (yd + yo) * _silu(z_)
        ya = _grouped(lambda a, w: a * _rms(a) * w, gw, ya, nw_a)
        yb = _grouped(lambda a, w: a * _rms(a) * w, HGRN_KDIM, o, nw_b) * _silu(g_)
        return jnp.concatenate([ya, yb], axis=1)

    cat = rowwise(gate, f"{tag}_gate", [y_diag, y_off, z, o_b, (qfvg, d, 3)], [ssd_norm_w, hgrn_norm_w], [(2 * d, BF16)], tm=128)[0]
    out = matmul(cat, w_out, "nn", F32, f"{tag}_out", residual=h)
    saved = (h, hn, z, xbc_raw, qfvg, dt_raw, xbc, dtc, dtr, dt_lane, a_lane, al3, ds3, x3, y_diag, y_off, s_ssd, o_b, s_hgrn, cat)
    return out, saved


def mixer_bwd(dout, saved, norm_w, wts, conv_w, conv_b, dt_bias, a_log, ssd_norm_w, lb, hgrn_norm_w, w_out, tag):
    (h, hn, z, xbc_raw, qfvg, dt_raw, xbc, dtc, dtr, dt_lane, a_lane, al3, ds3, x3, y_diag, y_off, s_ssd, o_b, s_hgrn, cat) = saved
    t, d = h.shape
    nh = d // SSD_HEAD_DIM
    p = SSD_HEAD_DIM
    gw = d // SSD_GROUPS
    w_z, w_xbc, w_qfvg, w_dt = wts
    dcat = matmul(dout, w_out, "nt", F32, f"{tag}_dcat")
    dw_out = matmul(cat, dout, "tn", F32, f"{tag}_dwout")

    def gate_bwd(dya_n, dyb_g, yd, yo, z_, o, g_, nw_a, nw_b):
        y = yd + yo
        sz = _silu(z_)
        dya, dnw_a = _grouped(lambda a, w, dy: _norm_bwd(a, w, dy), gw, y * sz, nw_a, dya_n)
        sg = _silu(g_)
        tb = _grouped(lambda a, w: a * _rms(a) * w, HGRN_KDIM, o, nw_b)
        do_, dnw_b = _grouped(lambda a, w, dy: _norm_bwd(a, w, dy), HGRN_KDIM, o, nw_b, dyb_g * sg)
        return dya * sz, dya * y * _dsilu(z_), do_, dyb_g * tb * _dsilu(g_), dnw_a, dnw_b

    dy, dz, do_b, dg, dnw_a, dnw_b = rowwise(
        gate_bwd, f"{tag}_dgate", [(dcat, d, 0), (dcat, d, 1), y_diag, y_off, z, o_b, (qfvg, d, 3)],
        [ssd_norm_w, hgrn_norm_w], [(d, F32), (d, BF16), (d, F32), (d, BF16)], [(1, d), (1, d)], tm=128)
    dq, dfr, dv, dlb = hgrn_bwd(qfvg, lb, s_hgrn, do_b, f"{tag}_dhgrn")
    dqfvg = jnp.concatenate([dq, dfr, dv, dg], axis=1)
    dy3 = dy.reshape(t, nh, p).transpose(1, 0, 2)
    dx3, ddt3, db_a, dc_a, dal_a, dds = ssd_intra_bwd(x3, xbc, dtc, dtr, al3, ds3, dy3, f"{tag}_dssd_intra")
    dx_s, ddt_lane, db_s, dc_s, da_lane = ssd_state_bwd(xbc, dt_lane, a_lane, s_ssd, dy, f"{tag}_dssd_state")
    n = SSD_STATE
    dxbc_act_parts = (dx3.transpose(1, 0, 2).reshape(t, d), dx_s,
                      db_a.transpose(1, 0, 2).reshape(t, SSD_GROUPS * n), db_s,
                      dc_a.transpose(1, 0, 2).reshape(t, SSD_GROUPS * n), dc_s)
    sel = _head_selector(d, p)
    ddt_a = jnp.pad(ddt3[:, :, 0].T, ((0, 0), (0, LANE - nh)))

    def dt_bwd(ddl, dda, x, b, s):
        r = (_dot(ddl, s, NN, HI) + dda) * _sigmoid(x + b)
        return r, jnp.sum(r, axis=0, keepdims=True)

    ddt_raw, ddt_bias = rowwise(dt_bwd, f"{tag}_ddt", [ddt_lane, ddt_a, dt_raw], [dt_bias, sel], [(LANE, BF16)], [(1, LANE)])
    a_pad = -jnp.exp(a_log)
    dal_a_row = jnp.pad(dal_a.reshape(1, nh), ((0, 0), (0, LANE - nh)))
    dalog = rowwise(lambda dal, da, a, s: dal + _dot(da, s, NN, HI) * a, f"{tag}_dalog",
                    [dal_a_row, da_lane, a_pad], [sel], [(LANE, F32)])[0]
    dxbc_act = rowwise(lambda x1, x2, b1, b2, c1, c2: jnp.concatenate([x1 + x2, b1 + b2, c1 + c2], axis=1),
                       f"{tag}_dxbc_sum", list(dxbc_act_parts), [], [(d + 2 * SSD_GROUPS * n, F32)], tm=128)[0]
    dxbc_raw, dconv_w, dconv_b = conv_bwd(xbc_raw, conv_w, conv_b, dxbc_act, f"{tag}_dconv")
    dhn = matmul(dz, w_z, "nt", F32, f"{tag}_dhn_z")
    dhn = matmul(dxbc_raw, w_xbc, "nt", F32, f"{tag}_dhn_xbc", residual=dhn)
    dhn = matmul(dqfvg, w_qfvg, "nt", F32, f"{tag}_dhn_qfvg", residual=dhn)
    dhn = matmul(ddt_raw, w_dt, "nt", F32, f"{tag}_dhn_dt", residual=dhn)
    dw_z = matmul(hn, dz, "tn", F32, f"{tag}_dwz")
    dw_xbc = matmul(hn, dxbc_raw, "tn", F32, f"{tag}_dwxbc")
    dw_qfvg = matmul(hn, dqfvg, "tn", F32, f"{tag}_dwqfvg")
    dw_dt = matmul(hn, ddt_raw, "tn", F32, f"{tag}_dwdt")
    dh, dnw = norm_bwd_res(h, norm_w, dhn, dout, f"{tag}_dnorm")
    grads = dict(mix_norm=dnw, w_z=dw_z, w_xbc=dw_xbc, w_qfvg=dw_qfvg, w_dt=dw_dt, conv_w=dconv_w, conv_b=dconv_b,
                 dt_bias=ddt_bias, a_log=dalog, d_skip=dds.reshape(1, nh), ssd_norm=dnw_a, lb=dlb, hgrn_norm=dnw_b,
                 w_out=dw_out)
    return dh, grads


ANY = pl.BlockSpec(memory_space=pl.ANY)


def _place():
    x, y, c = lax.axis_index("x"), lax.axis_index("y"), lax.axis_index("c")
    chips = [(1 - x, y), (x, 1 - y), (1 - x, 1 - y)]
    return x, y, c, chips


def _rcopy(src, dst, send_sems, recv_sems, k, dev):
    return pltpu.make_async_remote_copy(src_ref=src, dst_ref=dst, send_sem=send_sems.at[k], recv_sem=recv_sems.at[k],
                                        device_id=dev, device_id_type=MESH_ID)


def gather_weights(wsh, name):
    def body(w_ref, o_ref, send_sems, recv_sems):
        x, y, c, chips = _place()
        me = 2 * x + y
        sib = (x, y, 1 - c)
        first = [_rcopy(w_ref.at[c], o_ref.at[c, me], send_sems, recv_sems, j, (cx, cy, c)) for j, (cx, cy) in enumerate(chips)]
        for cp in first:
            cp.start()
        passed = []
        for j, (cx, cy) in enumerate(chips):
            blk = o_ref.at[c, 2 * cx + cy]
            _rcopy(blk, blk, send_sems, recv_sems, j, sib).wait_recv()
            cp = _rcopy(blk, blk, send_sems, recv_sems, 3 + j, sib)
            cp.start()
            passed.append(cp)
        for j, (cx, cy) in enumerate(chips):
            blk = o_ref.at[1 - c, 2 * cx + cy]
            _rcopy(blk, blk, send_sems, recv_sems, 3 + j, sib).wait_recv()
        for cp in first + passed:
            cp.wait_send()

    return pl.pallas_call(
        body, name=name, in_specs=[ANY], out_specs=ANY,
        out_shape=jax.ShapeDtypeStruct((2, 4) + wsh.shape[1:], wsh.dtype),
        scratch_shapes=[pltpu.SemaphoreType.DMA((6,)), pltpu.SemaphoreType.DMA((6,))],
    )(wsh)


def gather_filled(wsh, chip, name):
    return lax.dynamic_update_slice(gather_weights(wsh, name), wsh[:, None], (0, chip) + (0,) * (wsh.ndim - 1))


def exchange_halves(g0, g1, name):
    def body(g0_ref, g1_ref, o_ref, send_sems, recv_sems):
        x, y, c, _ = _place()
        for mine_c, src in ((0, g1_ref), (1, g0_ref)):
            @pl.when(c == mine_c)
            def _():
                cp = _rcopy(src, o_ref, send_sems, recv_sems, 0, (x, y, 1 - c))
                cp.start()
                cp.wait()

    return pl.pallas_call(
        body, name=name, in_specs=[ANY, ANY], out_specs=ANY, out_shape=jax.ShapeDtypeStruct(g0.shape, g0.dtype),
        scratch_shapes=[pltpu.SemaphoreType.DMA((1,)), pltpu.SemaphoreType.DMA((1,))],
    )(g0, g1)


def _rs_tile(r, c):
    if r % 256 == 0:
        return 256, c
    return r, _pick(c, (512, 256, 128))


def add_own_half(g0, g1, other, c_idx, name):
    nchip, r, cdim = g0.shape
    tr, tc = _rs_tile(r, cdim)

    def body(c_ref, a0_ref, a1_ref, b_ref, o_ref):
        own = jnp.where(c_ref[0] == 0, a0_ref[...], a1_ref[...])
        o_ref[...] = (own + b_ref[...]).astype(o_ref.dtype)

    blk = pl.BlockSpec((None, tr, tc), lambda k, i, j, c_ref: (k, i, j))
    grid_spec = pltpu.PrefetchScalarGridSpec(
        num_scalar_prefetch=1, grid=(nchip, r // tr, cdim // tc), in_specs=[blk, blk, blk], out_specs=blk)
    return pl.pallas_call(
        body, name=name, grid_spec=grid_spec, out_shape=jax.ShapeDtypeStruct((nchip, r, cdim), BF16),
        compiler_params=_cparams(("parallel", "parallel", "parallel")),
    )(c_idx, g0, g1, other)


def scatter_to_chips(part, name):
    def body(p_ref, o_ref, send_sems, recv_sems, local_sem):
        x, y, c, chips = _place()
        me = 2 * x + y
        mine = pltpu.make_async_copy(p_ref.at[me], o_ref.at[me], local_sem)
        mine.start()
        cps = [_rcopy(p_ref.at[2 * cx + cy], o_ref.at[me], send_sems, recv_sems, j, (cx, cy, c)) for j, (cx, cy) in enumerate(chips)]
        for cp in cps:
            cp.start()
        for j, (cx, cy) in enumerate(chips):
            blk = o_ref.at[2 * cx + cy]
            _rcopy(blk, blk, send_sems, recv_sems, j, (cx, cy, c)).wait_recv()
        for cp in cps:
            cp.wait_send()
        mine.wait()

    return pl.pallas_call(
        body, name=name, in_specs=[ANY], out_specs=ANY, out_shape=jax.ShapeDtypeStruct(part.shape, part.dtype),
        scratch_shapes=[pltpu.SemaphoreType.DMA((3,)), pltpu.SemaphoreType.DMA((3,)), pltpu.SemaphoreType.DMA],
    )(part)


def sum_chips(slots, name):
    nchip, r, cdim = slots.shape
    tr, tc = _rs_tile(r, cdim)

    def body(s_ref, o_ref):
        acc = s_ref[0].astype(F32)
        for k in range(1, nchip):
            acc = acc + s_ref[k].astype(F32)
        o_ref[...] = acc

    return pl.pallas_call(
        body, name=name, grid=(r // tr, cdim // tc), in_specs=[pl.BlockSpec((nchip, tr, tc), lambda i, j: (0, i, j))],
        out_specs=pl.BlockSpec((tr, tc), lambda i, j: (i, j)), out_shape=jax.ShapeDtypeStruct((r, cdim), F32),
        compiler_params=_cparams(("parallel", "parallel")),
    )(slots)


def share_with_sibling(half, name):
    def body(h_ref, o_ref, send_sems, recv_sems):
        x, y, c, _ = _place()
        cp = _rcopy(h_ref, o_ref.at[c], send_sems, recv_sems, 0, (x, y, 1 - c))
        cp.start()
        blk = o_ref.at[1 - c]
        _rcopy(blk, blk, send_sems, recv_sems, 0, (x, y, 1 - c)).wait_recv()
        cp.wait_send()

    return pl.pallas_call(
        body, name=name, in_specs=[ANY], out_specs=ANY,
        out_shape=jax.ShapeDtypeStruct((2,) + half.shape, half.dtype),
        scratch_shapes=[pltpu.SemaphoreType.DMA((1,)), pltpu.SemaphoreType.DMA((1,))],
    )(half)


def reduce_scatter_grads(g0, g1, c_idx, tag):
    other = exchange_halves(g0, g1, f"rs_exchange_{tag}")
    part = add_own_half(g0, g1, other, c_idx, f"rs_add_{tag}")
    slots = scatter_to_chips(part, f"rs_scatter_{tag}")
    half = sum_chips(slots, f"rs_sum_{tag}")
    return lax.dynamic_update_slice(share_with_sibling(half, f"rs_share_{tag}"), half[None], (c_idx[0], 0, 0))


def allreduce_small(v, name):
    rows, w = v.shape

    def body(v_ref, o_ref, slots, send_sems, recv_sems):
        x, y, c, _ = _place()
        me = 4 * x + 2 * y + c
        slots[me] = v_ref[...]
        cps = []
        for r in range(1, 8):
            dev = (x ^ (r >> 2), y ^ ((r >> 1) & 1), c ^ (r & 1))
            cp = _rcopy(v_ref, slots.at[me], send_sems, recv_sems, r - 1, dev)
            cp.start()
            cps.append(cp)
        for r in range(1, 8):
            blk = slots.at[me ^ r]
            _rcopy(blk, blk, send_sems, recv_sems, r - 1, (x, y, c)).wait_recv()
        for cp in cps:
            cp.wait_send()
        acc = slots[0]
        for k in range(1, 8):
            acc = acc + slots[k]
        o_ref[...] = acc

    vm = pl.BlockSpec(memory_space=pltpu.VMEM)
    return pl.pallas_call(
        body, name=name, in_specs=[vm], out_specs=vm, out_shape=jax.ShapeDtypeStruct((rows, w), F32),
        scratch_shapes=[pltpu.VMEM((8, rows, w), F32), pltpu.SemaphoreType.DMA((7,)), pltpu.SemaphoreType.DMA((7,))],
    )(v)


def adamw(w, g, m, v, name):
    def fn(w_, g_, m_, v_):
        m2 = ADAM_B1 * m_ + (1.0 - ADAM_B1) * g_
        v2 = ADAM_B2 * v_ + (1.0 - ADAM_B2) * (g_ * g_)
        m_hat = m2 / (1.0 - ADAM_B1 ** ADAM_STEP)
        v_hat = v2 / (1.0 - ADAM_B2 ** ADAM_STEP)
        return -ADAM_LR * (m_hat / (jnp.sqrt(v_hat) + ADAM_EPS) + ADAM_WD * w_), m2, v2

    rows, cols = w.shape
    tm = _pick(rows, tuple(r for r in (256, 128, 64, 32, 16, 8) if r * cols <= 256 * 1024))
    if tm >= 32 or cols % LANE:
        return rowwise(fn, name, [w, g, m, v], [], [(cols, F32)] * 3, tm=tm)
    tc = _pick(cols, tuple(c for c in (512, 256, 128) if rows * c <= 512 * 1024))

    def body(w_ref, g_ref, m_ref, v_ref, d_ref, m2_ref, v2_ref):
        d_ref[...], m2_ref[...], v2_ref[...] = fn(w_ref[...], g_ref[...], m_ref[...], v_ref[...])

    blk = pl.BlockSpec((rows, tc), lambda j: (0, j))
    return pl.pallas_call(
        body, name=name, grid=(cols // tc,), in_specs=[blk] * 4, out_specs=[blk] * 3,
        out_shape=[jax.ShapeDtypeStruct((rows, cols), F32)] * 3, compiler_params=_cparams(("parallel",)),
    )(w, g, m, v)


def _pack_rows(arrs, row_align, total_align=1):
    parts, offs, r = [], [], 0
    for a in arrs:
        n = a.size
        nr = -(-n // (FLAT_W * row_align)) * row_align
        parts.append(jnp.pad(a.reshape(-1), (0, nr * FLAT_W - n)).reshape(nr, FLAT_W))
        offs.append(r)
        r += nr
    if r % total_align:
        parts.append(jnp.zeros((-r % total_align, FLAT_W), arrs[0].dtype))
    return jnp.concatenate(parts, axis=0), offs


def _unpack_rows(packed, offs, shapes):
    out = []
    for o, s in zip(offs, shapes):
        n = 1
        for k in s:
            n *= k
        nr = -(-n // FLAT_W)
        out.append(packed[..., o:o + nr, :].reshape(packed.shape[:-2] + (nr * FLAT_W,))[..., :n].reshape(packed.shape[:-2] + tuple(s)))
    return out


BIG = ("ffn1_w_in", "ffn1_w_out", "ab_w_in", "ab_w_out", "fox_w_in", "fox_w_out", "ffn2_w_in", "ffn2_w_out",
       "ple_w_gate", "ple_w_up")
FFN = ("ffn1_w_in", "ffn1_w_out", "ffn2_w_in", "ffn2_w_out")
REST = ("ab_w_in", "ab_w_out", "fox_w_in", "fox_w_out", "ple_w_gate", "ple_w_up")
ROWS_MINOR = ("ffn1_w_in", "ffn2_w_in", "ab_w_in")
COL_SHARDED = ("ffn1_w_in", "ab_w_in", "fox_w_in", "ffn2_w_in", "ple_w_up")
SMALL = ("ffn1_norm", "mix_norm", "ssd_conv_w", "ssd_conv_b", "ssd_dt_bias", "ssd_a_log", "ssd_d", "ssd_norm",
         "hgrn_lb_logits", "hgrn_norm", "fox_b_f", "ffn2_norm", "ple_gate_norm", "ple_norm", "final_norm")
WEIGHTS = ("ffn1_norm", "ffn1_w_in", "ffn1_w_out", "mix_norm", "ab_w_in", "ssd_conv_w", "ssd_conv_b", "ssd_dt_bias",
           "ssd_a_log", "ssd_d", "ssd_norm", "hgrn_lb_logits", "hgrn_norm", "ab_w_out", "fox_w_in", "fox_b_f", "fox_w_out",
           "ffn2_norm", "ffn2_w_in", "ffn2_w_out", "ple_gate_norm", "ple_w_gate", "ple_w_up", "ple_norm", "final_norm")


def _full_from_shards(name, g4):
    if name in COL_SHARDED:
        return jnp.moveaxis(g4, 0, 2).reshape(g4.shape[1], g4.shape[2], 4 * g4.shape[3])
    return jnp.moveaxis(g4, 0, 1).reshape(g4.shape[1], 4 * g4.shape[2], g4.shape[3])


def _shards_from_full(name, full):
    ly, r, c = full.shape
    if name in COL_SHARDED:
        return jnp.moveaxis(full.reshape(ly, r, 4, c // 4), 2, 0)
    return jnp.moveaxis(full.reshape(ly, 4, r // 4, c), 1, 0)


def _pad_to(a, axis, n):
    pad = [(0, 0)] * a.ndim
    pad[axis] = (0, n - a.shape[axis])
    return jnp.pad(a, pad)


def _lane_row(v):
    return _pad_to(v.reshape(1, -1), 1, LANE)


def kernel(x, p, ffn1_norm, ffn1_w_in, ffn1_w_out, mix_norm, ab_w_in, ssd_conv_w, ssd_conv_b, ssd_dt_bias, ssd_a_log, ssd_d, ssd_norm, hgrn_lb_logits, hgrn_norm, ab_w_out, fox_w_in, fox_b_f, fox_w_out, ffn2_norm, ffn2_w_in, ffn2_w_out, ple_gate_norm, ple_w_gate, ple_w_up, ple_norm, final_norm, loss_target, m_ffn1_norm, m_ffn1_w_in, m_ffn1_w_out, m_mix_norm, m_ab_w_in, m_ssd_conv_w, m_ssd_conv_b, m_ssd_dt_bias, m_ssd_a_log, m_ssd_d, m_ssd_norm, m_hgrn_lb_logits, m_hgrn_norm, m_ab_w_out, m_fox_w_in, m_fox_b_f, m_fox_w_out, m_ffn2_norm, m_ffn2_w_in, m_ffn2_w_out, m_ple_gate_norm, m_ple_w_gate, m_ple_w_up, m_ple_norm, m_final_norm, v_ffn1_norm, v_ffn1_w_in, v_ffn1_w_out, v_mix_norm, v_ab_w_in, v_ssd_conv_w, v_ssd_conv_b, v_ssd_dt_bias, v_ssd_a_log, v_ssd_d, v_ssd_norm, v_hgrn_lb_logits, v_hgrn_norm, v_ab_w_out, v_fox_w_in, v_fox_b_f, v_fox_w_out, v_ffn2_norm, v_ffn2_w_in, v_ffn2_w_out, v_ple_gate_norm, v_ple_w_gate, v_ple_w_up, v_ple_norm, v_final_norm):
    given = dict(locals())
    w = {n: given[n] for n in WEIGHTS}
    mom = {n: given["m_" + n] for n in WEIGHTS}
    var = {n: given["v_" + n] for n in WEIGHTS}
    h = x[0]
    t, d = h.shape
    depth = p.shape[0]
    nh_ssd = d // SSD_HEAD_DIM
    nh_fox = d // FOX_HEAD_DIM
    conv_dim = d + 2 * SSD_GROUPS * SSD_STATE
    d_ff = ffn1_w_out.shape[1] * 4
    fp = -(-d_ff // FF_ALIGN) * FF_ALIGN
    xi, yi, ci = lax.axis_index("x"), lax.axis_index("y"), lax.axis_index("c")
    chip = 2 * xi + yi

    assert depth == 2
    ffn_w = {n: gather_filled(w[n].astype(BF16), chip, f"gather_{n}") for n in FFN}
    for n in ("ffn1_w_out", "ffn2_w_out"):
        g = ffn_w[n]
        ffn_w[n] = g.reshape(g.shape[0], 2, 2 * g.shape[2], g.shape[3])
    big_local = [w[n].astype(BF16) for n in REST]
    packed, offs = _pack_rows(big_local, ROW_ALIGN, PACK_ALIGN)
    rows = packed.shape[0]
    gathered = gather_filled(packed.reshape(2, rows // 2, FLAT_W), chip, "gather_rest")
    gathered = jnp.moveaxis(gathered, 0, 1).reshape(4, rows, FLAT_W)
    full = {n: _full_from_shards(n, g4) for n, g4 in zip(REST, _unpack_rows(gathered, offs, [a.shape for a in big_local]))}
    cw_local = ssd_conv_w[0]
    cshard = cw_local.shape[1]
    cw_rows = -(-SSD_CONV * conv_dim // FLAT_W)
    cw_placed = lax.dynamic_update_slice(jnp.zeros((SSD_CONV, conv_dim), F32), cw_local, (0, chip * cshard))
    cw_placed = jnp.where(ci == 0, cw_placed, 0.0)
    cw_packed, _ = _pack_rows([cw_placed], 8)
    conv_w = allreduce_small(cw_packed, "gather_conv_w")[:cw_rows].reshape(-1)[:SSD_CONV * conv_dim].reshape(SSD_CONV, conv_dim)

    ab = full["ab_w_in"][0]
    s = [0, d, d + conv_dim, d + conv_dim + nh_ssd]
    ab_wts = (ab[:, s[0]:s[1]], ab[:, s[1]:s[2]], ab[:, s[3]:], _pad_to(ab[:, s[2]:s[3]], 1, LANE))
    fox = full["fox_w_in"][0]
    fox_qkv, fox_f = fox[:, :3 * d], _pad_to(fox[:, 3 * d:], 1, LANE)
    fox_bias = _lane_row(fox_b_f[0])
    dt_bias, a_log = _lane_row(ssd_dt_bias[0]), _lane_row(ssd_a_log[0])
    lb_soft = rowwise(lambda z: (lambda e: e / jnp.sum(e, axis=0, keepdims=True))(jnp.exp(z - jnp.max(z, axis=0, keepdims=True))),
                      "lb_softmax", [hgrn_lb_logits], [], [(d, F32)], tm=hgrn_lb_logits.shape[0])[0]
    lb = lb_soft[0:1]
    conv_b = ssd_conv_b

    saved = []
    for i in range(depth):
        h, s1 = ffn_fwd(h, ffn1_norm[i:i + 1], ffn_w["ffn1_w_in"], ffn_w["ffn1_w_out"], i, f"l{i}_ffn1")
        if i % 2 == 0:
            h, s2 = mixer_fwd(h, mix_norm[i:i + 1], ab_wts, conv_w, conv_b, dt_bias, a_log, ssd_d[0], ssd_norm, lb, hgrn_norm,
                              full["ab_w_out"][0], f"l{i}_mix")
        else:
            h, s2 = fox_fwd(h, mix_norm[i:i + 1], fox_qkv, fox_f, fox_bias, full["fox_w_out"][0], f"l{i}_fox")
        h, s3 = ffn_fwd(h, ffn2_norm[i:i + 1], ffn_w["ffn2_w_in"], ffn_w["ffn2_w_out"], i, f"l{i}_ffn2")
        h, s4 = ple_fwd(h, p[i, 0], ple_gate_norm[i:i + 1], full["ple_w_gate"][i], full["ple_w_up"][i], ple_norm[i:i + 1], f"l{i}_ple")
        saved.append((s1, s2, s3, s4))
    loss, dh, g_final = loss_head(h, loss_target[0], final_norm.reshape(1, d))

    gb = {n: [None] * w[n].shape[0] for n in BIG}
    gs = {n: [None] * w[n].shape[0] for n in SMALL}
    gs["final_norm"] = g_final[0]

    def ffn_grads(k, i, dw_in, dw_out):
        gb[f"ffn{k}_w_in"][i] = dw_in
        gb[f"ffn{k}_w_out"][i] = dw_out.reshape(4, dw_out.shape[1] // 2, d)

    for i in reversed(range(depth)):
        s1, s2, s3, s4 = saved[i]
        dh, dgn, dwg, dwu, dpn = ple_bwd(dh, s4, p[i, 0], ple_gate_norm[i:i + 1], full["ple_w_gate"][i], ple_norm[i:i + 1], f"l{i}_ple")
        gs["ple_gate_norm"][i], gs["ple_norm"][i] = dgn[0], dpn[0]
        gb["ple_w_gate"][i], gb["ple_w_up"][i] = dwg, dwu
        dh, dnw, dw_in, dw_out = ffn_bwd(dh, s3, ffn2_norm[i:i + 1], ffn_w["ffn2_w_in"], ffn_w["ffn2_w_out"], i, f"l{i}_ffn2")
        gs["ffn2_norm"][i] = dnw[0]
        ffn_grads(2, i, dw_in, dw_out)
        if i % 2 == 0:
            dh, gm = mixer_bwd(dh, s2, mix_norm[i:i + 1], ab_wts, conv_w, conv_b, dt_bias, a_log, ssd_norm, lb, hgrn_norm,
                               full["ab_w_out"][0], f"l{i}_mix")
            gs["mix_norm"][i] = gm["mix_norm"][0]
            q4 = gm["w_qfvg"]
            gb["ab_w_in"][0] = jnp.concatenate([gm["w_z"], gm["w_xbc"], gm["w_dt"][:, :nh_ssd], q4], axis=1)
            gb["ab_w_out"][0] = gm["w_out"]
            gs["ssd_conv_w"][0], gs["ssd_conv_b"][0] = gm["conv_w"], gm["conv_b"][0]
            gs["ssd_dt_bias"][0], gs["ssd_a_log"][0] = gm["dt_bias"][0, :nh_ssd], gm["a_log"][0, :nh_ssd]
            gs["ssd_d"][0], gs["ssd_norm"][0], gs["hgrn_norm"][0] = gm["d_skip"][0], gm["ssd_norm"][0], gm["hgrn_norm"][0]
            dlb = gm["lb"]
        else:
            dh, dnw, dwqkv, dwf, dbf, dwo = fox_bwd(dh, s2, mix_norm[i:i + 1], fox_qkv, fox_f, fox_bias, full["fox_w_out"][0], f"l{i}_fox")
            gs["mix_norm"][i] = dnw[0]
            gb["fox_w_in"][0] = jnp.concatenate([dwqkv, dwf[:, :nh_fox]], axis=1)
            gb["fox_w_out"][0] = dwo
            gs["fox_b_f"][0] = dbf[0, :nh_fox]
        dh, dnw, dw_in, dw_out = ffn_bwd(dh, s1, ffn1_norm[i:i + 1], ffn_w["ffn1_w_in"], ffn_w["ffn1_w_out"], i, f"l{i}_ffn1")
        gs["ffn1_norm"][i] = dnw[0]
        ffn_grads(1, i, dw_in, dw_out)
    grad_x = dh[None]
    first_row = (jnp.arange(hgrn_lb_logits.shape[0]) == 0).astype(F32)[:, None]
    gs["hgrn_lb_logits"] = rowwise(lambda sm, g, e: sm * (e - sm[0:1]) * g, "lb_softmax_bwd",
                                   [lb_soft, jnp.broadcast_to(dlb, lb_soft.shape), jnp.broadcast_to(first_row, lb_soft.shape)],
                                   [], [(d, F32)], tm=lb_soft.shape[0])[0]

    c_idx = ci.reshape(1).astype(jnp.int32)
    g_big = {n: reduce_scatter_grads(gb[n][0], gb[n][1], c_idx, n) for n in FFN}
    g4 = [_shards_from_full(n, jnp.stack(gb[n])) for n in REST]
    g_packed = jnp.stack([_pack_rows([a[k] for a in g4], ROW_ALIGN, PACK_ALIGN)[0] for k in range(4)])
    g_halves = g_packed.reshape(4, 2, rows // 2, FLAT_W)
    g_red = reduce_scatter_grads(g_halves[:, 0], g_halves[:, 1], c_idx, "rest").reshape(rows, FLAT_W)
    g_big.update(zip(REST, _unpack_rows(g_red, offs, [a.shape for a in big_local])))

    small_local = [jnp.stack(gs[n]) if isinstance(gs[n], list) else gs[n] for n in SMALL]
    small_local = [a.reshape(w[n].shape if n != "ssd_conv_w" else (1, SSD_CONV, conv_dim)) for n, a in zip(SMALL, small_local)]
    sp, soffs = _pack_rows(small_local + [loss.reshape(1)], 8)
    sr = allreduce_small(sp, "allreduce_small_grads")
    small_red = _unpack_rows(sr, soffs, [a.shape for a in small_local] + [(1,)])
    loss_total = small_red[-1][0]
    g_small = dict(zip(SMALL, small_red[:-1]))
    g_small["ssd_conv_w"] = lax.dynamic_slice(g_small["ssd_conv_w"], (0, 0, chip * cshard), (1, SSD_CONV, cshard))

    grads, delta, new_m, new_v = {}, {}, {}, {}
    for n in BIG:
        shp = w[n].shape
        grads[n] = g_big[n]
        if n in ROWS_MINOR:
            def view(a, shp=shp):
                return jnp.swapaxes(a, 1, 2).reshape(shp[0] * shp[2], shp[1])

            def back(a, shp=shp):
                return jnp.swapaxes(a.reshape(shp[0], shp[2], shp[1]), 1, 2)
        else:
            def view(a, shp=shp):
                return a.reshape(shp[0] * shp[1], shp[2])

            def back(a, shp=shp):
                return a.reshape(shp)
        dl, m2, v2 = adamw(view(w[n]), view(g_big[n]), view(mom[n]), view(var[n]), f"adamw_{n}")
        delta[n], new_m[n], new_v[n] = back(dl), back(m2), back(v2)
    packs = [_pack_rows([src[n] for n in SMALL], 8) for src in (w, g_small, mom, var)]
    dl, m2, v2 = adamw(packs[0][0], packs[1][0], packs[2][0], packs[3][0], "adamw_small")
    shapes = [w[n].shape for n in SMALL]
    for n, a, b, c_ in zip(SMALL, _unpack_rows(dl, packs[0][1], shapes), _unpack_rows(m2, packs[0][1], shapes), _unpack_rows(v2, packs[0][1], shapes)):
        grads[n], delta[n], new_m[n], new_v[n] = g_small[n], a, b, c_
    return (loss_total, grad_x, *[grads[n] for n in WEIGHTS], *[delta[n] for n in WEIGHTS],
            *[new_m[n] for n in WEIGHTS], *[new_v[n] for n in WEIGHTS])
```

```python
import functools

import jax
import jax.numpy as jnp
from jax import lax
from jax.experimental import pallas as pl
from jax.experimental.pallas import tpu as pltpu

F32 = jnp.float32
BF16 = jnp.bfloat16
HI = lax.Precision.HIGHEST

EPS = 1e-6
CHUNK = 64
SSD_HEAD_DIM = 64
SSD_GROUPS = 4
SSD_STATE = 128
SSD_CONV = 4
HGRN_KDIM = 128
FOX_HEAD_DIM = 128
LOG2E = 1.4426950408889634
LANE = 128
FF_ALIGN = 512
FLAT_W = 2048
ROW_ALIGN = 32
PACK_ALIGN = 1024

ADAM_LR = 0.001
ADAM_B1 = 0.9
ADAM_B2 = 0.999
ADAM_EPS = 1e-08
ADAM_WD = 0.01
ADAM_STEP = 10

VMEM_LIMIT = 56 * 1024 * 1024
MESH_ID = pl.DeviceIdType.MESH


def _cparams(sem):
    return pltpu.CompilerParams(dimension_semantics=sem, vmem_limit_bytes=VMEM_LIMIT)


def _pick(n, prefs):
    for t in prefs:
        if n % t == 0:
            return t
    return n


def _dot(a, b, dims, precision=None):
    return lax.dot_general(a, b, (dims, ((), ())), preferred_element_type=F32, precision=precision)


NN = ((1,), (0,))
NT = ((1,), (1,))
TN = ((0,), (0,))


def _sigmoid(x):
    return 1.0 / (1.0 + jnp.exp(-x))


def _silu(x):
    return x * _sigmoid(x)


def _dsilu(x):
    s = _sigmoid(x)
    return s * (1.0 + x * (1.0 - s))


def matmul(a, b, mode, out_dtype, name, scale=None, residual=None):
    if mode == "nn":
        (m, k), (k2, n) = a.shape, b.shape
    elif mode == "nt":
        (m, k), (n, k2) = a.shape, b.shape
    else:
        (k, m), (k2, n) = a.shape, b.shape
    assert k == k2, (a.shape, b.shape, mode)
    tm = _pick(m, (1024, 512, 256, 128))
    tn = _pick(n, (1024, 1408, 512, 256, 128))
    tk = _pick(k, (2048, 1408, 1024, 512, 256, 128))
    nk = k // tk
    dims = {"nn": NN, "nt": NT, "tn": TN}[mode]

    def body(*refs):
        if residual is None:
            a_ref, b_ref, o_ref, acc_ref = refs
            r_ref = None
        else:
            a_ref, b_ref, r_ref, o_ref, acc_ref = refs
        kk = pl.program_id(2)

        @pl.when(kk == 0)
        def _():
            acc_ref[...] = jnp.zeros_like(acc_ref)

        acc_ref[...] += _dot(a_ref[...].astype(BF16), b_ref[...].astype(BF16), dims)

        @pl.when(kk == nk - 1)
        def _():
            r = acc_ref[...]
            if scale is not None:
                r = r * scale
            if r_ref is not None:
                r = r + r_ref[...].astype(F32)
            o_ref[...] = r.astype(o_ref.dtype)

    if mode == "nn":
        a_spec = pl.BlockSpec((tm, tk), lambda i, j, kk: (i, kk))
        b_spec = pl.BlockSpec((tk, tn), lambda i, j, kk: (kk, j))
    elif mode == "nt":
        a_spec = pl.BlockSpec((tm, tk), lambda i, j, kk: (i, kk))
        b_spec = pl.BlockSpec((tn, tk), lambda i, j, kk: (j, kk))
    else:
        a_spec = pl.BlockSpec((tk, tm), lambda i, j, kk: (kk, i))
        b_spec = pl.BlockSpec((tk, tn), lambda i, j, kk: (kk, j))
    o_spec = pl.BlockSpec((tm, tn), lambda i, j, kk: (i, j))
    in_specs = [a_spec, b_spec]
    args = [a, b]
    if residual is not None:
        in_specs.append(o_spec)
        args.append(residual)
    return pl.pallas_call(
        body, name=name, grid=(m // tm, n // tn, nk),
        in_specs=in_specs, out_specs=o_spec,
        out_shape=jax.ShapeDtypeStruct((m, n), out_dtype),
        scratch_shapes=[pltpu.VMEM((tm, tn), F32)],
        compiler_params=_cparams(("parallel", "parallel", "arbitrary")),
    )(*args)


def rowwise(fn, name, rows, consts, outs, accs=(), tm=256):
    rows = [r if isinstance(r, tuple) else (r, r.shape[1], 0) for r in rows]
    t = rows[0][0].shape[0]
    tm = min(tm, t)
    assert t % tm == 0
    n_in = len(rows) + len(consts)
    n_out = len(outs)

    def body(*refs):
        res = fn(*[r[...] for r in refs[:n_in]])
        if not isinstance(res, tuple):
            res = (res,)
        for r, v in zip(refs[n_in:n_in + n_out], res[:n_out]):
            r[...] = v.astype(r.dtype)
        if accs:
            a_refs = refs[n_in + n_out:]
            first = pl.program_id(0) == 0

            @pl.when(first)
            def _():
                for r, v in zip(a_refs, res[n_out:]):
                    r[...] = v

            @pl.when(jnp.logical_not(first))
            def _():
                for r, v in zip(a_refs, res[n_out:]):
                    r[...] += v

    in_specs = [pl.BlockSpec((tm, w), functools.partial(lambda i, cb: (i, cb), cb=cb)) for _, w, cb in rows]
    in_specs += [pl.BlockSpec(c.shape, lambda i: (0, 0)) for c in consts]
    out_specs = [pl.BlockSpec((tm, w), lambda i: (i, 0)) for w, _ in outs]
    out_specs += [pl.BlockSpec(s, lambda i: (0, 0)) for s in accs]
    out_shape = [jax.ShapeDtypeStruct((t, w), d) for w, d in outs]
    out_shape += [jax.ShapeDtypeStruct(s, F32) for s in accs]
    res = pl.pallas_call(
        body, name=name, grid=(t // tm,), in_specs=in_specs, out_specs=out_specs, out_shape=out_shape,
        compiler_params=_cparams(("arbitrary",) if accs else ("parallel",)),
    )(*[r[0] for r in rows], *consts)
    return res


def _rms(x):
    return lax.rsqrt(jnp.mean(x * x, axis=-1, keepdims=True) + EPS)


def _norm_bwd(x, w, dy):
    r = _rms(x)
    xh = x * r
    g = dy * w
    dx = r * (g - xh * jnp.mean(g * xh, axis=-1, keepdims=True))
    return dx, jnp.sum(dy * xh, axis=0, keepdims=True)


def norm_fwd(h, w, name):
    return rowwise(lambda x, w_: x * _rms(x) * w_, name, [h], [w], [(h.shape[1], BF16)])[0]


def norm_bwd_res(h, w, dhn, dres, name):
    def fn(x, dy, dr, w_):
        dx, dw = _norm_bwd(x, w_, dy.astype(F32))
        return dr + dx, dw
    d = h.shape[1]
    return rowwise(fn, name, [h, dhn, dres], [w], [(d, F32)], [(1, d)])


def _mm(name, grid, a, a_spec, b, b_spec, out_shape, o_spec, acc_shape, dims, scale=None, residual=None):
    nk = grid[2]

    def body(*refs):
        a_ref, b_ref = refs[0], refs[1]
        r_ref = refs[2] if residual is not None else None
        o_ref, acc_ref = refs[-2], refs[-1]
        kk = pl.program_id(2)

        @pl.when(kk == 0)
        def _():
            acc_ref[...] = jnp.zeros_like(acc_ref)

        acc_ref[...] += _dot(a_ref[...].astype(BF16), b_ref[...].astype(BF16), dims)

        @pl.when(kk == nk - 1)
        def _():
            r = acc_ref[...]
            if scale is not None:
                r = r * scale
            if r_ref is not None:
                r = r + r_ref[...]
            o_ref[...] = r.astype(o_ref.dtype)

    in_specs, args = [a_spec, b_spec], [a, b]
    if residual is not None:
        in_specs.append(o_spec)
        args.append(residual)
    return pl.pallas_call(
        body, name=name, grid=grid, in_specs=in_specs, out_specs=o_spec, out_shape=out_shape,
        scratch_shapes=[pltpu.VMEM(acc_shape, F32)],
        compiler_params=_cparams(("parallel", "parallel", "arbitrary")),
    )(*args)


def ffn_fwd(h, norm_w, w_in, w_out, layer, tag):
    t, d = h.shape
    ns = w_in.shape[3]
    tm = _pick(t, (1024, 512, 256, 128))
    tk = _pick(d, (1024, 512, 256, 128))
    tn = _pick(d, (1024, 512, 256, 128))
    hn = norm_fwd(h, norm_w, f"{tag}_norm")
    u = _mm(f"{tag}_in", (t // tm, 4, d // tk),
            hn, pl.BlockSpec((tm, tk), lambda i, s, k: (i, k)),
            w_in, pl.BlockSpec((None, None, tk, ns), lambda i, s, k: (layer, s, k, 0)),
            jax.ShapeDtypeStruct((4, t, ns), BF16), pl.BlockSpec((None, tm, ns), lambda i, s, k: (s, i, 0)), (tm, ns), NN)
    u4 = u.reshape(2, 2, t, ns)
    tr = _pick(t, (256, 128))

    def act(u_ref, a_ref):
        a_ref[...] = (_silu(u_ref[0].astype(F32)) * u_ref[1].astype(F32)).astype(a_ref.dtype)

    a = pl.pallas_call(
        act, name=f"{tag}_act", grid=(2, t // tr),
        in_specs=[pl.BlockSpec((2, None, tr, ns), lambda s, i: (0, s, i, 0))],
        out_specs=pl.BlockSpec((None, tr, ns), lambda s, i: (s, i, 0)),
        out_shape=jax.ShapeDtypeStruct((2, t, ns), BF16), compiler_params=_cparams(("parallel", "parallel")),
    )(u4)
    out = _mm(f"{tag}_out", (t // tm, d // tn, 2),
              a, pl.BlockSpec((None, tm, ns), lambda i, j, k: (k, i, 0)),
              w_out, pl.BlockSpec((None, None, ns, tn), lambda i, j, k: (layer, k, 0, j)),
              jax.ShapeDtypeStruct((t, d), F32), pl.BlockSpec((tm, tn), lambda i, j, k: (i, j)), (tm, tn), NN,
              scale=0.5, residual=h)
    return out, (h, hn, u4, a)


def ffn_bwd(dout, saved, norm_w, w_in, w_out, layer, tag):
    h, hn, u4, a = saved
    t, d = h.shape
    ns = w_in.shape[3]
    tm = _pick(t, (1024, 512, 256, 128))
    tk = _pick(d, (1024, 512, 256, 128))
    tn = _pick(d, (1024, 512, 256, 128))
    tkt = _pick(t, (1024, 512, 256, 128))
    th = _pick(d, (512, 256, 128))
    da = _mm(f"{tag}_da", (t // tm, 2, d // tk),
             dout, pl.BlockSpec((tm, tk), lambda i, s, k: (i, k)),
             w_out, pl.BlockSpec((None, None, ns, tk), lambda i, s, k: (layer, s, 0, k)),
             jax.ShapeDtypeStruct((2, t, ns), BF16), pl.BlockSpec((None, tm, ns), lambda i, s, k: (s, i, 0)), (tm, ns), NT,
             scale=0.5)
    tr = _pick(t, (256, 128))

    def dact(da_ref, u_ref, du_ref):
        da_, g, up = da_ref[...].astype(F32), u_ref[0].astype(F32), u_ref[1].astype(F32)
        du_ref[0] = (da_ * up * _dsilu(g)).astype(du_ref.dtype)
        du_ref[1] = (da_ * _silu(g)).astype(du_ref.dtype)

    pair = pl.BlockSpec((2, None, tr, ns), lambda s, i: (0, s, i, 0))
    du = pl.pallas_call(
        dact, name=f"{tag}_dact", grid=(2, t // tr),
        in_specs=[pl.BlockSpec((None, tr, ns), lambda s, i: (s, i, 0)), pair], out_specs=pair,
        out_shape=jax.ShapeDtypeStruct((2, 2, t, ns), BF16), compiler_params=_cparams(("parallel", "parallel")),
    )(da, u4).reshape(4, t, ns)
    dw_out = _mm(f"{tag}_dwout", (2, d // th, t // tkt),
                 a, pl.BlockSpec((None, tkt, ns), lambda s, j, k: (s, k, 0)),
                 dout, pl.BlockSpec((tkt, th), lambda s, j, k: (k, j)),
                 jax.ShapeDtypeStruct((2, ns, d), F32), pl.BlockSpec((None, ns, th), lambda s, j, k: (s, 0, j)), (ns, th), TN,
                 scale=0.5)
    dhn = _mm(f"{tag}_dhn", (t // tm, d // tn, 4),
              du, pl.BlockSpec((None, tm, ns), lambda i, j, k: (k, i, 0)),
              w_in, pl.BlockSpec((None, None, tn, ns), lambda i, j, k: (layer, k, j, 0)),
              jax.ShapeDtypeStruct((t, d), F32), pl.BlockSpec((tm, tn), lambda i, j, k: (i, j)), (tm, tn), NT)
    dw_in = _mm(f"{tag}_dwin", (d // th, 4, t // tkt),
                hn, pl.BlockSpec((tkt, th), lambda i, s, k: (k, i)),
                du, pl.BlockSpec((None, tkt, ns), lambda i, s, k: (s, k, 0)),
                jax.ShapeDtypeStruct((4, d, ns), F32), pl.BlockSpec((None, th, ns), lambda i, s, k: (s, i, 0)), (th, ns), TN)
    dh, dnw = norm_bwd_res(h, norm_w, dhn, dout, f"{tag}_dnorm")
    return dh, dnw, dw_in, dw_out


def ple_fwd(h, p_i, gate_norm_w, w_gate, w_up, post_norm_w, tag):
    d = h.shape[1]
    e0 = matmul(p_i, w_up, "nn", F32, f"{tag}_up")
    hn = norm_fwd(h, gate_norm_w, f"{tag}_norm")
    s = matmul(hn, w_gate, "nn", F32, f"{tag}_gate")
    out = rowwise(lambda x, e, s_, pw: x + e * _rms(e) * pw * _sigmoid(s_), f"{tag}_add",
                  [h, e0, s], [post_norm_w], [(d, F32)])[0]
    return out, (h, hn, e0, s)


def ple_bwd(dout, saved, p_i, gate_norm_w, w_gate, post_norm_w, tag):
    h, hn, e0, s = saved
    d = h.shape[1]

    def fn(dy, e, s_, pw):
        gate = _sigmoid(s_)
        emb = e * _rms(e) * pw
        de0, dpw = _norm_bwd(e, pw, dy * gate)
        return de0, dy * emb * gate * (1.0 - gate), dpw

    de0, ds, dpw = rowwise(fn, f"{tag}_dadd", [dout, e0, s], [post_norm_w], [(d, BF16), (d, BF16)], [(1, d)])
    dw_up = matmul(p_i, de0, "tn", F32, f"{tag}_dwup")
    dhn = matmul(ds, w_gate, "nt", F32, f"{tag}_dhn")
    dw_gate = matmul(hn, ds, "tn", F32, f"{tag}_dwgate")
    dh, dgn = norm_bwd_res(h, gate_norm_w, dhn, dout, f"{tag}_dnorm")
    return dh, dgn, dw_gate, dw_up, dpw


def loss_head(h, target, w):
    d = h.shape[1]

    def fn(x, tgt, w_):
        r = _rms(x)
        err = x * r * w_ - tgt
        dx, dw = _norm_bwd(x, w_, err * (1.0 / d))
        part = 0.5 * jnp.sum(jnp.sum(err * err, axis=-1, keepdims=True), axis=0, keepdims=True) * (1.0 / d)
        return dx, dw, jnp.broadcast_to(part, (1, LANE))

    dh, dw, loss = rowwise(fn, "loss_head", [h, target], [w], [(d, F32)], [(1, d), (1, LANE)])
    return loss[0, 0], dh, dw


def cumsum_rows(x, name, reverse=False):
    t, w = x.shape
    tb = min(256, t)
    nb = t // tb

    def body(x_ref, o_ref, carry):
        @pl.when(pl.program_id(0) == 0)
        def _():
            carry[...] = jnp.zeros_like(carry)

        r = lax.broadcasted_iota(jnp.int32, (tb, tb), 0)
        c = lax.broadcasted_iota(jnp.int32, (tb, tb), 1)
        tri = ((c >= r) if reverse else (c <= r)).astype(F32)
        y = _dot(tri, x_ref[...], NN, HI) + carry[...]
        o_ref[...] = y
        carry[...] = y[0:1, :] if reverse else y[tb - 1:tb, :]

    idx = (lambda i: (nb - 1 - i, 0)) if reverse else (lambda i: (i, 0))
    return pl.pallas_call(
        body, name=name, grid=(nb,), in_specs=[pl.BlockSpec((tb, w), idx)], out_specs=pl.BlockSpec((tb, w), idx),
        out_shape=jax.ShapeDtypeStruct((t, w), F32), scratch_shapes=[pltpu.VMEM((1, w), F32)],
        compiler_params=_cparams(("arbitrary",)),
    )(x)


def _fox_tiles(t):
    return _pick(t, (1024, 512, 256, 128)), _pick(t, (512, 256, 128))


def _fox_pairs(t, resident_is_query):
    tr, ts = _fox_tiles(t)
    rows = []
    for ri in range(t // tr):
        if resident_is_query:
            sis = list(range((ri * tr + tr - 1) // ts + 1))
        else:
            sis = list(range((ri * tr) // ts, t // ts))
        for si in sis:
            q0, k0 = (ri * tr, si * ts) if resident_is_query else (si * ts, ri * tr)
            qn, kn = (tr, ts) if resident_is_query else (ts, tr)
            rows.append((ri, si, si == sis[0], si == sis[-1], q0 < k0 + kn - 1))
    return tuple(jnp.asarray([r[j] for r in rows], jnp.int32) for j in range(5))


def _causal(s, row0, col0, transposed=False):
    r = row0 + lax.broadcasted_iota(jnp.int32, s.shape, 0)
    c = col0 + lax.broadcasted_iota(jnp.int32, s.shape, 1)
    return jnp.where((c >= r) if transposed else (r >= c), s, -jnp.inf)


def _on_diagonal(flag, step):
    @pl.when(flag == 1)
    def _():
        step(True)

    @pl.when(flag == 0)
    def _():
        step(False)


def fox_attn_fwd(qkv, dcol, drow, name):
    t = qkv.shape[0]
    nh = dcol.shape[0]
    tq, tk = _fox_tiles(t)
    pairs = _fox_pairs(t, True)
    scale = FOX_HEAD_DIM ** -0.5

    def body(qs, ks, fs, ls, dg, q_ref, k_ref, v_ref, dc_ref, dr_ref, o_ref, lse_ref, m_s, l_s, acc_s):
        p = pl.program_id(1)

        @pl.when(fs[p] == 1)
        def _():
            m_s[...] = jnp.full_like(m_s, -jnp.inf)
            l_s[...] = jnp.zeros_like(l_s)
            acc_s[...] = jnp.zeros_like(acc_s)

        def step(masked):
            s = _dot(q_ref[...], k_ref[...], NT) * (scale * LOG2E) + dc_ref[0] - dr_ref[0]
            if masked:
                s = _causal(s, qs[p] * tq, ks[p] * tk)
            m_new = jnp.maximum(m_s[...], jnp.max(s, axis=1, keepdims=True))
            alpha = jnp.exp2(m_s[...] - m_new)
            pr = jnp.exp2(s - m_new)
            l_s[...] = alpha * l_s[...] + jnp.sum(pr, axis=1, keepdims=True)
            acc_s[...] = alpha * acc_s[...] + _dot(pr.astype(BF16), v_ref[...], NN)
            m_s[...] = m_new

        _on_diagonal(dg[p], step)

        @pl.when(ls[p] == 1)
        def _():
            o_ref[...] = (acc_s[...] / l_s[...]).astype(o_ref.dtype)
            lse_ref[0] = m_s[...] + jnp.log2(l_s[...])

    hd = FOX_HEAD_DIM
    qcol = pl.BlockSpec((1, tq, 1), lambda h, p, qs, ks, *_: (h, qs[p], 0))
    grid_spec = pltpu.PrefetchScalarGridSpec(
        num_scalar_prefetch=5, grid=(nh, pairs[0].shape[0]),
        in_specs=[
            pl.BlockSpec((tq, hd), lambda h, p, qs, ks, *_: (qs[p], h)),
            pl.BlockSpec((tk, hd), lambda h, p, qs, ks, *_: (ks[p], nh + h)),
            pl.BlockSpec((tk, hd), lambda h, p, qs, ks, *_: (ks[p], 2 * nh + h)),
            qcol,
            pl.BlockSpec((1, 1, tk), lambda h, p, qs, ks, *_: (h, 0, ks[p])),
        ],
        out_specs=[pl.BlockSpec((tq, hd), lambda h, p, qs, ks, *_: (qs[p], h)), qcol],
        scratch_shapes=[pltpu.VMEM((tq, 1), F32), pltpu.VMEM((tq, 1), F32), pltpu.VMEM((tq, hd), F32)])
    return pl.pallas_call(
        body, name=name, grid_spec=grid_spec,
        out_shape=[jax.ShapeDtypeStruct((t, nh * hd), BF16), jax.ShapeDtypeStruct((nh, t, 1), F32)],
        compiler_params=_cparams(("parallel", "arbitrary")),
    )(*pairs, qkv, qkv, qkv, dcol, drow)


def fox_attn_dq(qkv, do, dcol, drow, lse_col, delta_col, name):
    t = qkv.shape[0]
    nh = dcol.shape[0]
    tq, tk = _fox_tiles(t)
    pairs = _fox_pairs(t, True)
    scale = FOX_HEAD_DIM ** -0.5

    def body(qs, ks, fs, ls, dg, q_ref, k_ref, v_ref, do_ref, dc_ref, dr_ref, lse_ref, dl_ref, dq_ref, db_ref, acc_s, db_s):
        p = pl.program_id(1)

        @pl.when(fs[p] == 1)
        def _():
            acc_s[...] = jnp.zeros_like(acc_s)
            db_s[...] = jnp.zeros_like(db_s)

        def step(masked):
            s = _dot(q_ref[...], k_ref[...], NT) * (scale * LOG2E) + dc_ref[0] - dr_ref[0]
            if masked:
                s = _causal(s, qs[p] * tq, ks[p] * tk)
            pr = jnp.exp2(s - lse_ref[0])
            dp = _dot(do_ref[...], v_ref[...], NT)
            ds = pr * (dp - dl_ref[0])
            acc_s[...] += _dot(ds.astype(BF16), k_ref[...], NN)
            db_s[...] += jnp.sum(ds, axis=1, keepdims=True)

        _on_diagonal(dg[p], step)

        @pl.when(ls[p] == 1)
        def _():
            dq_ref[...] = (acc_s[...] * scale).astype(dq_ref.dtype)
            db_ref[0] = db_s[...]

    hd = FOX_HEAD_DIM
    qblk = pl.BlockSpec((tq, hd), lambda h, p, qs, ks, *_: (qs[p], h))
    qcol = pl.BlockSpec((1, tq, 1), lambda h, p, qs, ks, *_: (h, qs[p], 0))
    grid_spec = pltpu.PrefetchScalarGridSpec(
        num_scalar_prefetch=5, grid=(nh, pairs[0].shape[0]),
        in_specs=[
            qblk,
            pl.BlockSpec((tk, hd), lambda h, p, qs, ks, *_: (ks[p], nh + h)),
            pl.BlockSpec((tk, hd), lambda h, p, qs, ks, *_: (ks[p], 2 * nh + h)),
            qblk, qcol,
            pl.BlockSpec((1, 1, tk), lambda h, p, qs, ks, *_: (h, 0, ks[p])),
            qcol, qcol,
        ],
        out_specs=[qblk, qcol],
        scratch_shapes=[pltpu.VMEM((tq, hd), F32), pltpu.VMEM((tq, 1), F32)])
    return pl.pallas_call(
        body, name=name, grid_spec=grid_spec,
        out_shape=[jax.ShapeDtypeStruct((t, nh * hd), BF16), jax.ShapeDtypeStruct((nh, t, 1), F32)],
        compiler_params=_cparams(("parallel", "arbitrary")),
    )(*pairs, qkv, qkv, qkv, do, dcol, drow, lse_col, delta_col)


def fox_attn_dkv(qkv, do, dcol, drow, lse_row, delta_row, name):
    t = qkv.shape[0]
    nh = dcol.shape[0]
    tk, tq = _fox_tiles(t)
    pairs = _fox_pairs(t, False)
    scale = FOX_HEAD_DIM ** -0.5

    def body(ks, qs, fs, ls, dg, q_ref, k_ref, v_ref, do_ref, dc_ref, dr_ref, lse_ref, dl_ref, dk_ref, dv_ref, db_ref, dk_s, dv_s, db_s):
        p = pl.program_id(1)

        @pl.when(fs[p] == 1)
        def _():
            dk_s[...] = jnp.zeros_like(dk_s)
            dv_s[...] = jnp.zeros_like(dv_s)
            db_s[...] = jnp.zeros_like(db_s)

        def step(masked):
            st = _dot(k_ref[...], q_ref[...], NT) * (scale * LOG2E) + dr_ref[0] - dc_ref[0]
            if masked:
                st = _causal(st, ks[p] * tk, qs[p] * tq, transposed=True)
            pt = jnp.exp2(st - lse_ref[0])
            dv_s[...] += _dot(pt.astype(BF16), do_ref[...], NN)
            dpt = _dot(v_ref[...], do_ref[...], NT)
            dst = pt * (dpt - dl_ref[0])
            dk_s[...] += _dot(dst.astype(BF16), q_ref[...], NN)
            db_s[...] -= jnp.sum(dst, axis=1, keepdims=True)

        _on_diagonal(dg[p], step)

        @pl.when(ls[p] == 1)
        def _():
            dk_ref[...] = (dk_s[...] * scale).astype(dk_ref.dtype)
            dv_ref[...] = dv_s[...].astype(dv_ref.dtype)
            db_ref[0] = db_s[...]

    hd = FOX_HEAD_DIM
    qblk = pl.BlockSpec((tq, hd), lambda h, p, ks, qs, *_: (qs[p], h))
    qrow = pl.BlockSpec((1, 1, tq), lambda h, p, ks, qs, *_: (h, 0, qs[p]))
    kcol = pl.BlockSpec((1, tk, 1), lambda h, p, ks, qs, *_: (h, ks[p], 0))
    kv_out = pl.BlockSpec((tk, hd), lambda h, p, ks, qs, *_: (ks[p], h))
    grid_spec = pltpu.PrefetchScalarGridSpec(
        num_scalar_prefetch=5, grid=(nh, pairs[0].shape[0]),
        in_specs=[
            qblk,
            pl.BlockSpec((tk, hd), lambda h, p, ks, qs, *_: (ks[p], nh + h)),
            pl.BlockSpec((tk, hd), lambda h, p, ks, qs, *_: (ks[p], 2 * nh + h)),
            qblk, kcol, qrow, qrow, qrow,
        ],
        out_specs=[kv_out, kv_out, kcol],
        scratch_shapes=[pltpu.VMEM((tk, hd), F32), pltpu.VMEM((tk, hd), F32), pltpu.VMEM((tk, 1), F32)])
    return pl.pallas_call(
        body, name=name, grid_spec=grid_spec,
        out_shape=[jax.ShapeDtypeStruct((t, nh * hd), BF16), jax.ShapeDtypeStruct((t, nh * hd), BF16),
                   jax.ShapeDtypeStruct((nh, t, 1), F32)],
        compiler_params=_cparams(("parallel", "arbitrary")),
    )(*pairs, qkv, qkv, qkv, do, dcol, drow, lse_row, delta_row)


def _head_selector(d, hd):
    return (jnp.arange(d)[:, None] // hd == jnp.arange(LANE)[None, :]).astype(F32)


def _log_sigmoid(x):
    return jnp.minimum(x, 0.0) - jnp.log1p(jnp.exp(-jnp.abs(x)))


def fox_fwd(h, norm_w, w_qkv, w_f, b_f, w_out, tag):
    t, d = h.shape
    nh = d // FOX_HEAD_DIM
    hn = norm_fwd(h, norm_w, f"{tag}_norm")
    qkv = matmul(hn, w_qkv, "nn", BF16, f"{tag}_qkv")
    fr = matmul(hn, w_f, "nn", F32, f"{tag}_f")
    logf = rowwise(lambda x, b: _log_sigmoid(x + b) * LOG2E, f"{tag}_logf", [fr], [b_f], [(LANE, F32)])[0]
    dcum = cumsum_rows(logf, f"{tag}_cum").T[:nh]
    dcol, drow = dcum[:, :, None], dcum[:, None, :]
    o, lse = fox_attn_fwd(qkv, dcol, drow, f"{tag}_attn")
    out = matmul(o, w_out, "nn", F32, f"{tag}_out", residual=h)
    return out, (h, hn, qkv, fr, dcol, drow, o, lse)


def fox_bwd(dout, saved, norm_w, w_qkv, w_f, b_f, w_out, tag):
    h, hn, qkv, fr, dcol, drow, o, lse = saved
    t, d = h.shape
    nh = d // FOX_HEAD_DIM
    do = matmul(dout, w_out, "nt", BF16, f"{tag}_do")
    dw_out = matmul(o, dout, "tn", F32, f"{tag}_dwout")
    sel = _head_selector(d, FOX_HEAD_DIM)
    delta = rowwise(lambda a, b, s: _dot(a.astype(F32) * b.astype(F32), s, NN, HI), f"{tag}_delta",
                    [do, o], [sel], [(LANE, F32)])[0].T[:nh]
    dq, dbias_q = fox_attn_dq(qkv, do, dcol, drow, lse, delta[:, :, None], f"{tag}_dq")
    dk, dv, dbias_k = fox_attn_dkv(qkv, do, dcol, drow, jnp.swapaxes(lse, 1, 2), delta[:, None, :], f"{tag}_dkv")
    dlogf_q = cumsum_rows(jnp.pad(dbias_q[:, :, 0].T, ((0, 0), (0, LANE - nh))), f"{tag}_dcum_q", reverse=True)
    dlogf_k = cumsum_rows(jnp.pad(dbias_k[:, :, 0].T, ((0, 0), (0, LANE - nh))), f"{tag}_dcum_k", reverse=True)

    def dlogf_fn(gq, gk, x, b):
        r = (gq + gk) * _sigmoid(-(x + b))
        return r, jnp.sum(r, axis=0, keepdims=True)

    dfr, db_f = rowwise(dlogf_fn, f"{tag}_dlogf", [dlogf_q, dlogf_k, fr], [b_f], [(LANE, BF16)], [(1, LANE)])
    dqkv = jnp.concatenate([dq, dk, dv], axis=1)
    dhn = matmul(dfr, w_f, "nt", F32, f"{tag}_dhn_f")
    dhn = matmul(dqkv, w_qkv, "nt", F32, f"{tag}_dhn", residual=dhn)
    dw_qkv = matmul(hn, dqkv, "tn", F32, f"{tag}_dwqkv")
    dw_f = matmul(hn, dfr, "tn", F32, f"{tag}_dwf")
    dh, dnw = norm_bwd_res(h, norm_w, dhn, dout, f"{tag}_dnorm")
    return dh, dnw, dw_qkv, dw_f, db_f, dw_out


CONV_ROWS = 256


def _shift_rows(cur, halo, shift, up=False):
    if shift == 0:
        return cur
    n = cur.shape[0]
    row = lax.broadcasted_iota(jnp.int32, cur.shape, 0)
    if up:
        return jnp.where(row >= n - shift, pltpu.roll(halo, n - shift, 0), pltpu.roll(cur, n - shift, 0))
    return jnp.where(row < shift, pltpu.roll(halo, shift, 0), pltpu.roll(cur, shift, 0))


def _conv_pre(x, halo, w, b):
    acc = b + w[SSD_CONV - 1:SSD_CONV] * x
    for k in range(SSD_CONV - 1):
        acc = acc + w[k:k + 1] * _shift_rows(x, halo, SSD_CONV - 1 - k)
    return acc


def conv_fwd(x, w, b, name):
    t, cw = x.shape
    tb = min(CONV_ROWS, t)

    def body(x_ref, w_ref, b_ref, o_ref, halo):
        @pl.when(pl.program_id(0) == 0)
        def _():
            halo[...] = jnp.zeros_like(halo)

        xv = x_ref[...]
        o_ref[...] = _silu(_conv_pre(xv, halo[...], w_ref[...], b_ref[...]))
        halo[...] = xv

    blk = pl.BlockSpec((tb, cw), lambda i: (i, 0))
    return pl.pallas_call(
        body, name=name, grid=(t // tb,),
        in_specs=[blk, pl.BlockSpec(w.shape, lambda i: (0, 0)), pl.BlockSpec(b.shape, lambda i: (0, 0))],
        out_specs=blk, out_shape=jax.ShapeDtypeStruct((t, cw), F32), scratch_shapes=[pltpu.VMEM((tb, cw), F32)],
        compiler_params=_cparams(("arbitrary",)),
    )(x, w, b)


def conv_bwd(x, w, b, dact, name):
    t, cw = x.shape
    tb = min(CONV_ROWS, t)
    nb = t // tb

    def body_pre(x_ref, w_ref, b_ref, da_ref, dpre_ref, dw_ref, db_ref, halo):
        first = pl.program_id(0) == 0

        @pl.when(first)
        def _():
            halo[...] = jnp.zeros_like(halo)

        xv, hv = x_ref[...], halo[...]
        dpre = da_ref[...] * _dsilu(_conv_pre(xv, hv, w_ref[...], b_ref[...]))
        dpre_ref[...] = dpre
        dw = jnp.concatenate([jnp.sum(dpre * _shift_rows(xv, hv, SSD_CONV - 1 - k), axis=0, keepdims=True)
                              for k in range(SSD_CONV)], axis=0)
        db = jnp.sum(dpre, axis=0, keepdims=True)

        @pl.when(first)
        def _():
            dw_ref[...] = dw
            db_ref[...] = db

        @pl.when(jnp.logical_not(first))
        def _():
            dw_ref[...] += dw
            db_ref[...] += db

        halo[...] = xv

    blk = pl.BlockSpec((tb, cw), lambda i: (i, 0))
    wspec = pl.BlockSpec(w.shape, lambda i: (0, 0))
    bspec = pl.BlockSpec(b.shape, lambda i: (0, 0))
    dpre, dw, db = pl.pallas_call(
        body_pre, name=f"{name}_pre", grid=(nb,), in_specs=[blk, wspec, bspec, blk], out_specs=[blk, wspec, bspec],
        out_shape=[jax.ShapeDtypeStruct((t, cw), F32), jax.ShapeDtypeStruct(w.shape, F32), jax.ShapeDtypeStruct(b.shape, F32)],
        scratch_shapes=[pltpu.VMEM((tb, cw), F32)], compiler_params=_cparams(("arbitrary",)),
    )(x, w, b, dact)

    def body_dx(dp_ref, w_ref, dx_ref, halo):
        @pl.when(pl.program_id(0) == 0)
        def _():
            halo[...] = jnp.zeros_like(halo)

        dp, wv = dp_ref[...], w_ref[...]
        acc = wv[SSD_CONV - 1:SSD_CONV] * dp
        for k in range(SSD_CONV - 1):
            acc = acc + wv[k:k + 1] * _shift_rows(dp, halo[...], SSD_CONV - 1 - k, up=True)
        dx_ref[...] = acc.astype(dx_ref.dtype)
        halo[...] = dp

    rblk = pl.BlockSpec((tb, cw), lambda i: (nb - 1 - i, 0))
    dx = pl.pallas_call(
        body_dx, name=f"{name}_dx", grid=(nb,), in_specs=[rblk, wspec], out_specs=rblk,
        out_shape=jax.ShapeDtypeStruct((t, cw), BF16), scratch_shapes=[pltpu.VMEM((tb, cw), F32)],
        compiler_params=_cparams(("arbitrary",)),
    )(dpre, w)
    return dx, dw, db


def _tri(n, upper=False):
    r = lax.broadcasted_iota(jnp.int32, (n, n), 0)
    c = lax.broadcasted_iota(jnp.int32, (n, n), 1)
    return (c >= r) if upper else (r >= c)


def _ssd_decay(dtc, dtr, a):
    low = _tri(CHUNK)[None]
    cumc = jnp.sum(jnp.where(low, dtr * a, 0.0), axis=2, keepdims=True)
    cumr = jnp.sum(jnp.where(_tri(CHUNK, upper=True)[None], dtc * a, 0.0), axis=1, keepdims=True)
    return cumc, cumr


def _bdot(a, b, nt=False):
    dims = (((2,), (2,)), ((0,), (0,))) if nt else (((2,), (1,)), ((0,), (0,)))
    return lax.dot_general(a.astype(BF16), b.astype(BF16), dims, preferred_element_type=F32)


def _ssd_specs(d, hpg):
    l, n, p = CHUNK, SSD_STATE, SSD_HEAD_DIM
    ng = d // LANE
    x3 = pl.BlockSpec((hpg, l, p), lambda g, c: (g, c, 0))
    bsp = pl.BlockSpec((l, n), lambda g, c: (c, ng + g))
    csp = pl.BlockSpec((l, n), lambda g, c: (c, ng + SSD_GROUPS + g))
    dtc = pl.BlockSpec((hpg, l, 1), lambda g, c: (g, c, 0))
    dtr = pl.BlockSpec((hpg, 1, 1, l), lambda g, c: (g, c, 0, 0))
    per_head = pl.BlockSpec((hpg, 1, 1), lambda g, c: (g, 0, 0))
    return x3, bsp, csp, dtc, dtr, per_head


def ssd_intra_fwd(x3, xbc, dtc, dtr, a_log, d_skip, name):
    nh, t, p = x3.shape
    hpg = nh // SSD_GROUPS
    d = nh * p

    def body(x_ref, b_ref, c_ref, dtc_ref, dtr_ref, al_ref, ds_ref, y_ref):
        a = -jnp.exp(al_ref[...])
        cumc, cumr = _ssd_decay(dtc_ref[...], dtr_ref[:, 0], a)
        mdec = jnp.exp(jnp.where(_tri(CHUNK)[None], cumc - cumr, -jnp.inf))
        g = _dot(c_ref[...].astype(BF16), b_ref[...].astype(BF16), NT)
        xv = x_ref[...]
        y_ref[...] = _bdot(g[None] * mdec, xv * dtc_ref[...]) + xv * ds_ref[...]

    x3s, bsp, csp, dtcs, dtrs, ph = _ssd_specs(d, hpg)
    return pl.pallas_call(
        body, name=name, grid=(SSD_GROUPS, t // CHUNK), in_specs=[x3s, bsp, csp, dtcs, dtrs, ph, ph], out_specs=x3s,
        out_shape=jax.ShapeDtypeStruct((nh, t, p), F32), compiler_params=_cparams(("parallel", "parallel")),
    )(x3, xbc, xbc, dtc, dtr, a_log, d_skip)


def ssd_intra_bwd(x3, xbc, dtc, dtr, a_log, d_skip, dy3, name):
    nh, t, p = x3.shape
    hpg = nh // SSD_GROUPS
    d = nh * p
    l, n = CHUNK, SSD_STATE

    def body(x_ref, b_ref, c_ref, dtc_ref, dtr_ref, al_ref, ds_ref, dy_ref,
             dx_ref, ddt_ref, db_ref, dc_ref, dal_ref, dds_ref):
        first = pl.program_id(1) == 0
        a = -jnp.exp(al_ref[...])
        dtc_v = dtc_ref[...]
        cumc, cumr = _ssd_decay(dtc_v, dtr_ref[:, 0], a)
        low = _tri(l)[None]
        mdec = jnp.exp(jnp.where(low, cumc - cumr, -jnp.inf))
        up = _tri(l, upper=True)[None]
        mdec_t = jnp.exp(jnp.where(up, cumr - cumc, -jnp.inf))
        bv, cv = b_ref[...].astype(BF16), c_ref[...].astype(BF16)
        g = _dot(cv, bv, NT)
        g_t = _dot(bv, cv, NT)
        xv, dy = x_ref[...], dy_ref[...]
        xd = xv * dtc_v
        dw = _bdot(dy, xd, nt=True)
        dw_t = _bdot(xd, dy, nt=True)
        dxd = _bdot(g_t[None] * mdec_t, dy)
        dx_ref[...] = dy * ds_ref[...] + dxd * dtc_v
        dg = jnp.sum(dw * mdec, axis=0)
        dg_t = jnp.sum(dw_t * mdec_t, axis=0)
        dc_ref[0] = _dot(dg.astype(BF16), bv, NN)
        db_ref[0] = _dot(dg_t.astype(BF16), cv, NN)
        e = dw * mdec * g[None]
        e_t = dw_t * mdec_t * g_t[None]
        dcum_r = jnp.sum(e_t, axis=1, keepdims=True) - jnp.sum(e, axis=1, keepdims=True)
        dda = jnp.sum(jnp.where(up, dcum_r, 0.0), axis=2, keepdims=True)
        ddt_ref[...] = jnp.sum(dxd * xv, axis=2, keepdims=True) + dda * a
        dal = jnp.sum(dda * dtc_v, axis=1, keepdims=True) * a
        dds = jnp.sum(jnp.sum(dy * xv, axis=2, keepdims=True), axis=1, keepdims=True)

        @pl.when(first)
        def _():
            dal_ref[...] = dal
            dds_ref[...] = dds

        @pl.when(jnp.logical_not(first))
        def _():
            dal_ref[...] += dal
            dds_ref[...] += dds

    x3s, bsp, csp, dtcs, dtrs, ph = _ssd_specs(d, hpg)
    grp = pl.BlockSpec((1, l, n), lambda g, c: (g, c, 0))
    return pl.pallas_call(
        body, name=name, grid=(SSD_GROUPS, t // l),
        in_specs=[x3s, bsp, csp, dtcs, dtrs, ph, ph, x3s], out_specs=[x3s, dtcs, grp, grp, ph, ph],
        out_shape=[jax.ShapeDtypeStruct((nh, t, p), F32), jax.ShapeDtypeStruct((nh, t, 1), F32),
                   jax.ShapeDtypeStruct((SSD_GROUPS, t, n), F32), jax.ShapeDtypeStruct((SSD_GROUPS, t, n), F32),
                   jax.ShapeDtypeStruct((nh, 1, 1), F32), jax.ShapeDtypeStruct((nh, 1, 1), F32)],
        compiler_params=_cparams(("parallel", "arbitrary")),
    )(x3, xbc, xbc, dtc, dtr, a_log, d_skip, dy3)


def _scan_rows(x, reverse=False):
    n = x.shape[0]
    row = lax.broadcasted_iota(jnp.int32, x.shape, 0)
    s = 1
    while s < n:
        if reverse:
            x = x + jnp.where(row < n - s, pltpu.roll(x, n - s, 0), 0.0)
        else:
            x = x + jnp.where(row >= s, pltpu.roll(x, s, 0), 0.0)
        s *= 2
    return x


def _ssd_state_common(dt, a_lane):
    l = CHUNK
    cum = _scan_rows(dt * a_lane)
    cend = cum[l - 1:l]
    return cum, jnp.exp(cum), jnp.exp(cend - cum), jnp.exp(cend)


def ssd_state_fwd(xbc, dt_lane, a_lane, name):
    t = xbc.shape[0]
    d = dt_lane.shape[1]
    gw = d // SSD_GROUPS
    l, n = CHUNK, SSD_STATE
    nc = t // l
    ng = d // LANE

    def body(x_ref, b_ref, c_ref, dt_ref, a_ref, y_ref, sp_ref, s_s):
        @pl.when(pl.program_id(1) == 0)
        def _():
            s_s[...] = jnp.zeros_like(s_s)

        s_prev = s_s[...]
        for j in range(cb):
            rows = slice(j * l, (j + 1) * l)
            xv, dt = x_ref[rows, :], dt_ref[rows, :]
            cum, ec, te, cd = _ssd_state_common(dt, a_ref[...])
            sp_ref[j] = s_prev.astype(BF16)
            y_ref[rows, :] = _dot(c_ref[rows, :].astype(BF16), s_prev.astype(BF16), NN) * ec
            xt = (xv * dt * te).astype(BF16)
            s_prev = s_prev * cd + _dot(b_ref[rows, :].astype(BF16), xt, TN)
        s_s[...] = s_prev

    cb = _pick(nc, SCAN_CHUNKS)
    xs = pl.BlockSpec((cb * l, gw), lambda g, c: (c, g))
    return pl.pallas_call(
        body, name=name, grid=(SSD_GROUPS, nc // cb),
        in_specs=[xs, pl.BlockSpec((cb * l, n), lambda g, c: (c, ng + g)),
                  pl.BlockSpec((cb * l, n), lambda g, c: (c, ng + SSD_GROUPS + g)),
                  xs, pl.BlockSpec((1, gw), lambda g, c: (0, g))],
        out_specs=[xs, pl.BlockSpec((cb, n, gw), lambda g, c: (c, 0, g))],
        out_shape=[jax.ShapeDtypeStruct((t, d), F32), jax.ShapeDtypeStruct((nc, n, d), BF16)],
        scratch_shapes=[pltpu.VMEM((n, gw), F32)],
        compiler_params=_cparams(("parallel", "arbitrary")),
    )(xbc, xbc, xbc, dt_lane, a_lane)


def ssd_state_bwd(xbc, dt_lane, a_lane, s_prev_all, dy, name):
    t = xbc.shape[0]
    d = dt_lane.shape[1]
    gw = d // SSD_GROUPS
    l, n = CHUNK, SSD_STATE
    nc = t // l
    ng = d // LANE

    def body(x_ref, b_ref, c_ref, dt_ref, a_ref, sp_ref, dy_ref, dx_ref, ddt_ref, db_ref, dc_ref, da_ref, ds_s):
        first = pl.program_id(1) == 0

        @pl.when(first)
        def _():
            ds_s[...] = jnp.zeros_like(ds_s)

        a_lane_v = a_ref[...]
        ds_next = ds_s[...]
        da = jnp.zeros_like(a_lane_v)
        last = lax.broadcasted_iota(jnp.int32, (l, gw), 0) == l - 1
        for j in reversed(range(cb)):
            rows = slice(j * l, (j + 1) * l)
            xv, dt = x_ref[rows, :], dt_ref[rows, :]
            cum, ec, te, cd = _ssd_state_common(dt, a_lane_v)
            bv, cv = b_ref[rows, :].astype(BF16), c_ref[rows, :].astype(BF16)
            s_prev = sp_ref[j]
            dyv = dy_ref[rows, :]
            z = _dot(cv, s_prev, NN)
            dz = (dyv * ec).astype(BF16)
            dc_ref[rows, :] = _dot(dz, s_prev, NT)
            xd = xv * dt
            dxt = _dot(bv, ds_next.astype(BF16), NN)
            db_ref[rows, :] = _dot((xd * te).astype(BF16), ds_next.astype(BF16), NT)
            dcd = jnp.sum(ds_next * s_prev.astype(F32), axis=0, keepdims=True)
            dte_te = dxt * xd * te
            dcum = dyv * z * ec - dte_te + jnp.where(last, jnp.sum(dte_te, axis=0, keepdims=True) + dcd * cd, 0.0)
            dda = _scan_rows(dcum, reverse=True)
            dxd = dxt * te
            dx_ref[rows, :] = dxd * dt
            ddt_ref[rows, :] = dxd * xv + dda * a_lane_v
            da = da + jnp.sum(dda * dt, axis=0, keepdims=True)
            ds_next = ds_next * cd + _dot(cv, dz, TN)
        ds_s[...] = ds_next

        @pl.when(first)
        def _():
            da_ref[...] = da

        @pl.when(jnp.logical_not(first))
        def _():
            da_ref[...] += da

    cb = _pick(nc, SCAN_CHUNKS)
    nb = nc // cb
    rc = lambda c: nb - 1 - c
    xs = pl.BlockSpec((cb * l, gw), lambda g, c: (rc(c), g))
    gs = pl.BlockSpec((cb * l, n), lambda g, c: (rc(c), g))
    return pl.pallas_call(
        body, name=name, grid=(SSD_GROUPS, nb),
        in_specs=[xs, pl.BlockSpec((cb * l, n), lambda g, c: (rc(c), ng + g)),
                  pl.BlockSpec((cb * l, n), lambda g, c: (rc(c), ng + SSD_GROUPS + g)),
                  xs, pl.BlockSpec((1, gw), lambda g, c: (0, g)),
                  pl.BlockSpec((cb, n, gw), lambda g, c: (rc(c), 0, g)), xs],
        out_specs=[xs, xs, gs, gs, pl.BlockSpec((1, gw), lambda g, c: (0, g))],
        out_shape=[jax.ShapeDtypeStruct((t, d), F32), jax.ShapeDtypeStruct((t, d), F32),
                   jax.ShapeDtypeStruct((t, SSD_GROUPS * n), F32), jax.ShapeDtypeStruct((t, SSD_GROUPS * n), F32),
                   jax.ShapeDtypeStruct((1, d), F32)],
        scratch_shapes=[pltpu.VMEM((n, gw), F32)],
        compiler_params=_cparams(("parallel", "arbitrary")),
    )(xbc, xbc, xbc, dt_lane, a_lane, s_prev_all, dy)


SCAN_CHUNKS = (8, 4, 2, 1)


def _hgrn_common(q, fr, lb):
    l = CHUNK
    sig = _sigmoid(fr)
    f = lb + (1.0 - lb) * sig
    kk = 1.0 - f
    cum = _scan_rows(jnp.log(f))
    mid = cum[l // 2 - 1:l // 2]
    cend = cum[l - 1:l]
    qf = _silu(q)
    eq, ek, ee, ec = jnp.exp(cum - mid), jnp.exp(mid - cum), jnp.exp(cend - cum), jnp.exp(cum)
    return sig, f, kk, qf, eq, ek, ee, ec, cend, jnp.exp(cend)


def hgrn_fwd(qfvg, lb, name):
    t = qfvg.shape[0]
    d = lb.shape[1]
    l, kd = CHUNK, HGRN_KDIM
    nh = d // kd
    nc = t // l

    cb = _pick(nc, SCAN_CHUNKS)

    def body(q_ref, f_ref, v_ref, lb_ref, o_ref, sp_ref, s_s):
        @pl.when(pl.program_id(1) == 0)
        def _():
            s_s[...] = jnp.zeros_like(s_s)

        s_prev = s_s[...]
        for j in range(cb):
            rows = slice(j * l, (j + 1) * l)
            sig, f, kk, qf, eq, ek, ee, ec, cend, cd = _hgrn_common(q_ref[rows, :], f_ref[rows, :], lb_ref[...])
            v = v_ref[rows, :].astype(BF16)
            sp_ref[j] = s_prev.astype(BF16)
            att = jnp.where(_tri(l), _dot((qf * eq).astype(BF16), (kk * ek).astype(BF16), NT), 0.0)
            o_ref[rows, :] = _dot(att.astype(BF16), v, NN) + _dot((qf * ec).astype(BF16), s_prev.astype(BF16), NT)
            s_prev = s_prev * cd + _dot(v, (kk * ee).astype(BF16), TN)
        s_s[...] = s_prev

    def col(j):
        return pl.BlockSpec((cb * l, kd), lambda h, c: (c, j * nh + h))

    return pl.pallas_call(
        body, name=name, grid=(nh, nc // cb),
        in_specs=[col(0), col(1), col(2), pl.BlockSpec((1, kd), lambda h, c: (0, h))],
        out_specs=[col(0), pl.BlockSpec((cb, kd, kd), lambda h, c: (c, h, 0))],
        out_shape=[jax.ShapeDtypeStruct((t, d), F32), jax.ShapeDtypeStruct((nc, d, kd), BF16)],
        scratch_shapes=[pltpu.VMEM((kd, kd), F32)],
        compiler_params=_cparams(("parallel", "arbitrary")),
    )(qfvg, qfvg, qfvg, lb)


def hgrn_bwd(qfvg, lb, s_prev_all, do, name):
    t = qfvg.shape[0]
    d = lb.shape[1]
    l, kd = CHUNK, HGRN_KDIM
    nh = d // kd
    nc = t // l
    cb = _pick(nc, SCAN_CHUNKS)
    nb = nc // cb

    def body(q_ref, f_ref, v_ref, lb_ref, sp_ref, do_ref, dq_ref, df_ref, dv_ref, dlb_ref, ds_s):
        first = pl.program_id(1) == 0

        @pl.when(first)
        def _():
            ds_s[...] = jnp.zeros_like(ds_s)

        lbv = lb_ref[...]
        ds_next = ds_s[...]
        dlb = jnp.zeros_like(lbv)
        low = _tri(l)
        row = lax.broadcasted_iota(jnp.int32, (l, kd), 0)
        for j in reversed(range(cb)):
            rows = slice(j * l, (j + 1) * l)
            q = q_ref[rows, :]
            sig, f, kk, qf, eq, ek, ee, ec, cend, cd = _hgrn_common(q, f_ref[rows, :], lbv)
            v = v_ref[rows, :].astype(BF16)
            dov = do_ref[rows, :].astype(BF16)
            s_prev = sp_ref[j]
            ds_b = ds_next.astype(BF16)
            qr, kr, ke, qe = qf * eq, kk * ek, kk * ee, qf * ec
            att = jnp.where(low, _dot(qr.astype(BF16), kr.astype(BF16), NT), 0.0).astype(BF16)
            datt = jnp.where(low, _dot(dov, v, NT), 0.0).astype(BF16)
            dqe = _dot(dov, s_prev, NN)
            dke = _dot(v, ds_b, NN)
            dqr = _dot(datt, kr.astype(BF16), NN)
            dkr = _dot(datt, qr.astype(BF16), TN)
            dv_ref[rows, :] = (_dot(ke.astype(BF16), ds_b, NT) + _dot(att, dov, TN)).astype(dv_ref.dtype)
            dcd = jnp.sum(ds_next * s_prev.astype(F32), axis=0, keepdims=True)
            a_q, a_k, a_e, a_c = dqr * qr, dkr * kr, dke * ke, dqe * qe
            dmid = jnp.sum(a_k - a_q, axis=0, keepdims=True)
            dcend = jnp.sum(a_e, axis=0, keepdims=True) + dcd * cd
            dcum = a_q - a_k - a_e + a_c + jnp.where(row == l // 2 - 1, dmid, 0.0) + jnp.where(row == l - 1, dcend, 0.0)
            dlf = _scan_rows(dcum, reverse=True)
            df = dlf / f - (dkr * ek + dke * ee)
            df_ref[rows, :] = (df * (1.0 - lbv) * sig * (1.0 - sig)).astype(df_ref.dtype)
            dq_ref[rows, :] = ((dqr * eq + dqe * ec) * _dsilu(q)).astype(dq_ref.dtype)
            dlb = dlb + jnp.sum(df * (1.0 - sig), axis=0, keepdims=True)
            ds_next = ds_next * cd + _dot(dov, qe.astype(BF16), TN)
        ds_s[...] = ds_next

        @pl.when(first)
        def _():
            dlb_ref[...] = dlb

        @pl.when(jnp.logical_not(first))
        def _():
            dlb_ref[...] += dlb

    def col(j):
        return pl.BlockSpec((cb * l, kd), lambda h, c: (nb - 1 - c, j * nh + h))

    head = pl.BlockSpec((1, kd), lambda h, c: (0, h))
    return pl.pallas_call(
        body, name=name, grid=(nh, nb),
        in_specs=[col(0), col(1), col(2), head, pl.BlockSpec((cb, kd, kd), lambda h, c: (nb - 1 - c, h, 0)), col(0)],
        out_specs=[col(0), col(0), col(0), head],
        out_shape=[jax.ShapeDtypeStruct((t, d), BF16)] * 3 + [jax.ShapeDtypeStruct((1, d), F32)],
        scratch_shapes=[pltpu.VMEM((kd, kd), F32)],
        compiler_params=_cparams(("parallel", "arbitrary")),
    )(qfvg, qfvg, qfvg, lb, s_prev_all, do)


def _softplus(x):
    return jnp.maximum(x, 0.0) + jnp.log1p(jnp.exp(-jnp.abs(x)))


def _grouped(fn, width, *arrs):
    n = arrs[0].shape[1] // width
    outs = [fn(*[a[:, i * width:(i + 1) * width] for a in arrs]) for i in range(n)]
    if isinstance(outs[0], tuple):
        return tuple(jnp.concatenate([o[j] for o in outs], axis=1) for j in range(len(outs[0])))
    return jnp.concatenate(outs, axis=1)


def mixer_fwd(h, norm_w, wts, conv_w, conv_b, dt_bias, a_log, d_skip, ssd_norm_w, lb, hgrn_norm_w, w_out, tag):
    t, d = h.shape
    nh = d // SSD_HEAD_DIM
    p = SSD_HEAD_DIM
    w_z, w_xbc, w_qfvg, w_dt = wts
    hn = norm_fwd(h, norm_w, f"{tag}_norm")
    z = matmul(hn, w_z, "nn", F32, f"{tag}_z")
    xbc_raw = matmul(hn, w_xbc, "nn", F32, f"{tag}_xbc")
    qfvg = matmul(hn, w_qfvg, "nn", F32, f"{tag}_qfvg")
    dt_raw = matmul(hn, w_dt, "nn", F32, f"{tag}_dt")
    xbc = conv_fwd(xbc_raw, conv_w, conv_b, f"{tag}_conv")
    dt = rowwise(lambda x, b: _softplus(x + b), f"{tag}_softplus", [dt_raw], [dt_bias], [(LANE, F32)])[0]
    dt_h = dt[:, :nh]
    dtr = dt_h.T.reshape(nh, t // CHUNK, 1, CHUNK)
    dtc = dt_h.T[:, :, None]
    dt_lane = jnp.repeat(dt_h, p, axis=1)
    a_lane = jnp.repeat(-jnp.exp(a_log[:, :nh]), p, axis=1)
    al3 = a_log[0, :nh].reshape(nh, 1, 1)
    ds3 = d_skip.reshape(nh, 1, 1)
    x3 = xbc[:, :d].reshape(t, nh, p).transpose(1, 0, 2)
    y3 = ssd_intra_fwd(x3, xbc, dtc, dtr, al3, ds3, f"{tag}_ssd_intra")
    y_off, s_ssd = ssd_state_fwd(xbc, dt_lane, a_lane, f"{tag}_ssd_state")
    y_diag = y3.transpose(1, 0, 2).reshape(t, d)
    o_b, s_hgrn = hgrn_fwd(qfvg, lb, f"{tag}_hgrn")
    gw = d // SSD_GROUPS

    def gate(yd, yo, z_, o, g_, nw_a, nw_b):
        ya = (yd + yo) * _silu(z_)
        ya = _grouped(lambda a, w: a * _rms(a) * w, gw, ya, nw_a)
        yb = _grouped(lambda a, w: a * _rms(a) * w, HGRN_KDIM, o, nw_b) * _silu(g_)
        return jnp.concatenate([ya, yb], axis=1)

    cat = rowwise(gate, f"{tag}_gate", [y_diag, y_off, z, o_b, (qfvg, d, 3)], [ssd_norm_w, hgrn_norm_w], [(2 * d, BF16)], tm=128)[0]
    out = matmul(cat, w_out, "nn", F32, f"{tag}_out", residual=h)
    saved = (h, hn, z, xbc_raw, qfvg, dt_raw, xbc, dtc, dtr, dt_lane, a_lane, al3, ds3, x3, y_diag, y_off, s_ssd, o_b, s_hgrn, cat)
    return out, saved


def mixer_bwd(dout, saved, norm_w, wts, conv_w, conv_b, dt_bias, a_log, ssd_norm_w, lb, hgrn_norm_w, w_out, tag):
    (h, hn, z, xbc_raw, qfvg, dt_raw, xbc, dtc, dtr, dt_lane, a_lane, al3, ds3, x3, y_diag, y_off, s_ssd, o_b, s_hgrn, cat) = saved
    t, d = h.shape
    nh = d // SSD_HEAD_DIM
    p = SSD_HEAD_DIM
    gw = d // SSD_GROUPS
    w_z, w_xbc, w_qfvg, w_dt = wts
    dcat = matmul(dout, w_out, "nt", F32, f"{tag}_dcat")
    dw_out = matmul(cat, dout, "tn", F32, f"{tag}_dwout")

    def gate_bwd(dya_n, dyb_g, yd, yo, z_, o, g_, nw_a, nw_b):
        y = yd + yo
        sz = _silu(z_)
        dya, dnw_a = _grouped(lambda a, w, dy: _norm_bwd(a, w, dy), gw, y * sz, nw_a, dya_n)
        sg = _silu(g_)
        tb = _grouped(lambda a, w: a * _rms(a) * w, HGRN_KDIM, o, nw_b)
        do_, dnw_b = _grouped(lambda a, w, dy: _norm_bwd(a, w, dy), HGRN_KDIM, o, nw_b, dyb_g * sg)
        return dya * sz, dya * y * _dsilu(z_), do_, dyb_g * tb * _dsilu(g_), dnw_a, dnw_b

    dy, dz, do_b, dg, dnw_a, dnw_b = rowwise(
        gate_bwd, f"{tag}_dgate", [(dcat, d, 0), (dcat, d, 1), y_diag, y_off, z, o_b, (qfvg, d, 3)],
        [ssd_norm_w, hgrn_norm_w], [(d, F32), (d, BF16), (d, F32), (d, BF16)], [(1, d), (1, d)], tm=128)
    dq, dfr, dv, dlb = hgrn_bwd(qfvg, lb, s_hgrn, do_b, f"{tag}_dhgrn")
    dqfvg = jnp.concatenate([dq, dfr, dv, dg], axis=1)
    dy3 = dy.reshape(t, nh, p).transpose(1, 0, 2)
    dx3, ddt3, db_a, dc_a, dal_a, dds = ssd_intra_bwd(x3, xbc, dtc, dtr, al3, ds3, dy3, f"{tag}_dssd_intra")
    dx_s, ddt_lane, db_s, dc_s, da_lane = ssd_state_bwd(xbc, dt_lane, a_lane, s_ssd, dy, f"{tag}_dssd_state")
    n = SSD_STATE
    dxbc_act_parts = (dx3.transpose(1, 0, 2).reshape(t, d), dx_s,
                      db_a.transpose(1, 0, 2).reshape(t, SSD_GROUPS * n), db_s,
                      dc_a.transpose(1, 0, 2).reshape(t, SSD_GROUPS * n), dc_s)
    sel = _head_selector(d, p)
    ddt_a = jnp.pad(ddt3[:, :, 0].T, ((0, 0), (0, LANE - nh)))

    def dt_bwd(ddl, dda, x, b, s):
        r = (_dot(ddl, s, NN, HI) + dda) * _sigmoid(x + b)
        return r, jnp.sum(r, axis=0, keepdims=True)

    ddt_raw, ddt_bias = rowwise(dt_bwd, f"{tag}_ddt", [ddt_lane, ddt_a, dt_raw], [dt_bias, sel], [(LANE, BF16)], [(1, LANE)])
    a_pad = -jnp.exp(a_log)
    dal_a_row = jnp.pad(dal_a.reshape(1, nh), ((0, 0), (0, LANE - nh)))
    dalog = rowwise(lambda dal, da, a, s: dal + _dot(da, s, NN, HI) * a, f"{tag}_dalog",
                    [dal_a_row, da_lane, a_pad], [sel], [(LANE, F32)])[0]
    dxbc_act = rowwise(lambda x1, x2, b1, b2, c1, c2: jnp.concatenate([x1 + x2, b1 + b2, c1 + c2], axis=1),
                       f"{tag}_dxbc_sum", list(dxbc_act_parts), [], [(d + 2 * SSD_GROUPS * n, F32)], tm=128)[0]
    dxbc_raw, dconv_w, dconv_b = conv_bwd(xbc_raw, conv_w, conv_b, dxbc_act, f"{tag}_dconv")
    dhn = matmul(dz, w_z, "nt", F32, f"{tag}_dhn_z")
    dhn = matmul(dxbc_raw, w_xbc, "nt", F32, f"{tag}_dhn_xbc", residual=dhn)
    dhn = matmul(dqfvg, w_qfvg, "nt", F32, f"{tag}_dhn_qfvg", residual=dhn)
    dhn = matmul(ddt_raw, w_dt, "nt", F32, f"{tag}_dhn_dt", residual=dhn)
    dw_z = matmul(hn, dz, "tn", F32, f"{tag}_dwz")
    dw_xbc = matmul(hn, dxbc_raw, "tn", F32, f"{tag}_dwxbc")
    dw_qfvg = matmul(hn, dqfvg, "tn", F32, f"{tag}_dwqfvg")
    dw_dt = matmul(hn, ddt_raw, "tn", F32, f"{tag}_dwdt")
    dh, dnw = norm_bwd_res(h, norm_w, dhn, dout, f"{tag}_dnorm")
    grads = dict(mix_norm=dnw, w_z=dw_z, w_xbc=dw_xbc, w_qfvg=dw_qfvg, w_dt=dw_dt, conv_w=dconv_w, conv_b=dconv_b,
                 dt_bias=ddt_bias, a_log=dalog, d_skip=dds.reshape(1, nh), ssd_norm=dnw_a, lb=dlb, hgrn_norm=dnw_b,
                 w_out=dw_out)
    return dh, grads


ANY = pl.BlockSpec(memory_space=pl.ANY)


def _place():
    x, y, c = lax.axis_index("x"), lax.axis_index("y"), lax.axis_index("c")
    chips = [(1 - x, y), (x, 1 - y), (1 - x, 1 - y)]
    return x, y, c, chips


def _rcopy(src, dst, send_sems, recv_sems, k, dev):
    return pltpu.make_async_remote_copy(src_ref=src, dst_ref=dst, send_sem=send_sems.at[k], recv_sem=recv_sems.at[k],
                                        device_id=dev, device_id_type=MESH_ID)


def gather_weights(wsh, name):
    def body(w_ref, o_ref, send_sems, recv_sems):
        x, y, c, chips = _place()
        me = 2 * x + y
        sib = (x, y, 1 - c)
        first = [_rcopy(w_ref.at[c], o_ref.at[c, me], send_sems, recv_sems, j, (cx, cy, c)) for j, (cx, cy) in enumerate(chips)]
        for cp in first:
            cp.start()
        passed = []
        for j, (cx, cy) in enumerate(chips):
            blk = o_ref.at[c, 2 * cx + cy]
            _rcopy(blk, blk, send_sems, recv_sems, j, sib).wait_recv()
            cp = _rcopy(blk, blk, send_sems, recv_sems, 3 + j, sib)
            cp.start()
            passed.append(cp)
        for j, (cx, cy) in enumerate(chips):
            blk = o_ref.at[1 - c, 2 * cx + cy]
            _rcopy(blk, blk, send_sems, recv_sems, 3 + j, sib).wait_recv()
        for cp in first + passed:
            cp.wait_send()

    return pl.pallas_call(
        body, name=name, in_specs=[ANY], out_specs=ANY,
        out_shape=jax.ShapeDtypeStruct((2, 4) + wsh.shape[1:], wsh.dtype),
        scratch_shapes=[pltpu.SemaphoreType.DMA((6,)), pltpu.SemaphoreType.DMA((6,))],
    )(wsh)


def gather_filled(wsh, chip, name):
    return lax.dynamic_update_slice(gather_weights(wsh, name), wsh[:, None], (0, chip) + (0,) * (wsh.ndim - 1))


def exchange_halves(g0, g1, name):
    def body(g0_ref, g1_ref, o_ref, send_sems, recv_sems):
        x, y, c, _ = _place()
        for mine_c, src in ((0, g1_ref), (1, g0_ref)):
            @pl.when(c == mine_c)
            def _():
                cp = _rcopy(src, o_ref, send_sems, recv_sems, 0, (x, y, 1 - c))
                cp.start()
                cp.wait()

    return pl.pallas_call(
        body, name=name, in_specs=[ANY, ANY], out_specs=ANY, out_shape=jax.ShapeDtypeStruct(g0.shape, g0.dtype),
        scratch_shapes=[pltpu.SemaphoreType.DMA((1,)), pltpu.SemaphoreType.DMA((1,))],
    )(g0, g1)


def _rs_tile(r, c):
    if r % 256 == 0:
        return 256, c
    return r, _pick(c, (512, 256, 128))


def add_own_half(g0, g1, other, c_idx, name):
    nchip, r, cdim = g0.shape
    tr, tc = _rs_tile(r, cdim)

    def body(c_ref, a0_ref, a1_ref, b_ref, o_ref):
        own = jnp.where(c_ref[0] == 0, a0_ref[...], a1_ref[...])
        o_ref[...] = (own + b_ref[...]).astype(o_ref.dtype)

    blk = pl.BlockSpec((None, tr, tc), lambda k, i, j, c_ref: (k, i, j))
    grid_spec = pltpu.PrefetchScalarGridSpec(
        num_scalar_prefetch=1, grid=(nchip, r // tr, cdim // tc), in_specs=[blk, blk, blk], out_specs=blk)
    return pl.pallas_call(
        body, name=name, grid_spec=grid_spec, out_shape=jax.ShapeDtypeStruct((nchip, r, cdim), BF16),
        compiler_params=_cparams(("parallel", "parallel", "parallel")),
    )(c_idx, g0, g1, other)


def scatter_to_chips(part, name):
    def body(p_ref, o_ref, send_sems, recv_sems, local_sem):
        x, y, c, chips = _place()
        me = 2 * x + y
        mine = pltpu.make_async_copy(p_ref.at[me], o_ref.at[me], local_sem)
        mine.start()
        cps = [_rcopy(p_ref.at[2 * cx + cy], o_ref.at[me], send_sems, recv_sems, j, (cx, cy, c)) for j, (cx, cy) in enumerate(chips)]
        for cp in cps:
            cp.start()
        for j, (cx, cy) in enumerate(chips):
            blk = o_ref.at[2 * cx + cy]
            _rcopy(blk, blk, send_sems, recv_sems, j, (cx, cy, c)).wait_recv()
        for cp in cps:
            cp.wait_send()
        mine.wait()

    return pl.pallas_call(
        body, name=name, in_specs=[ANY], out_specs=ANY, out_shape=jax.ShapeDtypeStruct(part.shape, part.dtype),
        scratch_shapes=[pltpu.SemaphoreType.DMA((3,)), pltpu.SemaphoreType.DMA((3,)), pltpu.SemaphoreType.DMA],
    )(part)


def sum_chips(slots, name):
    nchip, r, cdim = slots.shape
    tr, tc = _rs_tile(r, cdim)

    def body(s_ref, o_ref):
        acc = s_ref[0].astype(F32)
        for k in range(1, nchip):
            acc = acc + s_ref[k].astype(F32)
        o_ref[...] = acc

    return pl.pallas_call(
        body, name=name, grid=(r // tr, cdim // tc), in_specs=[pl.BlockSpec((nchip, tr, tc), lambda i, j: (0, i, j))],
        out_specs=pl.BlockSpec((tr, tc), lambda i, j: (i, j)), out_shape=jax.ShapeDtypeStruct((r, cdim), F32),
        compiler_params=_cparams(("parallel", "parallel")),
    )(slots)


def share_with_sibling(half, name):
    def body(h_ref, o_ref, send_sems, recv_sems):
        x, y, c, _ = _place()
        cp = _rcopy(h_ref, o_ref.at[c], send_sems, recv_sems, 0, (x, y, 1 - c))
        cp.start()
        blk = o_ref.at[1 - c]
        _rcopy(blk, blk, send_sems, recv_sems, 0, (x, y, 1 - c)).wait_recv()
        cp.wait_send()

    return pl.pallas_call(
        body, name=name, in_specs=[ANY], out_specs=ANY,
        out_shape=jax.ShapeDtypeStruct((2,) + half.shape, half.dtype),
        scratch_shapes=[pltpu.SemaphoreType.DMA((1,)), pltpu.SemaphoreType.DMA((1,))],
    )(half)


def reduce_scatter_grads(g0, g1, c_idx, tag):
    other = exchange_halves(g0, g1, f"rs_exchange_{tag}")
    part = add_own_half(g0, g1, other, c_idx, f"rs_add_{tag}")
    slots = scatter_to_chips(part, f"rs_scatter_{tag}")
    half = sum_chips(slots, f"rs_sum_{tag}")
    return lax.dynamic_update_slice(share_with_sibling(half, f"rs_share_{tag}"), half[None], (c_idx[0], 0, 0))


def allreduce_small(v, name):
    rows, w = v.shape

    def body(v_ref, o_ref, slots, send_sems, recv_sems):
        x, y, c, _ = _place()
        me = 4 * x + 2 * y + c
        slots[me] = v_ref[...]
        cps = []
        for r in range(1, 8):
            dev = (x ^ (r >> 2), y ^ ((r >> 1) & 1), c ^ (r & 1))
            cp = _rcopy(v_ref, slots.at[me], send_sems, recv_sems, r - 1, dev)
            cp.start()
            cps.append(cp)
        for r in range(1, 8):
            blk = slots.at[me ^ r]
            _rcopy(blk, blk, send_sems, recv_sems, r - 1, (x, y, c)).wait_recv()
        for cp in cps:
            cp.wait_send()
        acc = slots[0]
        for k in range(1, 8):
            acc = acc + slots[k]
        o_ref[...] = acc

    vm = pl.BlockSpec(memory_space=pltpu.VMEM)
    return pl.pallas_call(
        body, name=name, in_specs=[vm], out_specs=vm, out_shape=jax.ShapeDtypeStruct((rows, w), F32),
        scratch_shapes=[pltpu.VMEM((8, rows, w), F32), pltpu.SemaphoreType.DMA((7,)), pltpu.SemaphoreType.DMA((7,))],
    )(v)


def adamw(w, g, m, v, name):
    def fn(w_, g_, m_, v_):
        m2 = ADAM_B1 * m_ + (1.0 - ADAM_B1) * g_
        v2 = ADAM_B2 * v_ + (1.0 - ADAM_B2) * (g_ * g_)
        m_hat = m2 / (1.0 - ADAM_B1 ** ADAM_STEP)
        v_hat = v2 / (1.0 - ADAM_B2 ** ADAM_STEP)
        return -ADAM_LR * (m_hat / (jnp.sqrt(v_hat) + ADAM_EPS) + ADAM_WD * w_), m2, v2

    rows, cols = w.shape
    tm = _pick(rows, tuple(r for r in (256, 128, 64, 32, 16, 8) if r * cols <= 256 * 1024))
    if tm >= 32 or cols % LANE:
        return rowwise(fn, name, [w, g, m, v], [], [(cols, F32)] * 3, tm=tm)
    tc = _pick(cols, tuple(c for c in (512, 256, 128) if rows * c <= 512 * 1024))

    def body(w_ref, g_ref, m_ref, v_ref, d_ref, m2_ref, v2_ref):
        d_ref[...], m2_ref[...], v2_ref[...] = fn(w_ref[...], g_ref[...], m_ref[...], v_ref[...])

    blk = pl.BlockSpec((rows, tc), lambda j: (0, j))
    return pl.pallas_call(
        body, name=name, grid=(cols // tc,), in_specs=[blk] * 4, out_specs=[blk] * 3,
        out_shape=[jax.ShapeDtypeStruct((rows, cols), F32)] * 3, compiler_params=_cparams(("parallel",)),
    )(w, g, m, v)


def _pack_rows(arrs, row_align, total_align=1):
    parts, offs, r = [], [], 0
    for a in arrs:
        n = a.size
        nr = -(-n // (FLAT_W * row_align)) * row_align
        parts.append(jnp.pad(a.reshape(-1), (0, nr * FLAT_W - n)).reshape(nr, FLAT_W))
        offs.append(r)
        r += nr
    if r % total_align:
        parts.append(jnp.zeros((-r % total_align, FLAT_W), arrs[0].dtype))
    return jnp.concatenate(parts, axis=0), offs


def _unpack_rows(packed, offs, shapes):
    out = []
    for o, s in zip(offs, shapes):
        n = 1
        for k in s:
            n *= k
        nr = -(-n // FLAT_W)
        out.append(packed[..., o:o + nr, :].reshape(packed.shape[:-2] + (nr * FLAT_W,))[..., :n].reshape(packed.shape[:-2] + tuple(s)))
    return out


BIG = ("ffn1_w_in", "ffn1_w_out", "ab_w_in", "ab_w_out", "fox_w_in", "fox_w_out", "ffn2_w_in", "ffn2_w_out",
       "ple_w_gate", "ple_w_up")
FFN = ("ffn1_w_in", "ffn1_w_out", "ffn2_w_in", "ffn2_w_out")
REST = ("ab_w_in", "ab_w_out", "fox_w_in", "fox_w_out", "ple_w_gate", "ple_w_up")
ROWS_MINOR = ("ffn1_w_in", "ffn2_w_in", "ab_w_in")
COL_SHARDED = ("ffn1_w_in", "ab_w_in", "fox_w_in", "ffn2_w_in", "ple_w_up")
SMALL = ("ffn1_norm", "mix_norm", "ssd_conv_w", "ssd_conv_b", "ssd_dt_bias", "ssd_a_log", "ssd_d", "ssd_norm",
         "hgrn_lb_logits", "hgrn_norm", "fox_b_f", "ffn2_norm", "ple_gate_norm", "ple_norm", "final_norm")
WEIGHTS = ("ffn1_norm", "ffn1_w_in", "ffn1_w_out", "mix_norm", "ab_w_in", "ssd_conv_w", "ssd_conv_b", "ssd_dt_bias",
           "ssd_a_log", "ssd_d", "ssd_norm", "hgrn_lb_logits", "hgrn_norm", "ab_w_out", "fox_w_in", "fox_b_f", "fox_w_out",
           "ffn2_norm", "ffn2_w_in", "ffn2_w_out", "ple_gate_norm", "ple_w_gate", "ple_w_up", "ple_norm", "final_norm")


def _full_from_shards(name, g4):
    if name in COL_SHARDED:
        return jnp.moveaxis(g4, 0, 2).reshape(g4.shape[1], g4.shape[2], 4 * g4.shape[3])
    return jnp.moveaxis(g4, 0, 1).reshape(g4.shape[1], 4 * g4.shape[2], g4.shape[3])


def _shards_from_full(name, full):
    ly, r, c = full.shape
    if name in COL_SHARDED:
        return jnp.moveaxis(full.reshape(ly, r, 4, c // 4), 2, 0)
    return jnp.moveaxis(full.reshape(ly, 4, r // 4, c), 1, 0)


def _pad_to(a, axis, n):
    pad = [(0, 0)] * a.ndim
    pad[axis] = (0, n - a.shape[axis])
    return jnp.pad(a, pad)


def _lane_row(v):
    return _pad_to(v.reshape(1, -1), 1, LANE)


def kernel(x, p, ffn1_norm, ffn1_w_in, ffn1_w_out, mix_norm, ab_w_in, ssd_conv_w, ssd_conv_b, ssd_dt_bias, ssd_a_log, ssd_d, ssd_norm, hgrn_lb_logits, hgrn_norm, ab_w_out, fox_w_in, fox_b_f, fox_w_out, ffn2_norm, ffn2_w_in, ffn2_w_out, ple_gate_norm, ple_w_gate, ple_w_up, ple_norm, final_norm, loss_target, m_ffn1_norm, m_ffn1_w_in, m_ffn1_w_out, m_mix_norm, m_ab_w_in, m_ssd_conv_w, m_ssd_conv_b, m_ssd_dt_bias, m_ssd_a_log, m_ssd_d, m_ssd_norm, m_hgrn_lb_logits, m_hgrn_norm, m_ab_w_out, m_fox_w_in, m_fox_b_f, m_fox_w_out, m_ffn2_norm, m_ffn2_w_in, m_ffn2_w_out, m_ple_gate_norm, m_ple_w_gate, m_ple_w_up, m_ple_norm, m_final_norm, v_ffn1_norm, v_ffn1_w_in, v_ffn1_w_out, v_mix_norm, v_ab_w_in, v_ssd_conv_w, v_ssd_conv_b, v_ssd_dt_bias, v_ssd_a_log, v_ssd_d, v_ssd_norm, v_hgrn_lb_logits, v_hgrn_norm, v_ab_w_out, v_fox_w_in, v_fox_b_f, v_fox_w_out, v_ffn2_norm, v_ffn2_w_in, v_ffn2_w_out, v_ple_gate_norm, v_ple_w_gate, v_ple_w_up, v_ple_norm, v_final_norm):
    given = dict(locals())
    w = {n: given[n] for n in WEIGHTS}
    mom = {n: given["m_" + n] for n in WEIGHTS}
    var = {n: given["v_" + n] for n in WEIGHTS}
    h = x[0]
    t, d = h.shape
    depth = p.shape[0]
    nh_ssd = d // SSD_HEAD_DIM
    nh_fox = d // FOX_HEAD_DIM
    conv_dim = d + 2 * SSD_GROUPS * SSD_STATE
    d_ff = ffn1_w_out.shape[1] * 4
    fp = -(-d_ff // FF_ALIGN) * FF_ALIGN
    xi, yi, ci = lax.axis_index("x"), lax.axis_index("y"), lax.axis_index("c")
    chip = 2 * xi + yi

    assert depth == 2
    ffn_w = {n: gather_filled(w[n].astype(BF16), chip, f"gather_{n}") for n in FFN}
    for n in ("ffn1_w_out", "ffn2_w_out"):
        g = ffn_w[n]
        ffn_w[n] = g.reshape(g.shape[0], 2, 2 * g.shape[2], g.shape[3])
    big_local = [w[n].astype(BF16) for n in REST]
    packed, offs = _pack_rows(big_local, ROW_ALIGN, PACK_ALIGN)
    rows = packed.shape[0]
    gathered = gather_filled(packed.reshape(2, rows // 2, FLAT_W), chip, "gather_rest")
    gathered = jnp.moveaxis(gathered, 0, 1).reshape(4, rows, FLAT_W)
    full = {n: _full_from_shards(n, g4) for n, g4 in zip(REST, _unpack_rows(gathered, offs, [a.shape for a in big_local]))}
    cw_local = ssd_conv_w[0]
    cshard = cw_local.shape[1]
    cw_rows = -(-SSD_CONV * conv_dim // FLAT_W)
    cw_placed = lax.dynamic_update_slice(jnp.zeros((SSD_CONV, conv_dim), F32), cw_local, (0, chip * cshard))
    cw_placed = jnp.where(ci == 0, cw_placed, 0.0)
    cw_packed, _ = _pack_rows([cw_placed], 8)
    conv_w = allreduce_small(cw_packed, "gather_conv_w")[:cw_rows].reshape(-1)[:SSD_CONV * conv_dim].reshape(SSD_CONV, conv_dim)

    ab = full["ab_w_in"][0]
    s = [0, d, d + conv_dim, d + conv_dim + nh_ssd]
    ab_wts = (ab[:, s[0]:s[1]], ab[:, s[1]:s[2]], ab[:, s[3]:], _pad_to(ab[:, s[2]:s[3]], 1, LANE))
    fox = full["fox_w_in"][0]
    fox_qkv, fox_f = fox[:, :3 * d], _pad_to(fox[:, 3 * d:], 1, LANE)
    fox_bias = _lane_row(fox_b_f[0])
    dt_bias, a_log = _lane_row(ssd_dt_bias[0]), _lane_row(ssd_a_log[0])
    lb_soft = rowwise(lambda z: (lambda e: e / jnp.sum(e, axis=0, keepdims=True))(jnp.exp(z - jnp.max(z, axis=0, keepdims=True))),
                      "lb_softmax", [hgrn_lb_logits], [], [(d, F32)], tm=hgrn_lb_logits.shape[0])[0]
    lb = lb_soft[0:1]
    conv_b = ssd_conv_b

    saved = []
    for i in range(depth):
        h, s1 = ffn_fwd(h, ffn1_norm[i:i + 1], ffn_w["ffn1_w_in"], ffn_w["ffn1_w_out"], i, f"l{i}_ffn1")
        if i % 2 == 0:
            h, s2 = mixer_fwd(h, mix_norm[i:i + 1], ab_wts, conv_w, conv_b, dt_bias, a_log, ssd_d[0], ssd_norm, lb, hgrn_norm,
                              full["ab_w_out"][0], f"l{i}_mix")
        else:
            h, s2 = fox_fwd(h, mix_norm[i:i + 1], fox_qkv, fox_f, fox_bias, full["fox_w_out"][0], f"l{i}_fox")
        h, s3 = ffn_fwd(h, ffn2_norm[i:i + 1], ffn_w["ffn2_w_in"], ffn_w["ffn2_w_out"], i, f"l{i}_ffn2")
        h, s4 = ple_fwd(h, p[i, 0], ple_gate_norm[i:i + 1], full["ple_w_gate"][i], full["ple_w_up"][i], ple_norm[i:i + 1], f"l{i}_ple")
        saved.append((s1, s2, s3, s4))
    loss, dh, g_final = loss_head(h, loss_target[0], final_norm.reshape(1, d))

    gb = {n: [None] * w[n].shape[0] for n in BIG}
    gs = {n: [None] * w[n].shape[0] for n in SMALL}
    gs["final_norm"] = g_final[0]

    def ffn_grads(k, i, dw_in, dw_out):
        gb[f"ffn{k}_w_in"][i] = dw_in
        gb[f"ffn{k}_w_out"][i] = dw_out.reshape(4, dw_out.shape[1] // 2, d)

    for i in reversed(range(depth)):
        s1, s2, s3, s4 = saved[i]
        dh, dgn, dwg, dwu, dpn = ple_bwd(dh, s4, p[i, 0], ple_gate_norm[i:i + 1], full["ple_w_gate"][i], ple_norm[i:i + 1], f"l{i}_ple")
        gs["ple_gate_norm"][i], gs["ple_norm"][i] = dgn[0], dpn[0]
        gb["ple_w_gate"][i], gb["ple_w_up"][i] = dwg, dwu
        dh, dnw, dw_in, dw_out = ffn_bwd(dh, s3, ffn2_norm[i:i + 1], ffn_w["ffn2_w_in"], ffn_w["ffn2_w_out"], i, f"l{i}_ffn2")
        gs["ffn2_norm"][i] = dnw[0]
        ffn_grads(2, i, dw_in, dw_out)
        if i % 2 == 0:
            dh, gm = mixer_bwd(dh, s2, mix_norm[i:i + 1], ab_wts, conv_w, conv_b, dt_bias, a_log, ssd_norm, lb, hgrn_norm,
                               full["ab_w_out"][0], f"l{i}_mix")
            gs["mix_norm"][i] = gm["mix_norm"][0]
            q4 = gm["w_qfvg"]
            gb["ab_w_in"][0] = jnp.concatenate([gm["w_z"], gm["w_xbc"], gm["w_dt"][:, :nh_ssd], q4], axis=1)
            gb["ab_w_out"][0] = gm["w_out"]
            gs["ssd_conv_w"][0], gs["ssd_conv_b"][0] = gm["conv_w"], gm["conv_b"][0]
            gs["ssd_dt_bias"][0], gs["ssd_a_log"][0] = gm["dt_bias"][0, :nh_ssd], gm["a_log"][0, :nh_ssd]
            gs["ssd_d"][0], gs["ssd_norm"][0], gs["hgrn_norm"][0] = gm["d_skip"][0], gm["ssd_norm"][0], gm["hgrn_norm"][0]
            dlb = gm["lb"]
        else:
            dh, dnw, dwqkv, dwf, dbf, dwo = fox_bwd(dh, s2, mix_norm[i:i + 1], fox_qkv, fox_f, fox_bias, full["fox_w_out"][0], f"l{i}_fox")
            gs["mix_norm"][i] = dnw[0]
            gb["fox_w_in"][0] = jnp.concatenate([dwqkv, dwf[:, :nh_fox]], axis=1)
            gb["fox_w_out"][0] = dwo
            gs["fox_b_f"][0] = dbf[0, :nh_fox]
        dh, dnw, dw_in, dw_out = ffn_bwd(dh, s1, ffn1_norm[i:i + 1], ffn_w["ffn1_w_in"], ffn_w["ffn1_w_out"], i, f"l{i}_ffn1")
        gs["ffn1_norm"][i] = dnw[0]
        ffn_grads(1, i, dw_in, dw_out)
    grad_x = dh[None]
    first_row = (jnp.arange(hgrn_lb_logits.shape[0]) == 0).astype(F32)[:, None]
    gs["hgrn_lb_logits"] = rowwise(lambda sm, g, e: sm * (e - sm[0:1]) * g, "lb_softmax_bwd",
                                   [lb_soft, jnp.broadcast_to(dlb, lb_soft.shape), jnp.broadcast_to(first_row, lb_soft.shape)],
                                   [], [(d, F32)], tm=lb_soft.shape[0])[0]

    c_idx = ci.reshape(1).astype(jnp.int32)
    g_big = {n: reduce_scatter_grads(gb[n][0], gb[n][1], c_idx, n) for n in FFN}
    g4 = [_shards_from_full(n, jnp.stack(gb[n])) for n in REST]
    g_packed = jnp.stack([_pack_rows([a[k] for a in g4], ROW_ALIGN, PACK_ALIGN)[0] for k in range(4)])
    g_halves = g_packed.reshape(4, 2, rows // 2, FLAT_W)
    g_red = reduce_scatter_grads(g_halves[:, 0], g_halves[:, 1], c_idx, "rest").reshape(rows, FLAT_W)
    g_big.update(zip(REST, _unpack_rows(g_red, offs, [a.shape for a in big_local])))

    small_local = [jnp.stack(gs[n]) if isinstance(gs[n], list) else gs[n] for n in SMALL]
    small_local = [a.reshape(w[n].shape if n != "ssd_conv_w" else (1, SSD_CONV, conv_dim)) for n, a in zip(SMALL, small_local)]
    sp, soffs = _pack_rows(small_local + [loss.reshape(1)], 8)
    sr = allreduce_small(sp, "allreduce_small_grads")
    small_red = _unpack_rows(sr, soffs, [a.shape for a in small_local] + [(1,)])
    loss_total = small_red[-1][0]
    g_small = dict(zip(SMALL, small_red[:-1]))
    g_small["ssd_conv_w"] = lax.dynamic_slice(g_small["ssd_conv_w"], (0, 0, chip * cshard), (1, SSD_CONV, cshard))

    grads, delta, new_m, new_v = {}, {}, {}, {}
    for n in BIG:
        shp = w[n].shape
        grads[n] = g_big[n]
        if n in ROWS_MINOR:
            def view(a, shp=shp):
                return jnp.swapaxes(a, 1, 2).reshape(shp[0] * shp[2], shp[1])

            def back(a, shp=shp):
                return jnp.swapaxes(a.reshape(shp[0], shp[2], shp[1]), 1, 2)
        else:
            def view(a, shp=shp):
                return a.reshape(shp[0] * shp[1], shp[2])

            def back(a, shp=shp):
                return a.reshape(shp)
        dl, m2, v2 = adamw(view(w[n]), view(g_big[n]), view(mom[n]), view(var[n]), f"adamw_{n}")
        delta[n], new_m[n], new_v[n] = back(dl), back(m2), back(v2)
    packs = [_pack_rows([src[n] for n in SMALL], 8) for src in (w, g_small, mom, var)]
    dl, m2, v2 = adamw(packs[0][0], packs[1][0], packs[2][0], packs[3][0], "adamw_small")
    shapes = [w[n].shape for n in SMALL]
    for n, a, b, c_ in zip(SMALL, _unpack_rows(dl, packs[0][1], shapes), _unpack_rows(m2, packs[0][1], shapes), _unpack_rows(v2, packs[0][1], shapes)):
        grads[n], delta[n], new_m[n], new_v[n] = g_small[n], a, b, c_
    return (loss_total, grad_x, *[grads[n] for n in WEIGHTS], *[delta[n] for n in WEIGHTS],
            *[new_m[n] for n in WEIGHTS], *[new_v[n] for n in WEIGHTS])
```
